```python
import math
import jax
import jax.numpy as jnp
from jax import lax
import numpy as np

D_MODEL = 1024
BATCH = 4
SEQ = 4096
DEPTH = 2

GRID_W = 64
CTX_LEN = 256
GLA_HEADS = 4
GLA_DK = 64
GLA_DV = 128
GLA_KEY = GLA_HEADS * GLA_DK
GLA_VAL = GLA_HEADS * GLA_DV
GLA_RANK = 16
GLA_GATE_NORM = 16.0
GLA_CHUNK = 64
S5_WIDTH = 512
S5_GROUP = 16
S5_GROUPS = S5_WIDTH // S5_GROUP
S5_STATE = 64
FFN_DENSE = 2816
N_EXPERTS = 8
TOP_K = 2
FFN_EXPERT = 3584
N_DENSE = (DEPTH + 1) // 2
N_MOE = DEPTH // 2
EPS = 1e-6
IN_SIZES = (GLA_KEY, GLA_VAL, GLA_RANK, GLA_RANK, S5_WIDTH, GLA_KEY, GLA_VAL, D_MODEL, D_MODEL)
IN_OFFSETS = tuple(int(o) for o in np.cumsum(IN_SIZES)[:-1])
IN_WIDTH = int(sum(IN_SIZES))
STATE_WIDTH = GLA_KEY + GLA_VAL + 2 * GLA_RANK + S5_WIDTH

kernel_name = 'hybrid_gla_s5_moe_prefix_dit'


def _rmsnorm(x, w):
    xf = x.astype(jnp.float32)
    y = xf * lax.rsqrt(jnp.mean(xf * xf, axis=-1, keepdims=True) + EPS)
    return (y * w.astype(jnp.float32)).astype(x.dtype)


def _modulate(h, shift, scale):
    return h * (1 + scale) + shift


def _flip(t):
    return jnp.flip(t, axis=1)


def _to_col(t, rows):
    b_, n, ch = t.shape
    return t.reshape(b_, rows, GRID_W, ch).transpose(0, 2, 1, 3).reshape(b_, n, ch)


def _to_row(t, rows):
    b_, n, ch = t.shape
    return t.reshape(b_, GRID_W, rows, ch).transpose(0, 2, 1, 3).reshape(b_, n, ch)


def _gla_chunked(q, k, v, g, s0):
    b_, n, h, dk = q.shape
    dv = v.shape[-1]
    nc = n // GLA_CHUNK
    q, k, v, g = (t.reshape(b_, nc, GLA_CHUNK, h, t.shape[-1]) for t in (q, k, v, g))
    bcum = jnp.cumsum(g, axis=2)
    blast = bcum[:, :, -1:]
    qd = q * jnp.exp(bcum)
    kd = k * jnp.exp(-bcum)
    kl = k * jnp.exp(blast - bcum)
    mask = jnp.tril(jnp.ones((GLA_CHUNK, GLA_CHUNK), bool))
    att = jnp.where(mask, jnp.einsum('bnihd,bnjhd->bnhij', qd, kd), 0.0)
    o = jnp.einsum('bnhij,bnjhe->bnihe', att, v)
    ds = jnp.einsum('bnjhd,bnjhe->nbhde', kl, v)
    decay = jnp.exp(blast[:, :, 0]).transpose(1, 0, 2, 3)

    def step(s, inp):
        d, dsn = inp
        return d[..., None] * s + dsn, s

    s_fin, s_start = lax.scan(step, s0, (decay, ds))
    o = o + jnp.einsum('bnihd,nbhde->bnihe', qd, s_start)
    return o.reshape(b_, n, h, dv), s_fin


def _gla_final_state(k, v, g):
    btot = jnp.cumsum(g, axis=1)
    kl = k * jnp.exp(btot[:, -1:] - btot)
    return jnp.einsum('blhd,blhe->bhde', kl, v)


def _gla_bidir(q, k, v, gf, gb, sf0, sb0):
    of, sf = _gla_chunked(q, k, v, gf, sf0)
    ob, sb = _gla_chunked(_flip(q), _flip(k), _flip(v), _flip(gb), sb0)
    return of + _flip(ob), sf, sb


def _s5_discretize(a_re, a_im, log_dt, b_re, b_im):
    f32 = jnp.float32
    lam = lax.complex(a_re.astype(f32), a_im.astype(f32))
    delta = jnp.exp(log_dt.astype(f32))[:, None]
    lam_bar = jnp.exp(lam * delta)
    b_mat = lax.complex(b_re.astype(f32), b_im.astype(f32))
    b_bar = ((lam_bar - 1) / lam)[..., None] * b_mat
    return lam_bar, b_bar


def _linear_combine(e1, e2):
    a1, b1 = e1
    a2, b2 = e2
    return a1 * a2, a2 * b1 + b2


def _s5_scan(u, lam_bar, b_bar, h0):
    bu = jnp.einsum('blgh,gph->blgp', u.astype(jnp.float32).astype(jnp.complex64), b_bar)
    a = jnp.broadcast_to(lam_bar, (1, u.shape[1]) + lam_bar.shape)
    a_cum, h = lax.associative_scan(_linear_combine, (a, bu), axis=1)
    if h0 is not None:
        h = h + a_cum * h0[:, None]
    return h


def _s5_bidir(u, disc_f, disc_b, h0f, h0b):
    b_, n, _ = u.shape
    ug = u.reshape(b_, n, S5_GROUPS, S5_GROUP)
    hf = _s5_scan(ug, *disc_f, h0f)
    hb = _flip(_s5_scan(_flip(ug), *disc_b, h0b))
    return hf, hb


def _s5_readout(hf, hb, u, c_mat, d_skip):
    b_, n, _ = u.shape
    y = jnp.real(jnp.einsum('blgp,ghp->blgh', hf + hb, c_mat)).reshape(b_, n, S5_WIDTH)
    return y.astype(u.dtype) + d_skip * u


def _state_inputs(k, v, zf, zb, u, p):
    b_, n, _ = k.shape
    gf = jax.nn.log_sigmoid((zf @ p['gla_lr_f'] + p['gla_bias_f']).astype(jnp.float32)) / GLA_GATE_NORM
    gb = jax.nn.log_sigmoid((zb @ p['gla_lr_b'] + p['gla_bias_b']).astype(jnp.float32)) / GLA_GATE_NORM
    hd = lambda t, d: t.reshape(b_, n, GLA_HEADS, d)
    return {'k': hd(k, GLA_DK), 'v': hd(v, GLA_DV), 'gf': hd(gf, GLA_DK), 'gb': hd(gb, GLA_DK), 'u': u}


def _project(h, p, full):
    b_, n, _ = h.shape
    if full:
        k, v, zf, zb, u, q, r, ga, gm = jnp.split(h @ p['w_in'], IN_OFFSETS, axis=-1)
        d = _state_inputs(k, v, zf, zb, u, p)
        d.update(q=q.reshape(b_, n, GLA_HEADS, GLA_DK) * GLA_DK ** -0.5, r=r, ga=ga, gm=gm)
        return d
    k, v, zf, zb, u = jnp.split(h @ p['w_in'][:, :STATE_WIDTH], IN_OFFSETS[:4], axis=-1)
    return _state_inputs(k, v, zf, zb, u, p)


def _merge(o_gla, y_s5, pr, p):
    b_, n = y_s5.shape[:2]
    dt = y_s5.dtype
    o = o_gla * lax.rsqrt(jnp.mean(o_gla * o_gla, axis=-1, keepdims=True) + EPS) * p['gla_norm_w'].astype(jnp.float32)
    o = o.reshape(b_, n, GLA_VAL).astype(dt) * jax.nn.silu(pr['r'])
    ya = o @ p['w_gla_proj']
    s = jax.nn.gelu(y_s5)
    s = s * jax.nn.sigmoid(s @ p['s5_w_glu'] + p['s5_b_glu'])
    yb = s @ p['w_s5_proj']
    m = jax.nn.sigmoid(pr['ga']) * ya + jax.nn.sigmoid(pr['gm']) * yb
    return m @ p['w_out']


def _token_mixer(hl, hc, p, rows, ctx_out):
    disc_f = _s5_discretize(p['s5_a_re_f'], p['s5_a_im_f'], p['s5_log_dt_f'], p['s5_b_re'], p['s5_b_im'])
    disc_b = _s5_discretize(p['s5_a_re_b'], p['s5_a_im_b'], p['s5_log_dt_b'], p['s5_b_re'], p['s5_b_im'])
    c_mat = lax.complex(p['s5_c_re'].astype(jnp.float32), p['s5_c_im'].astype(jnp.float32))
    pc = _project(hc, p, ctx_out)
    hcf, hcb = _s5_bidir(pc['u'], disc_f, disc_b, None, None)
    if ctx_out:
        s0 = jnp.zeros((hc.shape[0], GLA_HEADS, GLA_DK, GLA_DV), jnp.float32)
        oc, sfc, sbc = _gla_bidir(pc['q'], pc['k'], pc['v'], pc['gf'], pc['gb'], s0, s0)
        out_c = _merge(oc, _s5_readout(hcf, hcb, pc['u'], c_mat, p['s5_d']), pc, p)
    else:
        sfc = _gla_final_state(pc['k'], pc['v'], pc['gf'])
        sbc = _gla_final_state(_flip(pc['k']), _flip(pc['v']), _flip(pc['gb']))
        out_c = None
    pl = _project(hl, p, True)
    ol, _, _ = _gla_bidir(pl['q'], pl['k'], pl['v'], pl['gf'], pl['gb'], sfc, sbc)
    ul = _to_col(pl['u'], rows)
    hlf, hlb = _s5_bidir(ul, disc_f, disc_b, hcf[:, -1], hcb[:, 0])
    yl = _to_row(_s5_readout(hlf, hlb, ul, c_mat, p['s5_d']), rows)
    out_l = _merge(ol, yl, pl, p)
    return out_l, out_c


def _swiglu(h, w_gate, w_up, w_down):
    return (jax.nn.silu(h @ w_gate) * (h @ w_up)) @ w_down


def _moe(h, w_router, w_gate, w_up, w_down):
    logits = (h @ w_router).astype(jnp.float32)
    top_v, top_i = lax.top_k(logits, TOP_K)
    weights = jax.nn.softmax(top_v, axis=-1)
    gates = jnp.sum(jax.nn.one_hot(top_i, N_EXPERTS, dtype=jnp.float32) * weights[..., None], axis=-2).astype(h.dtype)
    out = jnp.zeros_like(h)
    for e in range(N_EXPERTS):
        out = out + gates[..., e:e + 1] * _swiglu(h, w_gate[e], w_up[e], w_down[e])
    return out


def setup_inputs(seed: int = 0) -> dict:
    key = jax.random.key(seed)
    ks = iter(jax.random.split(key, 64))
    f32 = jnp.float32
    nrm = lambda shape, scale: scale * jax.random.normal(next(ks), shape, f32)
    D, G, P, H, L_ = D_MODEL, S5_GROUPS, S5_STATE, S5_GROUP, DEPTH
    n_idx = jnp.arange(P, dtype=f32)
    log_dt = lambda: jax.random.uniform(next(ks), (L_, G), f32, math.log(1e-3), math.log(1e-1))
    return {
        'x': nrm((BATCH, SEQ, D), 1.0),
        'c': nrm((BATCH, D), 1.0),
        'ctx': nrm((BATCH, CTX_LEN, D), 1.0),
        'c_ctx': nrm((D,), 1.0),
        'w_mod': nrm((L_, D, 6 * D), 0.5 * D ** -0.5),
        'b_mod': nrm((L_, 6 * D), 0.01),
        'norm1_w': 1.0 + nrm((L_, D), 0.02),
        'norm2_w': 1.0 + nrm((L_, D), 0.02),
        'final_norm_w': 1.0 + nrm((D,), 0.02),
        'w_in': nrm((L_, D, IN_WIDTH), D ** -0.5),
        'gla_lr_f': nrm((L_, GLA_RANK, GLA_KEY), GLA_RANK ** -0.5),
        'gla_lr_b': nrm((L_, GLA_RANK, GLA_KEY), GLA_RANK ** -0.5),
        'gla_bias_f': nrm((L_, GLA_KEY), 0.1),
        'gla_bias_b': nrm((L_, GLA_KEY), 0.1),
        'gla_norm_w': 1.0 + nrm((L_, GLA_DV), 0.02),
        's5_a_re_f': -0.5 + nrm((L_, G, P), 0.01),
        's5_a_im_f': math.pi * n_idx + nrm((L_, G, P), 0.01),
        's5_log_dt_f': log_dt(),
        's5_a_re_b': -0.5 + nrm((L_, G, P), 0.01),
        's5_a_im_b': math.pi * n_idx + nrm((L_, G, P), 0.01),
        's5_log_dt_b': log_dt(),
        's5_b_re': nrm((L_, G, P, H), (2 * H) ** -0.5),
        's5_b_im': nrm((L_, G, P, H), (2 * H) ** -0.5),
        's5_c_re': nrm((L_, G, H, P), 2 ** -0.5),
        's5_c_im': nrm((L_, G, H, P), 2 ** -0.5),
        's5_d': nrm((L_, S5_WIDTH), 1.0),
        's5_w_glu': nrm((L_, S5_WIDTH, S5_WIDTH), S5_WIDTH ** -0.5),
        's5_b_glu': nrm((L_, S5_WIDTH), 0.01),
        'w_gla_proj': nrm((L_, GLA_VAL, D), GLA_VAL ** -0.5),
        'w_s5_proj': nrm((L_, S5_WIDTH, D), S5_WIDTH ** -0.5),
        'w_out': nrm((L_, D, D), D ** -0.5),
        'ffn_w_gate': nrm((N_DENSE, D, FFN_DENSE), D ** -0.5),
        'ffn_w_up': nrm((N_DENSE, D, FFN_DENSE), D ** -0.5),
        'ffn_w_down': nrm((N_DENSE, FFN_DENSE, D), FFN_DENSE ** -0.5),
        'moe_router': nrm((N_MOE, D, N_EXPERTS), D ** -0.5),
        'moe_w_gate': nrm((N_MOE, N_EXPERTS, D, FFN_EXPERT), D ** -0.5),
        'moe_w_up': nrm((N_MOE, N_EXPERTS, D, FFN_EXPERT), D ** -0.5),
        'moe_w_down': nrm((N_MOE, N_EXPERTS, FFN_EXPERT, D), FFN_EXPERT ** -0.5),
    }


def reference(x, c, ctx, c_ctx, w_mod, b_mod, norm1_w, norm2_w, final_norm_w, w_in,
              gla_lr_f, gla_lr_b, gla_bias_f, gla_bias_b, gla_norm_w,
              s5_a_re_f, s5_a_im_f, s5_log_dt_f, s5_a_re_b, s5_a_im_b, s5_log_dt_b,
              s5_b_re, s5_b_im, s5_c_re, s5_c_im, s5_d, s5_w_glu, s5_b_glu,
              w_gla_proj, w_s5_proj, w_out,
              ffn_w_gate, ffn_w_up, ffn_w_down,
              moe_router, moe_w_gate, moe_w_up, moe_w_down):
    D = D_MODEL
    rows = x.shape[1] // GRID_W
    silu_c = jax.nn.silu(c)
    silu_cc = jax.nn.silu(c_ctx)
    lat, cx = x, ctx
    for l in range(DEPTH):
        ctx_out = l < DEPTH - 1
        p = {
            'w_in': w_in[l], 'gla_lr_f': gla_lr_f[l], 'gla_lr_b': gla_lr_b[l],
            'gla_bias_f': gla_bias_f[l], 'gla_bias_b': gla_bias_b[l], 'gla_norm_w': gla_norm_w[l],
            's5_a_re_f': s5_a_re_f[l], 's5_a_im_f': s5_a_im_f[l], 's5_log_dt_f': s5_log_dt_f[l],
            's5_a_re_b': s5_a_re_b[l], 's5_a_im_b': s5_a_im_b[l], 's5_log_dt_b': s5_log_dt_b[l],
            's5_b_re': s5_b_re[l], 's5_b_im': s5_b_im[l], 's5_c_re': s5_c_re[l], 's5_c_im': s5_c_im[l],
            's5_d': s5_d[l], 's5_w_glu': s5_w_glu[l], 's5_b_glu': s5_b_glu[l],
            'w_gla_proj': w_gla_proj[l], 'w_s5_proj': w_s5_proj[l], 'w_out': w_out[l],
        }
        mod = (silu_c @ w_mod[l] + b_mod[l])[:, None, :]
        sh1, sc1, g1, sh2, sc2, g2 = jnp.split(mod, 6, axis=-1)
        n_mod = 6 if ctx_out else 2
        mc = jnp.split(silu_cc @ w_mod[l][:, :n_mod * D] + b_mod[l][:n_mod * D], n_mod)

        def channel_mixer(h):
            if l % 2 == 0:
                i = l // 2
                return _swiglu(h, ffn_w_gate[i], ffn_w_up[i], ffn_w_down[i])
            i = l // 2
            return _moe(h, moe_router[i], moe_w_gate[i], moe_w_up[i], moe_w_down[i])

        hl = _modulate(_rmsnorm(lat, norm1_w[l]), sh1, sc1)
        hc = _modulate(_rmsnorm(cx, norm1_w[l]), mc[0], mc[1])
        ml, mcx = _token_mixer(hl, hc, p, rows, ctx_out)
        lat = lat + g1 * ml
        lat = lat + g2 * channel_mixer(_modulate(_rmsnorm(lat, norm2_w[l]), sh2, sc2))
        if ctx_out:
            cx = cx + mc[2] * mcx
            cx = cx + mc[5] * channel_mixer(_modulate(_rmsnorm(cx, norm2_w[l]), mc[3], mc[4]))
    return _rmsnorm(lat, final_norm_w)
```

```python
import functools
import math

import jax
import jax.numpy as jnp
from jax import lax
from jax.experimental import pallas as pl
from jax.experimental.pallas import tpu as pltpu

F32 = jnp.float32
BF16 = jnp.bfloat16
HIGHEST = lax.Precision.HIGHEST

D_MODEL = 1024
GRID_W = 64
GLA_HEADS = 4
GLA_DK = 64
GLA_DV = 128
GLA_KEY = GLA_HEADS * GLA_DK
GLA_VAL = GLA_HEADS * GLA_DV
GLA_RANK = 16
GLA_GATE_NORM = 16.0
GLA_CHUNK = 64
S5_WIDTH = 512
S5_GROUP = 16
S5_GROUPS = S5_WIDTH // S5_GROUP
S5_STATE = 64
S5_CHUNK = 16
S5_CW = S5_CHUNK * S5_GROUP
S5_SW = 2 * S5_STATE
S5_PAD = 128
N_EXPERTS = 8
EPS = 1e-6
Z_PAD = 128

VMEM_LIMIT = 56 * 1024 * 1024

SEC_K = (0, GLA_KEY)
SEC_V = (SEC_K[1], SEC_K[1] + GLA_VAL)
SEC_U = (SEC_V[1], SEC_V[1] + S5_WIDTH)
SEC_Q = (SEC_U[1], SEC_U[1] + GLA_KEY)
SEC_R = (SEC_Q[1], SEC_Q[1] + GLA_VAL)
SEC_GA = (SEC_R[1], SEC_R[1] + D_MODEL)
SEC_GM = (SEC_GA[1], SEC_GA[1] + D_MODEL)
SEC_Z = (SEC_GM[1], SEC_GM[1] + Z_PAD)
IN_PERM_WIDTH = SEC_Z[1]


def _params(n_axes):
    return pltpu.CompilerParams(dimension_semantics=("arbitrary",) * n_axes,
                                vmem_limit_bytes=VMEM_LIMIT)


def _silu(x):
    return x * jax.nn.sigmoid(x)


def _norm_mod(x, nw, sh, sc):
    y = x * lax.rsqrt(jnp.mean(x * x, axis=-1, keepdims=True) + EPS) * nw
    return y * (1.0 + sc) + sh


def _dot(a, b):
    return jnp.dot(a, b, preferred_element_type=F32)


def _dot_nt(a, b):
    return lax.dot_general(a, b, (((1,), (1,)), ((), ())), preferred_element_type=F32)


def _dot_tn(a, b):
    return lax.dot_general(a, b, (((0,), (0,)), ((), ())), preferred_element_type=F32)


def _mod_kernel(c_ref, w_ref, b_ref, o_ref):
    s = _silu(c_ref[...]).astype(BF16)
    o_ref[0] = _dot(s, w_ref[0].astype(BF16)) + b_ref[0]


def _modulation(cond, w_mod, b_mod):
    depth, d, n = w_mod.shape
    tn = 1536
    return pl.pallas_call(
        _mod_kernel,
        grid=(depth, n // tn),
        in_specs=[pl.BlockSpec((8, d), lambda l, j: (0, 0)),
                  pl.BlockSpec((1, d, tn), lambda l, j: (l, 0, j)),
                  pl.BlockSpec((1, 1, tn), lambda l, j: (l, 0, j))],
        out_specs=pl.BlockSpec((1, 8, tn), lambda l, j: (l, 0, j)),
        out_shape=jax.ShapeDtypeStruct((depth, 8, n), F32),
        compiler_params=_params(2),
        name="modulation",
    )(cond, w_mod, b_mod.reshape(depth, 1, n))


def _inproj_kernel(x_ref, sh_ref, sc_ref, nw_ref, w_ref, lr_ref, lb_ref,
                   k_ref, v_ref, u_ref, q_ref, r_ref, ga_ref, gm_ref, gf_ref, gb_ref):
    h = _norm_mod(x_ref[...], nw_ref[...], sh_ref[...], sc_ref[...]).astype(BF16)

    def sec(s):
        return _dot(h, w_ref[:, s[0]:s[1]])

    k_ref[...] = sec(SEC_K).astype(BF16)
    v_ref[...] = sec(SEC_V).astype(BF16)
    u_ref[...] = sec(SEC_U).astype(BF16)
    q_ref[...] = (sec(SEC_Q) * GLA_DK ** -0.5).astype(BF16)
    r_ref[...] = sec(SEC_R).astype(BF16)
    ga_ref[...] = sec(SEC_GA).astype(BF16)
    gm_ref[...] = sec(SEC_GM).astype(BF16)
    z = sec(SEC_Z).astype(BF16)
    pre = _dot(z, lr_ref[...]) + lb_ref[...]
    g = (jnp.minimum(pre, 0.0) - jnp.log(1.0 + jnp.exp(-jnp.abs(pre)))) * (1.0 / GLA_GATE_NORM)
    gf_ref[...] = g[:, :GLA_KEY]
    gb_ref[...] = g[:, GLA_KEY:]


def _mod_spec(col, tile_row):
    return pl.BlockSpec((None, 1, D_MODEL), lambda i: (tile_row(i), 0, col))


def _in_projection(x_all, mod3, tile_row, tm, norm_w, w_perm, lr, lb):
    t, d = x_all.shape
    widths = (GLA_KEY, GLA_VAL, S5_WIDTH, GLA_KEY, GLA_VAL, D_MODEL, D_MODEL)
    tok = lambda w: pl.BlockSpec((tm, w), lambda i: (i, 0))
    full = lambda a: pl.BlockSpec(a.shape, lambda i: (0,) * a.ndim)
    return pl.pallas_call(
        _inproj_kernel,
        grid=(t // tm,),
        in_specs=[tok(d), _mod_spec(0, tile_row), _mod_spec(1, tile_row),
                  full(norm_w), full(w_perm), full(lr), full(lb)],
        out_specs=[tok(w) for w in widths] + [tok(GLA_KEY), tok(GLA_KEY)],
        out_shape=[jax.ShapeDtypeStruct((t, w), BF16) for w in widths]
        + [jax.ShapeDtypeStruct((t, GLA_KEY), F32)] * 2,
        compiler_params=_params(1),
        name="in_projection",
    )(x_all, mod3, mod3, norm_w, w_perm, lr, lb)


def _gla_chunk(q_ref, k_ref, v_ref, g_ref, o_ref, s_ref, r0, tri_f, tri_mask, last):
    rows = pl.ds(r0, GLA_CHUNK)
    bc = jnp.dot(tri_f, g_ref[rows, :], precision=HIGHEST, preferred_element_type=F32)
    bl = bc[last:last + 1, :]
    q = q_ref[rows, :].astype(F32)
    k = k_ref[rows, :].astype(F32)
    qd = (q * jnp.exp(bc)).astype(BF16)
    kd = (k * jnp.exp(-bc)).astype(BF16)
    kl = (k * jnp.exp(bl - bc)).astype(BF16)
    dec = jnp.exp(bl)
    v = v_ref[rows, :]
    outs = []
    for h in range(GLA_HEADS):
        ks = slice(h * GLA_DK, (h + 1) * GLA_DK)
        vs = slice(h * GLA_DV, (h + 1) * GLA_DV)
        att = jnp.where(tri_mask, _dot_nt(qd[:, ks], kd[:, ks]), 0.0).astype(BF16)
        st = s_ref[h]
        outs.append(_dot(att, v[:, vs]) + _dot_nt(qd[:, ks], st.astype(BF16)))
        s_ref[h] = st * dec[:, ks] + _dot_tn(v[:, vs], kl[:, ks])
    o_ref[rows, :] = jnp.concatenate(outs, axis=1)


def _gla_kernel(qf_ref, kf_ref, vf_ref, gf_ref, qb_ref, kb_ref, vb_ref, gb_ref,
                of_ref, ob_ref, sf_ref, sb_ref, *, n_chunks):
    @pl.when(pl.program_id(1) == 0)
    def _():
        sf_ref[...] = jnp.zeros_like(sf_ref)
        sb_ref[...] = jnp.zeros_like(sb_ref)

    row = lax.broadcasted_iota(jnp.int32, (GLA_CHUNK, GLA_CHUNK), 0)
    col = lax.broadcasted_iota(jnp.int32, (GLA_CHUNK, GLA_CHUNK), 1)
    lower = col <= row
    upper = col >= row
    for c in range(n_chunks):
        _gla_chunk(qf_ref, kf_ref, vf_ref, gf_ref, of_ref, sf_ref, c * GLA_CHUNK,
                   lower.astype(F32), lower, GLA_CHUNK - 1)
        _gla_chunk(qb_ref, kb_ref, vb_ref, gb_ref, ob_ref, sb_ref, (n_chunks - 1 - c) * GLA_CHUNK,
                   upper.astype(F32), upper, 0)


def _gla(q, k, v, gf, gb, batch, n_lat_blocks, blk):
    t = q.shape[0]
    ctx0 = batch * n_lat_blocks

    def fwd(b, i):
        return (jnp.where(i == 0, ctx0 + b, b * n_lat_blocks + i - 1), 0)

    def bwd(b, i):
        return (jnp.where(i == 0, ctx0 + b, b * n_lat_blocks + n_lat_blocks - i), 0)

    spec = lambda w, m: pl.BlockSpec((blk, w), m)
    state = pltpu.VMEM((GLA_HEADS, GLA_DV, GLA_DK), F32)
    return pl.pallas_call(
        functools.partial(_gla_kernel, n_chunks=blk // GLA_CHUNK),
        grid=(batch, n_lat_blocks + 1),
        in_specs=[spec(GLA_KEY, fwd), spec(GLA_KEY, fwd), spec(GLA_VAL, fwd), spec(GLA_KEY, fwd),
                  spec(GLA_KEY, bwd), spec(GLA_KEY, bwd), spec(GLA_VAL, bwd), spec(GLA_KEY, bwd)],
        out_specs=[spec(GLA_VAL, fwd), spec(GLA_VAL, bwd)],
        out_shape=[jax.ShapeDtypeStruct((t, GLA_VAL), F32)] * 2,
        scratch_shapes=[state, state],
        compiler_params=_params(2),
        name="gla_bidir",
    )(q, k, v, gf, q, k, v, gb)


def _s5_tables(a_re, a_im, log_dt, b_re, b_im, c_re, c_im, reverse):
    g_, p_ = a_re.shape
    c_len = S5_CHUNK
    delta = jnp.exp(log_dt)[:, None]
    zr, zi = a_re * delta, a_im * delta

    def power(n):
        n = jnp.asarray(n, F32).reshape((-1, 1, 1))
        mag = jnp.exp(zr[None] * n)
        return mag * jnp.cos(zi[None] * n), mag * jnp.sin(zi[None] * n)

    lr_, li_ = power(jnp.ones((1,)))
    lr_, li_ = lr_[0], li_[0]
    den = a_re * a_re + a_im * a_im
    xr, xi = lr_ - 1.0, li_
    cr = (xr * a_re + xi * a_im) / den
    ci = (xi * a_re - xr * a_im) / den
    bbr = cr[..., None] * b_re - ci[..., None] * b_im
    bbi = cr[..., None] * b_im + ci[..., None] * b_re

    t_idx = jnp.arange(c_len)
    pr, pi = power(t_idx)
    clr = c_re[None] * pr[:, :, None, :] - c_im[None] * pi[:, :, None, :]
    cli = c_re[None] * pi[:, :, None, :] + c_im[None] * pr[:, :, None, :]
    kern = (jnp.einsum('cghp,gpj->cghj', clr, bbr, precision=HIGHEST)
            - jnp.einsum('cghp,gpj->cghj', cli, bbi, precision=HIGHEST))
    s_i = t_idx[:, None]
    t_i = t_idx[None, :]
    lag = (s_i - t_i) if reverse else (t_i - s_i)
    kt = jnp.where((lag >= 0)[:, :, None, None, None], kern[jnp.clip(lag, 0, c_len - 1)], 0.0)
    toeplitz = kt.transpose(2, 0, 4, 1, 3).reshape(g_, S5_CW, S5_CW)

    e_in = t_idx if reverse else (c_len - 1 - t_idx)
    er, ei = power(e_in)
    sr = er[..., None] * bbr[None] - ei[..., None] * bbi[None]
    si = er[..., None] * bbi[None] + ei[..., None] * bbr[None]
    to_state = jnp.concatenate([sr, si], axis=2).transpose(1, 0, 3, 2).reshape(g_, S5_CW, S5_SW)

    e_out = (c_len - t_idx) if reverse else (t_idx + 1)
    orr, oi = power(e_out)
    qr = c_re[None] * orr[:, :, None, :] - c_im[None] * oi[:, :, None, :]
    qi = c_re[None] * oi[:, :, None, :] + c_im[None] * orr[:, :, None, :]
    from_state = jnp.concatenate([qr, -qi], axis=3).transpose(1, 3, 0, 2).reshape(g_, S5_SW, S5_CW)

    offs = c_len * (2 ** jnp.arange(8))
    wr, wi = power(offs)
    wa = jnp.concatenate([wr, wr], axis=-1)
    wb = jnp.concatenate([-wi, wi], axis=-1)
    decay = jnp.stack([wa, wb], axis=1).reshape(16, g_, S5_SW).transpose(1, 0, 2)
    return toeplitz.astype(BF16), to_state.astype(BF16), from_state.astype(BF16), decay


def _cmul(wa, wb, x2d):
    return wa * x2d + wb * pltpu.roll(x2d, S5_STATE, 1)


def _s5_segment_scan(x, h0, w_ref, scr, reverse):
    nb, n, sw = x.shape
    lo = S5_PAD
    first, last = (n - 1, 0) if reverse else (0, n - 1)
    scr[...] = jnp.zeros_like(scr)
    scr[:, lo:lo + n, :] = x
    if h0 is not None:
        carried = _cmul(w_ref[0:1, :], w_ref[1:2, :], h0.reshape(nb, sw)).reshape(nb, 1, sw)
        scr[:, lo + first:lo + first + 1, :] = x[:, first:first + 1, :] + carried
    off, lvl = 1, 0
    while off < n:
        src = lo + off if reverse else lo - off
        cur = scr[:, lo:lo + n, :]
        sh = scr[:, src:src + n, :].reshape(nb * n, sw)
        upd = _cmul(w_ref[2 * lvl:2 * lvl + 1, :], w_ref[2 * lvl + 1:2 * lvl + 2, :], sh)
        scr[:, lo:lo + n, :] = cur + upd.reshape(nb, n, sw)
        off, lvl = off * 2, lvl + 1
    h_out = scr[:, lo + last:lo + last + 1, :]
    edge = lo + n if reverse else lo - 1
    if h0 is not None:
        scr[:, edge:edge + 1, :] = h0
    start = lo + 1 if reverse else lo - 1
    return scr[:, start:start + n, :], h_out


def _s5_kernel(u_ref, mf_ref, mb_ref, pf_ref, pb_ref, qf_ref, qb_ref, wf_ref, wb_ref,
               y_ref, scr, *, nb, n_ctx, n_lat):
    n_rows = n_ctx + n_lat
    u = u_ref[0]
    y = _dot(u, mf_ref[0]) + _dot(u, mb_ref[0])
    for reverse, p_ref, q_ref, w_ref in ((False, pf_ref, qf_ref, wf_ref), (True, pb_ref, qb_ref, wb_ref)):
        w2 = w_ref.at[0]
        hloc = _dot(u, p_ref[0]).reshape(nb, n_rows, S5_SW)
        hc, h_ctx = _s5_segment_scan(hloc[:, :n_ctx, :], None, w2, scr, reverse)
        hl, _ = _s5_segment_scan(hloc[:, n_ctx:, :], h_ctx, w2, scr, reverse)
        hstart = jnp.concatenate([hc, hl], axis=1).reshape(nb * n_rows, S5_SW)
        y = y + _dot(hstart.astype(BF16), q_ref[0])
    y_ref[0] = y


def _s5(u_grp, tabs_f, tabs_b, nb, n_ctx, n_lat):
    g_, rows, cw = u_grp.shape
    grp = lambda a: pl.BlockSpec((1,) + a.shape[1:], lambda g: (g, 0, 0))
    mf, pf, qf, wf = tabs_f
    mb, pb, qb, wb = tabs_b
    ops = (u_grp, mf, mb, pf, pb, qf, qb, wf, wb)
    return pl.pallas_call(
        functools.partial(_s5_kernel, nb=nb, n_ctx=n_ctx, n_lat=n_lat),
        grid=(g_,),
        in_specs=[grp(a) for a in ops],
        out_specs=pl.BlockSpec((1, rows, cw), lambda g: (g, 0, 0)),
        out_shape=jax.ShapeDtypeStruct((g_, rows, cw), F32),
        scratch_shapes=[pltpu.VMEM((nb, 2 * S5_PAD + n_lat, S5_SW), F32)],
        compiler_params=_params(1),
        name="s5_bidir",
    )(*ops)


def _s5_group_layout(u_all, batch, n_lat, n_ctx):
    rows = n_lat // GRID_W
    rq = rows // S5_CHUNK
    lat = u_all[:batch * n_lat].reshape(batch, rq, S5_CHUNK, GRID_W, S5_GROUPS, S5_GROUP)
    lat = lat.transpose(4, 0, 3, 1, 2, 5).reshape(S5_GROUPS, batch, n_lat // S5_CHUNK, S5_CW)
    ctx = u_all[batch * n_lat:].reshape(batch, n_ctx // S5_CHUNK, S5_CHUNK, S5_GROUPS, S5_GROUP)
    ctx = ctx.transpose(3, 0, 1, 2, 4).reshape(S5_GROUPS, batch, n_ctx // S5_CHUNK, S5_CW)
    return jnp.concatenate([ctx, lat], axis=2).reshape(S5_GROUPS, -1, S5_CW)


def _s5_token_layout(y_grp, batch, n_lat, n_ctx):
    rows = n_lat // GRID_W
    rq = rows // S5_CHUNK
    c_ctx = n_ctx // S5_CHUNK
    y = y_grp.reshape(S5_GROUPS, batch, -1, S5_CHUNK, S5_GROUP)
    ctx = y[:, :, :c_ctx].transpose(1, 2, 3, 0, 4).reshape(batch * n_ctx, S5_WIDTH)
    lat = y[:, :, c_ctx:].reshape(S5_GROUPS, batch, GRID_W, rq, S5_CHUNK, S5_GROUP)
    lat = lat.transpose(1, 3, 4, 2, 0, 5).reshape(batch * n_lat, S5_WIDTH)
    return jnp.concatenate([lat, ctx], axis=0)


def _gelu_tanh(x):
    return 0.5 * x * (1.0 + jnp.tanh(math.sqrt(2.0 / math.pi) * (x + 0.044715 * (x * x * x))))


def _merge_kernel(of_ref, ob_ref, y_ref, u_ref, r_ref, ga_ref, gm_ref, x_ref, g1_ref,
                  gnw_ref, d_ref, wgp_ref, wglu_ref, bglu_ref, wsp_ref, wout_ref, o_ref):
    o = of_ref[...] + ob_ref[...]
    heads = []
    for h in range(GLA_HEADS):
        oh = o[:, h * GLA_DV:(h + 1) * GLA_DV]
        heads.append(oh * lax.rsqrt(jnp.mean(oh * oh, axis=-1, keepdims=True) + EPS) * gnw_ref[...])
    og = jnp.concatenate(heads, axis=1) * _silu(r_ref[...].astype(F32))
    ya = _dot(og.astype(BF16), wgp_ref[...])
    s = _gelu_tanh(y_ref[...] + d_ref[...] * u_ref[...].astype(F32))
    s = s * jax.nn.sigmoid(_dot(s.astype(BF16), wglu_ref[...]) + bglu_ref[...])
    yb = _dot(s.astype(BF16), wsp_ref[...])
    m = jax.nn.sigmoid(ga_ref[...].astype(F32)) * ya + jax.nn.sigmoid(gm_ref[...].astype(F32)) * yb
    o_ref[...] = x_ref[...] + g1_ref[...] * _dot(m.astype(BF16), wout_ref[...])


def _merge(of, ob, y, u, r, ga, gm, x_all, mod3, tile_row, tm, n_tiles, weights):
    t, d = x_all.shape
    tok = lambda w: pl.BlockSpec((tm, w), lambda i: (i, 0))
    full = lambda a: pl.BlockSpec(a.shape, lambda i: (0,) * a.ndim)
    return pl.pallas_call(
        _merge_kernel,
        grid=(n_tiles,),
        in_specs=[tok(GLA_VAL), tok(GLA_VAL), tok(S5_WIDTH), tok(S5_WIDTH), tok(GLA_VAL),
                  tok(d), tok(d), tok(d), _mod_spec(2, tile_row)] + [full(a) for a in weights],
        out_specs=tok(d),
        out_shape=jax.ShapeDtypeStruct((n_tiles * tm, d), F32),
        compiler_params=_params(1),
        name="merge",
    )(of, ob, y, u, r, ga, gm, x_all, mod3, *weights)


def _ffn_kernel(x_ref, sh_ref, sc_ref, g2_ref, nw_ref, wg_ref, wu_ref, wd_ref, o_ref, h_ref, acc_ref):
    f = pl.program_id(1)

    @pl.when(f == 0)
    def _():
        h_ref[...] = _norm_mod(x_ref[...], nw_ref[...], sh_ref[...], sc_ref[...]).astype(BF16)
        acc_ref[...] = jnp.zeros_like(acc_ref)

    h = h_ref[...]
    a = _silu(_dot(h, wg_ref[...])) * _dot(h, wu_ref[...])
    acc_ref[...] += _dot(a.astype(BF16), wd_ref[...])

    @pl.when(f == pl.num_programs(1) - 1)
    def _():
        o_ref[...] = x_ref[...] + g2_ref[...] * acc_ref[...]


def _mod_spec2(col, tile_row):
    return pl.BlockSpec((None, 1, D_MODEL), lambda i, j: (tile_row(i), 0, col))


def _ffn(x_all, mod3, tile_row, tm, norm_w, wg, wu, wd, tf):
    t, d = x_all.shape
    f_dim = wg.shape[1]
    return pl.pallas_call(
        _ffn_kernel,
        grid=(t // tm, f_dim // tf),
        in_specs=[pl.BlockSpec((tm, d), lambda i, j: (i, 0)),
                  _mod_spec2(3, tile_row), _mod_spec2(4, tile_row), _mod_spec2(5, tile_row),
                  pl.BlockSpec(norm_w.shape, lambda i, j: (0, 0)),
                  pl.BlockSpec((d, tf), lambda i, j: (0, j)),
                  pl.BlockSpec((d, tf), lambda i, j: (0, j)),
                  pl.BlockSpec((tf, d), lambda i, j: (j, 0))],
        out_specs=pl.BlockSpec((tm, d), lambda i, j: (i, 0)),
        out_shape=jax.ShapeDtypeStruct((t, d), F32),
        scratch_shapes=[pltpu.VMEM((tm, d), BF16), pltpu.VMEM((tm, d), F32)],
        compiler_params=_params(2),
        name="ffn_dense",
    )(x_all, mod3, mod3, mod3, norm_w, wg, wu, wd)


def _moe_kernel(x_ref, sh_ref, sc_ref, g2_ref, nw_ref, fw_ref, wr_ref, wg_ref, wu_ref, wd_ref,
                o_ref, h_ref, gate_ref, acc_ref):
    e = pl.program_id(1)
    f = pl.program_id(2)
    lane = lax.broadcasted_iota(jnp.int32, gate_ref.shape, 1).astype(F32)

    @pl.when((e == 0) & (f == 0))
    def _():
        hf = _norm_mod(x_ref[...], nw_ref[...], sh_ref[...], sc_ref[...])
        h_ref[...] = hf.astype(BF16)
        acc_ref[...] = jnp.zeros_like(acc_ref)
        logits = jnp.dot(hf, wr_ref[...], precision=HIGHEST, preferred_element_type=F32)
        logits = jnp.where(lane < N_EXPERTS, logits, -jnp.inf)
        m1 = jnp.max(logits, axis=-1, keepdims=True)
        i1 = jnp.min(jnp.where(logits == m1, lane, float(Z_PAD)), axis=-1, keepdims=True)
        rest = jnp.where(lane == i1, -jnp.inf, logits)
        m2 = jnp.max(rest, axis=-1, keepdims=True)
        i2 = jnp.min(jnp.where(rest == m2, lane, float(Z_PAD)), axis=-1, keepdims=True)
        e2 = jnp.exp(m2 - m1)
        w1 = 1.0 / (1.0 + e2)
        gate_ref[...] = jnp.where(lane == i1, w1, 0.0) + jnp.where(lane == i2, e2 * w1, 0.0)

    h = h_ref[...]
    ge = jnp.sum(jnp.where(lane == e.astype(F32), gate_ref[...], 0.0), axis=-1, keepdims=True)
    a = _silu(_dot(h, wg_ref[0])) * _dot(h, wu_ref[0]) * ge
    acc_ref[...] += _dot(a.astype(BF16), wd_ref[0])

    @pl.when((e == pl.num_programs(1) - 1) & (f == pl.num_programs(2) - 1))
    def _():
        y = x_ref[...] + g2_ref[...] * acc_ref[...]
        o_ref[...] = y * lax.rsqrt(jnp.mean(y * y, axis=-1, keepdims=True) + EPS) * fw_ref[...]


def _moe(x_lat, mod3, tile_row, tm, norm_w, final_w, w_router, wg, wu, wd, tf):
    t, d = x_lat.shape
    n_e, _, f_dim = wg.shape
    mspec = lambda col: pl.BlockSpec((None, 1, D_MODEL), lambda i, e, j: (tile_row(i), 0, col))
    const = lambda a: pl.BlockSpec(a.shape, lambda i, e, j: (0,) * a.ndim)
    return pl.pallas_call(
        _moe_kernel,
        grid=(t // tm, n_e, f_dim // tf),
        in_specs=[pl.BlockSpec((tm, d), lambda i, e, j: (i, 0)),
                  mspec(3), mspec(4), mspec(5), const(norm_w), const(final_w), const(w_router),
                  pl.BlockSpec((1, d, tf), lambda i, e, j: (e, 0, j)),
                  pl.BlockSpec((1, d, tf), lambda i, e, j: (e, 0, j)),
                  pl.BlockSpec((1, tf, d), lambda i, e, j: (e, j, 0))],
        out_specs=pl.BlockSpec((tm, d), lambda i, e, j: (i, 0)),
        out_shape=jax.ShapeDtypeStruct((t, d), F32),
        scratch_shapes=[pltpu.VMEM((tm, d), BF16), pltpu.VMEM((tm, Z_PAD), F32), pltpu.VMEM((tm, d), F32)],
        compiler_params=_params(3),
        name="moe",
    )(x_lat, mod3, mod3, mod3, norm_w, final_w, w_router, wg, wu, wd)


def _permute_w_in(w):
    o = 0
    parts = {}
    for name, width in (("k", GLA_KEY), ("v", GLA_VAL), ("zf", GLA_RANK), ("zb", GLA_RANK), ("u", S5_WIDTH),
                        ("q", GLA_KEY), ("r", GLA_VAL), ("ga", D_MODEL), ("gm", D_MODEL)):
        parts[name] = w[:, o:o + width]
        o += width
    pad = jnp.zeros((w.shape[0], Z_PAD - 2 * GLA_RANK), w.dtype)
    cols = [parts[n] for n in ("k", "v", "u", "q", "r", "ga", "gm", "zf", "zb")] + [pad]
    return jnp.concatenate(cols, axis=1).astype(BF16)


def _low_rank_gate(lr_f, lr_b, bias_f, bias_b):
    lr = jnp.zeros((Z_PAD, 2 * GLA_KEY), F32)
    lr = lr.at[:GLA_RANK, :GLA_KEY].set(lr_f).at[GLA_RANK:2 * GLA_RANK, GLA_KEY:].set(lr_b)
    return lr.astype(BF16), jnp.concatenate([bias_f, bias_b])[None, :]


def kernel(x, c, ctx, c_ctx, w_mod, b_mod, norm1_w, norm2_w, final_norm_w, w_in, gla_lr_f, gla_lr_b, gla_bias_f, gla_bias_b, gla_norm_w, s5_a_re_f, s5_a_im_f, s5_log_dt_f, s5_a_re_b, s5_a_im_b, s5_log_dt_b, s5_b_re, s5_b_im, s5_c_re, s5_c_im, s5_d, s5_w_glu, s5_b_glu, w_gla_proj, w_s5_proj, w_out, ffn_w_gate, ffn_w_up, ffn_w_down, moe_router, moe_w_gate, moe_w_up, moe_w_down):
    batch, n_lat, d = x.shape
    n_ctx = ctx.shape[1]
    depth = w_mod.shape[0]
    assert d == D_MODEL and batch < 8 and depth == 2
    assert n_lat % n_ctx == 0 and n_ctx % GLA_CHUNK == 0 and (n_lat // GRID_W) % S5_CHUNK == 0
    t_lat, t_ctx = batch * n_lat, batch * n_ctx
    t_all = t_lat + t_ctx

    tm = math.gcd(512, t_ctx)
    assert n_lat % tm == 0
    lat_tiles_per_batch = n_lat // tm
    n_lat_tiles = t_lat // tm
    tile_row = lambda i: jnp.where(i < n_lat_tiles, i // lat_tiles_per_batch, batch)

    cond = jnp.zeros((8, d), F32).at[:batch].set(c).at[batch].set(c_ctx)
    mod = _modulation(cond, w_mod, b_mod)

    x_all = jnp.concatenate([x.reshape(t_lat, d), ctx.reshape(t_ctx, d)], axis=0)
    out = None
    for l in range(depth):
        last = l == depth - 1
        mod3 = mod[l].reshape(8, 1, 6 * d)
        lr, lb = _low_rank_gate(gla_lr_f[l], gla_lr_b[l], gla_bias_f[l], gla_bias_b[l])
        k, v, u, q, r, ga, gm, gf, gb = _in_projection(
            x_all, mod3, tile_row, tm, norm1_w[l][None, :], _permute_w_in(w_in[l]), lr, lb)

        of, ob = _gla(q, k, v, gf, gb, batch, n_lat // n_ctx, n_ctx)

        tabs_f = _s5_tables(s5_a_re_f[l], s5_a_im_f[l], s5_log_dt_f[l], s5_b_re[l], s5_b_im[l],
                            s5_c_re[l], s5_c_im[l], False)
        tabs_b = _s5_tables(s5_a_re_b[l], s5_a_im_b[l], s5_log_dt_b[l], s5_b_re[l], s5_b_im[l],
                            s5_c_re[l], s5_c_im[l], True)
        y_grp = _s5(_s5_group_layout(u, batch, n_lat, n_ctx), tabs_f, tabs_b,
                    batch, n_ctx // S5_CHUNK, n_lat // S5_CHUNK)
        y = _s5_token_layout(y_grp, batch, n_lat, n_ctx)

        merge_w = (gla_norm_w[l][None, :], s5_d[l][None, :], w_gla_proj[l].astype(BF16),
                   s5_w_glu[l].astype(BF16), s5_b_glu[l][None, :], w_s5_proj[l].astype(BF16),
                   w_out[l].astype(BF16))
        n_tiles = n_lat_tiles if last else t_all // tm
        x_mid = _merge(of, ob, y, u, r, ga, gm, x_all, mod3, tile_row, tm, n_tiles, merge_w)

        if l % 2 == 0:
            i = l // 2
            tm_f = math.gcd(1024, n_ctx * batch)
            row_f = lambda j, tm_f=tm_f: jnp.where(j < t_lat // tm_f, j // (n_lat // tm_f), batch)
            x_all = _ffn(x_mid, mod3, row_f, tm_f, norm2_w[l][None, :], ffn_w_gate[i].astype(BF16),
                         ffn_w_up[i].astype(BF16), ffn_w_down[i].astype(BF16), 1408)
        else:
            i = l // 2
            tm_f = math.gcd(1024, n_lat)
            row_f = lambda j, tm_f=tm_f: j // (n_lat // tm_f)
            w_router = jnp.zeros((d, Z_PAD), F32).at[:, :N_EXPERTS].set(moe_router[i])
            res = _moe(x_mid, mod3, row_f, tm_f, norm2_w[l][None, :], final_norm_w[None, :], w_router,
                       moe_w_gate[i].astype(BF16), moe_w_up[i].astype(BF16), moe_w_down[i].astype(BF16), 896)
            out = res.reshape(batch, n_lat, d)
    return out
```

```python
import functools
import math

import jax
import jax.numpy as jnp
from jax import lax
from jax.experimental import pallas as pl
from jax.experimental.pallas import tpu as pltpu

F32 = jnp.float32
BF16 = jnp.bfloat16
HIGHEST = lax.Precision.HIGHEST

D_MODEL = 1024
GRID_W = 64
GLA_HEADS = 4
GLA_DK = 64
GLA_DV = 128
GLA_KEY = GLA_HEADS * GLA_DK
GLA_VAL = GLA_HEADS * GLA_DV
GLA_RANK = 16
GLA_GATE_NORM = 16.0
GLA_CHUNK = 64
S5_WIDTH = 512
S5_GROUP = 16
S5_GROUPS = S5_WIDTH // S5_GROUP
S5_STATE = 64
S5_CHUNK = 16
S5_CW = S5_CHUNK * S5_GROUP
S5_SW = 2 * S5_STATE
S5_PAD = GRID_W // 2
S5_TILE_GROUPS = 128 // S5_GROUP
S5_GROUP_SHIFT = 4
N_EXPERTS = 8
EPS = 1e-6
Z_PAD = 128

VMEM_LIMIT = 56 * 1024 * 1024

SEC_K = (0, GLA_KEY)
SEC_V = (SEC_K[1], SEC_K[1] + GLA_VAL)
SEC_U = (SEC_V[1], SEC_V[1] + S5_WIDTH)
SEC_Q = (SEC_U[1], SEC_U[1] + GLA_KEY)
SEC_R = (SEC_Q[1], SEC_Q[1] + GLA_VAL)
SEC_GA = (SEC_R[1], SEC_R[1] + D_MODEL)
SEC_GM = (SEC_GA[1], SEC_GA[1] + D_MODEL)
SEC_Z = (SEC_GM[1], SEC_GM[1] + Z_PAD)
IN_PERM_WIDTH = SEC_Z[1]


def _params(n_axes):
    return pltpu.CompilerParams(dimension_semantics=("arbitrary",) * n_axes,
                                vmem_limit_bytes=VMEM_LIMIT)


def _silu(x):
    return x * jax.nn.sigmoid(x)


def _norm_mod(x, nw, sh, sc):
    y = x * lax.rsqrt(jnp.mean(x * x, axis=-1, keepdims=True) + EPS) * nw
    return y * (1.0 + sc) + sh


def _dot(a, b):
    return jnp.dot(a, b, preferred_element_type=F32)


def _dot_nt(a, b):
    return lax.dot_general(a, b, (((1,), (1,)), ((), ())), preferred_element_type=F32)


def _dot_tn(a, b):
    return lax.dot_general(a, b, (((0,), (0,)), ((), ())), preferred_element_type=F32)


def _mod_kernel(c_ref, w_ref, b_ref, o_ref):
    s = _silu(c_ref[...]).astype(BF16)
    o_ref[0] = _dot(s, w_ref[0].astype(BF16)) + b_ref[0]


def _modulation(cond, w_mod, b_mod):
    depth, d, n = w_mod.shape
    tn = 1536
    return pl.pallas_call(
        _mod_kernel,
        grid=(depth, n // tn),
        in_specs=[pl.BlockSpec((8, d), lambda l, j: (0, 0)),
                  pl.BlockSpec((1, d, tn), lambda l, j: (l, 0, j)),
                  pl.BlockSpec((1, 1, tn), lambda l, j: (l, 0, j))],
        out_specs=pl.BlockSpec((1, 8, tn), lambda l, j: (l, 0, j)),
        out_shape=jax.ShapeDtypeStruct((depth, 8, n), F32),
        compiler_params=_params(2),
        name="modulation",
    )(cond, w_mod, b_mod.reshape(depth, 1, n))


def _inproj_kernel(x_ref, sh_ref, sc_ref, nw_ref, w_ref, lr_ref, lb_ref,
                   k_ref, v_ref, u_ref, q_ref, r_ref, ga_ref, gm_ref, gf_ref, gb_ref):
    h = _norm_mod(x_ref[...], nw_ref[...], sh_ref[...], sc_ref[...]).astype(BF16)

    def sec(s):
        return _dot(h, w_ref[:, s[0]:s[1]])

    k_ref[...] = sec(SEC_K).astype(BF16)
    v_ref[...] = sec(SEC_V).astype(BF16)
    u_ref[...] = sec(SEC_U)
    q_ref[...] = (sec(SEC_Q) * GLA_DK ** -0.5).astype(BF16)
    r_ref[...] = sec(SEC_R).astype(BF16)
    ga_ref[...] = sec(SEC_GA).astype(BF16)
    gm_ref[...] = sec(SEC_GM).astype(BF16)
    z = sec(SEC_Z).astype(BF16)
    pre = _dot(z, lr_ref[...]) + lb_ref[...]
    g = (jnp.minimum(pre, 0.0) - jnp.log(1.0 + jnp.exp(-jnp.abs(pre)))) * (1.0 / GLA_GATE_NORM)
    gf_ref[...] = g[:, :GLA_KEY]
    gb_ref[...] = g[:, GLA_KEY:]


def _mod_spec(col, tile_row):
    return pl.BlockSpec((None, 1, D_MODEL), lambda i: (tile_row(i), 0, col))


def _in_projection(x_all, mod3, tile_row, tm, norm_w, w_perm, lr, lb):
    t, d = x_all.shape
    widths = (GLA_KEY, GLA_VAL, S5_WIDTH, GLA_KEY, GLA_VAL, D_MODEL, D_MODEL)
    tok = lambda w: pl.BlockSpec((tm, w), lambda i: (i, 0))
    full = lambda a: pl.BlockSpec(a.shape, lambda i: (0,) * a.ndim)
    return pl.pallas_call(
        _inproj_kernel,
        grid=(t // tm,),
        in_specs=[tok(d), _mod_spec(0, tile_row), _mod_spec(1, tile_row),
                  full(norm_w), full(w_perm), full(lr), full(lb)],
        out_specs=[tok(w) for w in widths] + [tok(GLA_KEY), tok(GLA_KEY)],
        out_shape=[jax.ShapeDtypeStruct((t, w), F32 if i == 2 else BF16) for i, w in enumerate(widths)]
        + [jax.ShapeDtypeStruct((t, GLA_KEY), F32)] * 2,
        compiler_params=_params(1),
        name="in_projection",
    )(x_all, mod3, mod3, norm_w, w_perm, lr, lb)


def _gla_chunk(q_ref, k_ref, v_ref, g_ref, o_ref, s_ref, r0, tri_f, tri_mask, last):
    rows = pl.ds(r0, GLA_CHUNK)
    bc = jnp.dot(tri_f, g_ref[rows, :], precision=HIGHEST, preferred_element_type=F32)
    bl = bc[last:last + 1, :]
    q = q_ref[rows, :].astype(F32)
    k = k_ref[rows, :].astype(F32)
    qd = (q * jnp.exp(bc)).astype(BF16)
    kd = (k * jnp.exp(-bc)).astype(BF16)
    kl = (k * jnp.exp(bl - bc)).astype(BF16)
    dec = jnp.exp(bl)
    v = v_ref[rows, :]
    outs = []
    for h in range(GLA_HEADS):
        ks = slice(h * GLA_DK, (h + 1) * GLA_DK)
        vs = slice(h * GLA_DV, (h + 1) * GLA_DV)
        att = jnp.where(tri_mask, _dot_nt(qd[:, ks], kd[:, ks]), 0.0).astype(BF16)
        st = s_ref[h]
        outs.append(_dot(att, v[:, vs]) + _dot_nt(qd[:, ks], st.astype(BF16)))
        s_ref[h] = st * dec[:, ks] + _dot_tn(v[:, vs], kl[:, ks])
    o_ref[rows, :] = jnp.concatenate(outs, axis=1)


def _gla_kernel(qf_ref, kf_ref, vf_ref, gf_ref, qb_ref, kb_ref, vb_ref, gb_ref,
                of_ref, ob_ref, sf_ref, sb_ref, *, n_chunks):
    @pl.when(pl.program_id(1) == 0)
    def _():
        sf_ref[...] = jnp.zeros_like(sf_ref)
        sb_ref[...] = jnp.zeros_like(sb_ref)

    row = lax.broadcasted_iota(jnp.int32, (GLA_CHUNK, GLA_CHUNK), 0)
    col = lax.broadcasted_iota(jnp.int32, (GLA_CHUNK, GLA_CHUNK), 1)
    lower = col <= row
    upper = col >= row
    for c in range(n_chunks):
        _gla_chunk(qf_ref, kf_ref, vf_ref, gf_ref, of_ref, sf_ref, c * GLA_CHUNK,
                   lower.astype(F32), lower, GLA_CHUNK - 1)
        _gla_chunk(qb_ref, kb_ref, vb_ref, gb_ref, ob_ref, sb_ref, (n_chunks - 1 - c) * GLA_CHUNK,
                   upper.astype(F32), upper, 0)


def _gla(q, k, v, gf, gb, batch, n_lat_blocks, blk):
    t = q.shape[0]
    ctx0 = batch * n_lat_blocks

    def fwd(b, i):
        return (jnp.where(i == 0, ctx0 + b, b * n_lat_blocks + i - 1), 0)

    def bwd(b, i):
        return (jnp.where(i == 0, ctx0 + b, b * n_lat_blocks + n_lat_blocks - i), 0)

    spec = lambda w, m: pl.BlockSpec((blk, w), m)
    state = pltpu.VMEM((GLA_HEADS, GLA_DV, GLA_DK), F32)
    return pl.pallas_call(
        functools.partial(_gla_kernel, n_chunks=blk // GLA_CHUNK),
        grid=(batch, n_lat_blocks + 1),
        in_specs=[spec(GLA_KEY, fwd), spec(GLA_KEY, fwd), spec(GLA_VAL, fwd), spec(GLA_KEY, fwd),
                  spec(GLA_KEY, bwd), spec(GLA_KEY, bwd), spec(GLA_VAL, bwd), spec(GLA_KEY, bwd)],
        out_specs=[spec(GLA_VAL, fwd), spec(GLA_VAL, bwd)],
        out_shape=[jax.ShapeDtypeStruct((t, GLA_VAL), F32)] * 2,
        scratch_shapes=[state, state],
        compiler_params=_params(2),
        name="gla_bidir",
    )(q, k, v, gf, q, k, v, gb)


def _s5_group_tables(arow, acol, log_dt, bt, ct, reverse):
    delta = jnp.exp(log_dt)
    lane_s = lax.broadcasted_iota(jnp.int32, (1, S5_SW), 1)
    sign = jnp.where(lane_s < S5_STATE, -1.0, 1.0)
    ar, ai = arow[0:1, :], arow[1:2, :]
    zr, zi = ar * delta, ai * delta

    def row_power(n):
        mag = jnp.exp(zr * n)
        return mag * jnp.cos(zi * n), mag * jnp.sin(zi * n)

    one = jnp.ones((1, 1), F32)
    l_r, l_i = row_power(one)
    den = ar * ar + ai * ai
    xr, xi = l_r - 1.0, l_i
    cr = (xr * ar + xi * ai) / den
    ci = (xi * ar - xr * ai) / den
    bdup = cr * bt + ci * (pltpu.roll(bt, S5_STATE, 1) * sign)
    bswp = pltpu.roll(bdup, S5_STATE, 1) * sign

    step = lax.broadcasted_iota(jnp.int32, (S5_CHUNK, 1), 0).astype(F32)
    er, ei = row_power(step if reverse else (S5_CHUNK - 1.0 - step))
    to_state = jnp.concatenate(
        [er[s:s + 1, :] * bdup + ei[s:s + 1, :] * bswp for s in range(S5_CHUNK)], axis=0)

    zrc, zic = acol[:, 0:1] * delta, acol[:, 1:2] * delta
    lane_c = lax.broadcasted_iota(jnp.int32, (1, S5_CW), 1)
    t_lane = (lane_c >> S5_GROUP_SHIFT).astype(F32)
    e_k = (S5_CHUNK - 1.0 - t_lane) if reverse else t_lane
    mag = jnp.exp(zrc * e_k)
    pr, pi = mag * jnp.cos(zic * e_k), mag * jnp.sin(zic * e_k)
    ctr, cti = ct[:S5_STATE, :], ct[S5_STATE:, :]
    kr, ki = ctr * pr - cti * pi, ctr * pi + cti * pr
    lam_mag = jnp.exp(zrc)
    lcr, lci = lam_mag * jnp.cos(zic), lam_mag * jnp.sin(zic)
    from_state = jnp.concatenate([kr * lcr - ki * lci, -(kr * lci + ki * lcr)], axis=0)
    krow = jnp.dot(bdup, jnp.concatenate([kr, -ki], axis=0), precision=HIGHEST,
                   preferred_element_type=F32)
    blocks = []
    for s in range(S5_CHUNK):
        if reverse:
            sh = S5_GROUP * (S5_CHUNK - 1 - s)
            blk = pltpu.roll(krow, (S5_CW - sh) % S5_CW, 1) if sh else krow
            blocks.append(jnp.where(lane_c < S5_GROUP * (s + 1), blk, 0.0))
        else:
            sh = S5_GROUP * s
            blk = pltpu.roll(krow, sh, 1) if sh else krow
            blocks.append(jnp.where(lane_c >= sh, blk, 0.0))
    toeplitz = jnp.concatenate(blocks, axis=0)

    wr, wi = row_power(one * float(S5_CHUNK))
    decay = []
    for _ in range(8):
        decay.append((wr, wi * sign))
        wr, wi = wr * wr - wi * wi, 2.0 * wr * wi
    return toeplitz.astype(BF16), to_state.astype(BF16), from_state.astype(BF16), decay


def _cmul(wa, wb, x2d):
    return wa * x2d + wb * pltpu.roll(x2d, S5_STATE, 1)


def _s5_segment_scan(x, h0, decay, lvl0, scr, reverse):
    nb, n, sw = x.shape
    lo = S5_PAD
    first, last = (n - 1, 0) if reverse else (0, n - 1)
    scr[...] = jnp.zeros_like(scr)
    scr[:, lo:lo + n, :] = x
    if h0 is not None:
        carried = _cmul(*decay[lvl0], h0.reshape(nb, sw)).reshape(nb, 1, sw)
        scr[:, lo + first:lo + first + 1, :] = x[:, first:first + 1, :] + carried
    off, lvl = 1, lvl0
    while off < n:
        src = lo + off if reverse else lo - off
        cur = scr[:, lo:lo + n, :]
        sh = scr[:, src:src + n, :].reshape(nb * n, sw)
        scr[:, lo:lo + n, :] = cur + _cmul(*decay[lvl], sh).reshape(nb, n, sw)
        off, lvl = off * 2, lvl + 1
    h_out = scr[:, lo + last:lo + last + 1, :]
    edge = lo + n if reverse else lo - 1
    if h0 is not None:
        scr[:, edge:edge + 1, :] = h0
    start = lo + 1 if reverse else lo - 1
    return scr[:, start:start + n, :], h_out


def _s5_direction(hloc, decay, scr, reverse, nb, n_rq, n_cc):
    n_lat_rows = nb * n_rq * GRID_W
    lat = hloc[:n_lat_rows].reshape(nb, n_rq, GRID_W, S5_SW)
    ctx = hloc[n_lat_rows:].reshape(nb, n_cc, S5_SW)
    h_ctx_rows, h_ctx = _s5_segment_scan(ctx, None, decay, 0, scr, reverse)
    step = lambda h: _cmul(*decay[0], h.reshape(nb * GRID_W, S5_SW)).reshape(nb, GRID_W, S5_SW)
    order = list(range(n_rq))[::-1] if reverse else list(range(n_rq))
    col = lat[:, order[0]]
    for rq in order[1:]:
        col = step(col) + lat[:, rq]
    lvl_col = int(math.log2(n_rq))
    h_col, _ = _s5_segment_scan(col, h_ctx, decay, lvl_col, scr, reverse)
    starts = {order[0]: h_col}
    for prev, rq in zip(order[:-1], order[1:]):
        starts[rq] = step(starts[prev]) + lat[:, prev]
    lat_start = jnp.stack([starts[rq] for rq in range(n_rq)], axis=1)
    return jnp.concatenate([lat_start.reshape(n_lat_rows, S5_SW), h_ctx_rows.reshape(nb * n_cc, S5_SW)], axis=0)


def _s5_kernel(u_ref, arf_ref, arb_ref, acf_ref, acb_ref, ldf_ref, ldb_ref, bt_ref, ct_ref,
               y_ref, ug_ref, scr, *, nb, n_lat, n_ctx):
    j = pl.program_id(0) % S5_TILE_GROUPS
    n_rq = n_lat // (GRID_W * S5_CHUNK)
    n_cc = n_ctx // S5_CHUNK
    lat_rows = n_rq * GRID_W
    lanes = S5_TILE_GROUPS * S5_GROUP
    slot = lax.broadcasted_iota(jnp.int32, (1, lanes), 1) >> S5_GROUP_SHIFT
    wrap = S5_TILE_GROUPS - 1

    runs = []
    for b in range(nb):
        for rq in range(n_rq):
            runs.append((lambda t, b=b, rq=rq: pl.ds(b * n_lat + (rq * S5_CHUNK + t) * GRID_W, GRID_W),
                         b * lat_rows + rq * GRID_W, GRID_W))
    for b in range(nb):
        runs.append((lambda t, b=b: pl.ds(nb * n_lat + b * n_ctx + t, n_cc, stride=S5_CHUNK),
                     nb * lat_rows + b * n_cc, n_cc))

    for tok_rows, row0, n in runs:
        halves = []
        for th in range(2):
            acc = jnp.zeros((n, lanes), F32)
            for tp in range(S5_TILE_GROUPS):
                x = u_ref[tok_rows(th * S5_TILE_GROUPS + tp), :]
                acc = jnp.where(slot == tp, pltpu.roll(x, ((tp - j) & wrap) * S5_GROUP, 1), acc)
            halves.append(acc)
        ug_ref[pl.ds(row0, n), :] = jnp.concatenate(halves, axis=1).astype(BF16)

    u = ug_ref[...]
    y = None
    for reverse, ar_ref, ac_ref, ld_ref in ((False, arf_ref, acf_ref, ldf_ref), (True, arb_ref, acb_ref, ldb_ref)):
        toeplitz, to_state, from_state, decay = _s5_group_tables(
            ar_ref[0], ac_ref[0], ld_ref[0], bt_ref[0], ct_ref[0], reverse)
        hstart = _s5_direction(_dot(u, to_state), decay, scr, reverse, nb, n_rq, n_cc)
        part = _dot(u, toeplitz) + _dot(hstart.astype(BF16), from_state)
        y = part if y is None else y + part

    @pl.when(j == 0)
    def _():
        y_ref[...] = jnp.zeros_like(y_ref)

    for tok_rows, row0, n in runs:
        for th in range(2):
            piece = y[row0:row0 + n, th * lanes:(th + 1) * lanes]
            for tp in range(S5_TILE_GROUPS):
                rows = tok_rows(th * S5_TILE_GROUPS + tp)
                placed = pltpu.roll(piece, ((j - tp) & wrap) * S5_GROUP, 1)
                y_ref[rows, :] = jnp.where(slot == j, placed, y_ref[rows, :])


def _s5(u, params_f, params_b, bt, ct, nb, n_lat, n_ctx):
    t, width = u.shape
    lanes = S5_TILE_GROUPS * S5_GROUP
    n_rows = nb * (n_lat + n_ctx) // S5_CHUNK
    tile = pl.BlockSpec((t, lanes), lambda g: (0, g // S5_TILE_GROUPS))
    grp = lambda a: pl.BlockSpec((1,) + a.shape[1:], lambda g: (g, 0, 0))
    arf, acf, ldf = params_f
    arb, acb, ldb = params_b
    ops = (arf, arb, acf, acb, ldf, ldb, bt, ct)
    return pl.pallas_call(
        functools.partial(_s5_kernel, nb=nb, n_lat=n_lat, n_ctx=n_ctx),
        grid=(S5_GROUPS,),
        in_specs=[tile] + [grp(a) for a in ops],
        out_specs=tile,
        out_shape=jax.ShapeDtypeStruct((t, width), F32),
        scratch_shapes=[pltpu.VMEM((n_rows, S5_CW), BF16),
                        pltpu.VMEM((nb, 2 * S5_PAD + GRID_W, S5_SW), F32)],
        compiler_params=_params(1),
        name="s5_bidir",
    )(u, *ops)


def _s5_param_layout(a_re, a_im, log_dt):
    arow = jnp.stack([jnp.concatenate([a_re, a_re], axis=-1), jnp.concatenate([a_im, a_im], axis=-1)], axis=1)
    acol = jnp.stack([a_re, a_im], axis=-1)
    return arow, acol, log_dt[:, None, None]


def _s5_bc_layout(b_re, b_im, c_re, c_im):
    bt = jnp.concatenate([b_re.transpose(0, 2, 1), b_im.transpose(0, 2, 1)], axis=-1)
    ct = jnp.concatenate([c_re.transpose(0, 2, 1), c_im.transpose(0, 2, 1)], axis=1)
    return bt, jnp.tile(ct, (1, 1, S5_CHUNK))


def _gelu_tanh(x):
    return 0.5 * x * (1.0 + jnp.tanh(math.sqrt(2.0 / math.pi) * (x + 0.044715 * (x * x * x))))


def _merge_kernel(of_ref, ob_ref, y_ref, u_ref, r_ref, ga_ref, gm_ref, x_ref, g1_ref,
                  gnw_ref, d_ref, wgp_ref, wglu_ref, bglu_ref, wsp_ref, wout_ref, o_ref):
    o = of_ref[...] + ob_ref[...]
    heads = []
    for h in range(GLA_HEADS):
        oh = o[:, h * GLA_DV:(h + 1) * GLA_DV]
        heads.append(oh * lax.rsqrt(jnp.mean(oh * oh, axis=-1, keepdims=True) + EPS) * gnw_ref[...])
    og = jnp.concatenate(heads, axis=1) * _silu(r_ref[...].astype(F32))
    ya = _dot(og.astype(BF16), wgp_ref[...])
    s = _gelu_tanh(y_ref[...] + d_ref[...] * u_ref[...].astype(F32))
    s = s * jax.nn.sigmoid(_dot(s.astype(BF16), wglu_ref[...]) + bglu_ref[...])
    yb = _dot(s.astype(BF16), wsp_ref[...])
    m = jax.nn.sigmoid(ga_ref[...].astype(F32)) * ya + jax.nn.sigmoid(gm_ref[...].astype(F32)) * yb
    o_ref[...] = x_ref[...] + g1_ref[...] * _dot(m.astype(BF16), wout_ref[...])


def _merge(of, ob, y, u, r, ga, gm, x_all, mod3, tile_row, tm, n_tiles, weights):
    t, d = x_all.shape
    tok = lambda w: pl.BlockSpec((tm, w), lambda i: (i, 0))
    full = lambda a: pl.BlockSpec(a.shape, lambda i: (0,) * a.ndim)
    return pl.pallas_call(
        _merge_kernel,
        grid=(n_tiles,),
        in_specs=[tok(GLA_VAL), tok(GLA_VAL), tok(S5_WIDTH), tok(S5_WIDTH), tok(GLA_VAL),
                  tok(d), tok(d), tok(d), _mod_spec(2, tile_row)] + [full(a) for a in weights],
        out_specs=tok(d),
        out_shape=jax.ShapeDtypeStruct((n_tiles * tm, d), F32),
        compiler_params=_params(1),
        name="merge",
    )(of, ob, y, u, r, ga, gm, x_all, mod3, *weights)


def _ffn_kernel(x_ref, sh_ref, sc_ref, g2_ref, nw_ref, wg_ref, wu_ref, wd_ref, o_ref, h_ref, acc_ref):
    f = pl.program_id(1)

    @pl.when(f == 0)
    def _():
        h_ref[...] = _norm_mod(x_ref[...], nw_ref[...], sh_ref[...], sc_ref[...]).astype(BF16)
        acc_ref[...] = jnp.zeros_like(acc_ref)

    h = h_ref[...]
    a = _silu(_dot(h, wg_ref[...])) * _dot(h, wu_ref[...])
    acc_ref[...] += _dot(a.astype(BF16), wd_ref[...])

    @pl.when(f == pl.num_programs(1) - 1)
    def _():
        o_ref[...] = x_ref[...] + g2_ref[...] * acc_ref[...]


def _mod_spec2(col, tile_row):
    return pl.BlockSpec((None, 1, D_MODEL), lambda i, j: (tile_row(i), 0, col))


def _ffn(x_all, mod3, tile_row, tm, norm_w, wg, wu, wd, tf):
    t, d = x_all.shape
    f_dim = wg.shape[1]
    return pl.pallas_call(
        _ffn_kernel,
        grid=(t // tm, f_dim // tf),
        in_specs=[pl.BlockSpec((tm, d), lambda i, j: (i, 0)),
                  _mod_spec2(3, tile_row), _mod_spec2(4, tile_row), _mod_spec2(5, tile_row),
                  pl.BlockSpec(norm_w.shape, lambda i, j: (0, 0)),
                  pl.BlockSpec((d, tf), lambda i, j: (0, j)),
                  pl.BlockSpec((d, tf), lambda i, j: (0, j)),
                  pl.BlockSpec((tf, d), lambda i, j: (j, 0))],
        out_specs=pl.BlockSpec((tm, d), lambda i, j: (i, 0)),
        out_shape=jax.ShapeDtypeStruct((t, d), F32),
        scratch_shapes=[pltpu.VMEM((tm, d), BF16), pltpu.VMEM((tm, d), F32)],
        compiler_params=_params(2),
        name="ffn_dense",
    )(x_all, mod3, mod3, mod3, norm_w, wg, wu, wd)


def _moe_kernel(x_ref, sh_ref, sc_ref, g2_ref, nw_ref, fw_ref, wr_ref, wg_ref, wu_ref, wd_ref,
                o_ref, h_ref, gate_ref, acc_ref):
    e = pl.program_id(1)
    f = pl.program_id(2)
    lane = lax.broadcasted_iota(jnp.int32, gate_ref.shape, 1).astype(F32)

    @pl.when((e == 0) & (f == 0))
    def _():
        hf = _norm_mod(x_ref[...], nw_ref[...], sh_ref[...], sc_ref[...])
        h_ref[...] = hf.astype(BF16)
        acc_ref[...] = jnp.zeros_like(acc_ref)
        logits = jnp.dot(hf, wr_ref[...], precision=HIGHEST, preferred_element_type=F32)
        logits = jnp.where(lane < N_EXPERTS, logits, -jnp.inf)
        m1 = jnp.max(logits, axis=-1, keepdims=True)
        i1 = jnp.min(jnp.where(logits == m1, lane, float(Z_PAD)), axis=-1, keepdims=True)
        rest = jnp.where(lane == i1, -jnp.inf, logits)
        m2 = jnp.max(rest, axis=-1, keepdims=True)
        i2 = jnp.min(jnp.where(rest == m2, lane, float(Z_PAD)), axis=-1, keepdims=True)
        e2 = jnp.exp(m2 - m1)
        w1 = 1.0 / (1.0 + e2)
        gate_ref[...] = jnp.where(lane == i1, w1, 0.0) + jnp.where(lane == i2, e2 * w1, 0.0)

    h = h_ref[...]
    ge = jnp.sum(jnp.where(lane == e.astype(F32), gate_ref[...], 0.0), axis=-1, keepdims=True)
    a = _silu(_dot(h, wg_ref[0])) * _dot(h, wu_ref[0]) * ge
    acc_ref[...] += _dot(a.astype(BF16), wd_ref[0])

    @pl.when((e == pl.num_programs(1) - 1) & (f == pl.num_programs(2) - 1))
    def _():
        y = x_ref[...] + g2_ref[...] * acc_ref[...]
        o_ref[...] = y * lax.rsqrt(jnp.mean(y * y, axis=-1, keepdims=True) + EPS) * fw_ref[...]


def _moe(x_lat, mod3, tile_row, tm, norm_w, final_w, w_router, wg, wu, wd, tf):
    t, d = x_lat.shape
    n_e, _, f_dim = wg.shape
    mspec = lambda col: pl.BlockSpec((None, 1, D_MODEL), lambda i, e, j: (tile_row(i), 0, col))
    const = lambda a: pl.BlockSpec(a.shape, lambda i, e, j: (0,) * a.ndim)
    return pl.pallas_call(
        _moe_kernel,
        grid=(t // tm, n_e, f_dim // tf),
        in_specs=[pl.BlockSpec((tm, d), lambda i, e, j: (i, 0)),
                  mspec(3), mspec(4), mspec(5), const(norm_w), const(final_w), const(w_router),
                  pl.BlockSpec((1, d, tf), lambda i, e, j: (e, 0, j)),
                  pl.BlockSpec((1, d, tf), lambda i, e, j: (e, 0, j)),
                  pl.BlockSpec((1, tf, d), lambda i, e, j: (e, j, 0))],
        out_specs=pl.BlockSpec((tm, d), lambda i, e, j: (i, 0)),
        out_shape=jax.ShapeDtypeStruct((t, d), F32),
        scratch_shapes=[pltpu.VMEM((tm, d), BF16), pltpu.VMEM((tm, Z_PAD), F32), pltpu.VMEM((tm, d), F32)],
        compiler_params=_params(3),
        name="moe",
    )(x_lat, mod3, mod3, mod3, norm_w, final_w, w_router, wg, wu, wd)


def _permute_w_in(w):
    o = 0
    parts = {}
    for name, width in (("k", GLA_KEY), ("v", GLA_VAL), ("zf", GLA_RANK), ("zb", GLA_RANK), ("u", S5_WIDTH),
                        ("q", GLA_KEY), ("r", GLA_VAL), ("ga", D_MODEL), ("gm", D_MODEL)):
        parts[name] = w[:, o:o + width]
        o += width
    pad = jnp.zeros((w.shape[0], Z_PAD - 2 * GLA_RANK), w.dtype)
    cols = [parts[n] for n in ("k", "v", "u", "q", "r", "ga", "gm", "zf", "zb")] + [pad]
    return jnp.concatenate(cols, axis=1).astype(BF16)


def _low_rank_gate(lr_f, lr_b, bias_f, bias_b):
    lr = jnp.zeros((Z_PAD, 2 * GLA_KEY), F32)
    lr = lr.at[:GLA_RANK, :GLA_KEY].set(lr_f).at[GLA_RANK:2 * GLA_RANK, GLA_KEY:].set(lr_b)
    return lr.astype(BF16), jnp.concatenate([bias_f, bias_b])[None, :]


def kernel(x, c, ctx, c_ctx, w_mod, b_mod, norm1_w, norm2_w, final_norm_w, w_in, gla_lr_f, gla_lr_b, gla_bias_f, gla_bias_b, gla_norm_w, s5_a_re_f, s5_a_im_f, s5_log_dt_f, s5_a_re_b, s5_a_im_b, s5_log_dt_b, s5_b_re, s5_b_im, s5_c_re, s5_c_im, s5_d, s5_w_glu, s5_b_glu, w_gla_proj, w_s5_proj, w_out, ffn_w_gate, ffn_w_up, ffn_w_down, moe_router, moe_w_gate, moe_w_up, moe_w_down):
    batch, n_lat, d = x.shape
    n_ctx = ctx.shape[1]
    depth = w_mod.shape[0]
    assert d == D_MODEL and batch < 8 and depth == 2
    assert n_lat % n_ctx == 0 and n_ctx % GLA_CHUNK == 0 and (n_lat // GRID_W) % S5_CHUNK == 0
    t_lat, t_ctx = batch * n_lat, batch * n_ctx
    t_all = t_lat + t_ctx

    tm = math.gcd(512, t_ctx)
    assert n_lat % tm == 0
    lat_tiles_per_batch = n_lat // tm
    n_lat_tiles = t_lat // tm
    tile_row = lambda i: jnp.where(i < n_lat_tiles, i // lat_tiles_per_batch, batch)

    cond = jnp.zeros((8, d), F32).at[:batch].set(c).at[batch].set(c_ctx)
    mod = _modulation(cond, w_mod, b_mod)

    x_all = jnp.concatenate([x.reshape(t_lat, d), ctx.reshape(t_ctx, d)], axis=0)
    out = None
    for l in range(depth):
        last = l == depth - 1
        mod3 = mod[l].reshape(8, 1, 6 * d)
        lr, lb = _low_rank_gate(gla_lr_f[l], gla_lr_b[l], gla_bias_f[l], gla_bias_b[l])
        k, v, u, q, r, ga, gm, gf, gb = _in_projection(
            x_all, mod3, tile_row, tm, norm1_w[l][None, :], _permute_w_in(w_in[l]), lr, lb)

        of, ob = _gla(q, k, v, gf, gb, batch, n_lat // n_ctx, n_ctx)

        bt, ct = _s5_bc_layout(s5_b_re[l], s5_b_im[l], s5_c_re[l], s5_c_im[l])
        y = _s5(u, _s5_param_layout(s5_a_re_f[l], s5_a_im_f[l], s5_log_dt_f[l]),
                _s5_param_layout(s5_a_re_b[l], s5_a_im_b[l], s5_log_dt_b[l]), bt, ct, batch, n_lat, n_ctx)

        merge_w = (gla_norm_w[l][None, :], s5_d[l][None, :], w_gla_proj[l].astype(BF16),
                   s5_w_glu[l].astype(BF16), s5_b_glu[l][None, :], w_s5_proj[l].astype(BF16),
                   w_out[l].astype(BF16))
        n_tiles = n_lat_tiles if last else t_all // tm
        x_mid = _merge(of, ob, y, u, r, ga, gm, x_all, mod3, tile_row, tm, n_tiles, merge_w)

        if l % 2 == 0:
            i = l // 2
            tm_f = math.gcd(1024, n_ctx * batch)
            row_f = lambda j, tm_f=tm_f: jnp.where(j < t_lat // tm_f, j // (n_lat // tm_f), batch)
            x_all = _ffn(x_mid, mod3, row_f, tm_f, norm2_w[l][None, :], ffn_w_gate[i].astype(BF16),
                         ffn_w_up[i].astype(BF16), ffn_w_down[i].astype(BF16), 1408)
        else:
            i = l // 2
            tm_f = math.gcd(1024, n_lat)
            row_f = lambda j, tm_f=tm_f: j // (n_lat // tm_f)
            w_router = jnp.zeros((d, Z_PAD), F32).at[:, :N_EXPERTS].set(moe_router[i])
            res = _moe(x_mid, mod3, row_f, tm_f, norm2_w[l][None, :], final_norm_w[None, :], w_router,
                       moe_w_gate[i].astype(BF16), moe_w_up[i].astype(BF16), moe_w_down[i].astype(BF16), 896)
            out = res.reshape(batch, n_lat, d)
    return out
```

```python
import functools
import math

import jax
import jax.numpy as jnp
from jax import lax
from jax.experimental import pallas as pl
from jax.experimental.pallas import tpu as pltpu

F32 = jnp.float32
BF16 = jnp.bfloat16
HIGHEST = lax.Precision.HIGHEST

D_MODEL = 1024
GRID_W = 64
GLA_HEADS = 4
GLA_DK = 64
GLA_DV = 128
GLA_KEY = GLA_HEADS * GLA_DK
GLA_VAL = GLA_HEADS * GLA_DV
GLA_RANK = 16
GLA_GATE_NORM = 16.0
GLA_CHUNK = 64
S5_WIDTH = 512
S5_GROUP = 16
S5_GROUPS = S5_WIDTH // S5_GROUP
S5_STATE = 64
S5_CHUNK = 16
S5_CW = S5_CHUNK * S5_GROUP
S5_SW = 2 * S5_STATE
S5_PAD = GRID_W // 2
S5_TILE_GROUPS = 128 // S5_GROUP
S5_GROUP_SHIFT = 4
N_EXPERTS = 8
EPS = 1e-6
MXU_DIM = 256
MOE_F_TILE = 7 * MXU_DIM
Z_PAD = 128

VMEM_LIMIT = 56 * 1024 * 1024

SEC_K = (0, GLA_KEY)
SEC_V = (SEC_K[1], SEC_K[1] + GLA_VAL)
SEC_U = (SEC_V[1], SEC_V[1] + S5_WIDTH)
SEC_Q = (SEC_U[1], SEC_U[1] + GLA_KEY)
SEC_R = (SEC_Q[1], SEC_Q[1] + GLA_VAL)
SEC_GA = (SEC_R[1], SEC_R[1] + D_MODEL)
SEC_GM = (SEC_GA[1], SEC_GA[1] + D_MODEL)
SEC_Z = (SEC_GM[1], SEC_GM[1] + Z_PAD)
IN_PERM_WIDTH = SEC_Z[1]


def _params(n_axes):
    return pltpu.CompilerParams(dimension_semantics=("arbitrary",) * n_axes,
                                vmem_limit_bytes=VMEM_LIMIT)


def _silu(x):
    return x * jax.nn.sigmoid(x)


def _norm_mod(x, nw, sh, sc):
    y = x * lax.rsqrt(jnp.mean(x * x, axis=-1, keepdims=True) + EPS) * nw
    return y * (1.0 + sc) + sh


def _dot(a, b):
    return jnp.dot(a, b, preferred_element_type=F32)


def _dot_nt(a, b):
    return lax.dot_general(a, b, (((1,), (1,)), ((), ())), preferred_element_type=F32)


def _dot_tn(a, b):
    return lax.dot_general(a, b, (((0,), (0,)), ((), ())), preferred_element_type=F32)


def _mod_kernel(c_ref, w_ref, b_ref, o_ref):
    s = _silu(c_ref[...]).astype(BF16)
    o_ref[0] = _dot(s, w_ref[0].astype(BF16)) + b_ref[0]


def _modulation(cond, w_mod, b_mod):
    depth, d, n = w_mod.shape
    tn = 1536
    return pl.pallas_call(
        _mod_kernel,
        grid=(depth, n // tn),
        in_specs=[pl.BlockSpec((8, d), lambda l, j: (0, 0)),
                  pl.BlockSpec((1, d, tn), lambda l, j: (l, 0, j)),
                  pl.BlockSpec((1, 1, tn), lambda l, j: (l, 0, j))],
        out_specs=pl.BlockSpec((1, 8, tn), lambda l, j: (l, 0, j)),
        out_shape=jax.ShapeDtypeStruct((depth, 8, n), F32),
        compiler_params=_params(2),
        name="modulation",
    )(cond, w_mod, b_mod.reshape(depth, 1, n))


def _inproj_kernel(x_ref, sh_ref, sc_ref, nw_ref, w_ref, lr_ref, lb_ref,
                   k_ref, v_ref, u_ref, q_ref, r_ref, ga_ref, gm_ref, gf_ref, gb_ref):
    h = _norm_mod(x_ref[...], nw_ref[...], sh_ref[...], sc_ref[...]).astype(BF16)

    def sec(s):
        return _dot(h, w_ref[:, s[0]:s[1]])

    k_ref[...] = sec(SEC_K).astype(BF16)
    v_ref[...] = sec(SEC_V).astype(BF16)
    u_ref[...] = sec(SEC_U)
    q_ref[...] = (sec(SEC_Q) * GLA_DK ** -0.5).astype(BF16)
    r_ref[...] = sec(SEC_R).astype(BF16)
    ga_ref[...] = sec(SEC_GA).astype(BF16)
    gm_ref[...] = sec(SEC_GM).astype(BF16)
    z = sec(SEC_Z).astype(BF16)
    pre = _dot(z, lr_ref[...]) + lb_ref[...]
    g = (jnp.minimum(pre, 0.0) - jnp.log(1.0 + jnp.exp(-jnp.abs(pre)))) * (1.0 / GLA_GATE_NORM)
    gf_ref[...] = g[:, :GLA_KEY]
    gb_ref[...] = g[:, GLA_KEY:]


def _mod_spec(col, tile_row):
    return pl.BlockSpec((None, 1, D_MODEL), lambda i: (tile_row(i), 0, col))


def _in_projection(x_all, mod3, tile_row, tm, norm_w, w_perm, lr, lb):
    t, d = x_all.shape
    widths = (GLA_KEY, GLA_VAL, S5_WIDTH, GLA_KEY, GLA_VAL, D_MODEL, D_MODEL)
    tok = lambda w: pl.BlockSpec((tm, w), lambda i: (i, 0))
    full = lambda a: pl.BlockSpec(a.shape, lambda i: (0,) * a.ndim)
    return pl.pallas_call(
        _inproj_kernel,
        grid=(t // tm,),
        in_specs=[tok(d), _mod_spec(0, tile_row), _mod_spec(1, tile_row),
                  full(norm_w), full(w_perm), full(lr), full(lb)],
        out_specs=[tok(w) for w in widths] + [tok(GLA_KEY), tok(GLA_KEY)],
        out_shape=[jax.ShapeDtypeStruct((t, w), F32 if i == 2 else BF16) for i, w in enumerate(widths)]
        + [jax.ShapeDtypeStruct((t, GLA_KEY), F32)] * 2,
        compiler_params=_params(1),
        name="in_projection",
    )(x_all, mod3, mod3, norm_w, w_perm, lr, lb)


def _gla_chunk(q_ref, k_ref, v_ref, g_ref, o_ref, s_ref, r0, tri_f, tri_mask, last):
    rows = pl.ds(r0, GLA_CHUNK)
    g = g_ref[rows, :]
    g_hi = g.astype(BF16)
    g_lo = (g - g_hi.astype(F32)).astype(BF16)
    bc = _dot(tri_f, g_hi) + _dot(tri_f, g_lo)
    bl = bc[last:last + 1, :]
    q = q_ref[rows, :].astype(F32)
    k = k_ref[rows, :].astype(F32)
    qd = (q * jnp.exp(bc)).astype(BF16)
    kd = (k * jnp.exp(-bc)).astype(BF16)
    kl = (k * jnp.exp(bl - bc)).astype(BF16)
    dec = jnp.exp(bl)
    v = v_ref[rows, :]
    outs = []
    for h in range(GLA_HEADS):
        ks = slice(h * GLA_DK, (h + 1) * GLA_DK)
        vs = slice(h * GLA_DV, (h + 1) * GLA_DV)
        att = jnp.where(tri_mask, _dot_nt(qd[:, ks], kd[:, ks]), 0.0).astype(BF16)
        st = s_ref[h]
        outs.append(_dot(att, v[:, vs]) + _dot_nt(qd[:, ks], st.astype(BF16)))
        s_ref[h] = st * dec[:, ks] + _dot_tn(v[:, vs], kl[:, ks])
    o_ref[rows, :] = jnp.concatenate(outs, axis=1)


def _gla_kernel(qf_ref, kf_ref, vf_ref, gf_ref, qb_ref, kb_ref, vb_ref, gb_ref,
                of_ref, ob_ref, sf_ref, sb_ref, *, n_chunks):
    @pl.when(pl.program_id(1) == 0)
    def _():
        sf_ref[...] = jnp.zeros_like(sf_ref)
        sb_ref[...] = jnp.zeros_like(sb_ref)

    row = lax.broadcasted_iota(jnp.int32, (GLA_CHUNK, GLA_CHUNK), 0)
    col = lax.broadcasted_iota(jnp.int32, (GLA_CHUNK, GLA_CHUNK), 1)
    lower = col <= row
    upper = col >= row
    for c in range(n_chunks):
        _gla_chunk(qf_ref, kf_ref, vf_ref, gf_ref, of_ref, sf_ref, c * GLA_CHUNK,
                   lower.astype(BF16), lower, GLA_CHUNK - 1)
        _gla_chunk(qb_ref, kb_ref, vb_ref, gb_ref, ob_ref, sb_ref, (n_chunks - 1 - c) * GLA_CHUNK,
                   upper.astype(BF16), upper, 0)


def _gla(q, k, v, gf, gb, batch, n_lat_blocks, blk):
    t = q.shape[0]
    ctx0 = batch * n_lat_blocks

    def fwd(b, i):
        return (jnp.where(i == 0, ctx0 + b, b * n_lat_blocks + i - 1), 0)

    def bwd(b, i):
        return (jnp.where(i == 0, ctx0 + b, b * n_lat_blocks + n_lat_blocks - i), 0)

    spec = lambda w, m: pl.BlockSpec((blk, w), m)
    state = pltpu.VMEM((GLA_HEADS, GLA_DV, GLA_DK), F32)
    return pl.pallas_call(
        functools.partial(_gla_kernel, n_chunks=blk // GLA_CHUNK),
        grid=(batch, n_lat_blocks + 1),
        in_specs=[spec(GLA_KEY, fwd), spec(GLA_KEY, fwd), spec(GLA_VAL, fwd), spec(GLA_KEY, fwd),
                  spec(GLA_KEY, bwd), spec(GLA_KEY, bwd), spec(GLA_VAL, bwd), spec(GLA_KEY, bwd)],
        out_specs=[spec(GLA_VAL, fwd), spec(GLA_VAL, bwd)],
        out_shape=[jax.ShapeDtypeStruct((t, GLA_VAL), F32)] * 2,
        scratch_shapes=[state, state],
        compiler_params=_params(2),
        name="gla_bidir",
    )(q, k, v, gf, q, k, v, gb)


def _s5_group_tables(arow, acol, log_dt, bt, ct, reverse):
    delta = jnp.exp(log_dt)
    lane_s = lax.broadcasted_iota(jnp.int32, (1, S5_SW), 1)
    sign = jnp.where(lane_s < S5_STATE, -1.0, 1.0)
    ar, ai = arow[0:1, :], arow[1:2, :]
    zr, zi = ar * delta, ai * delta

    def row_power(n):
        mag = jnp.exp(zr * n)
        return mag * jnp.cos(zi * n), mag * jnp.sin(zi * n)

    one = jnp.ones((1, 1), F32)
    l_r, l_i = row_power(one)
    den = ar * ar + ai * ai
    xr, xi = l_r - 1.0, l_i
    cr = (xr * ar + xi * ai) / den
    ci = (xi * ar - xr * ai) / den
    bdup = cr * bt + ci * (pltpu.roll(bt, S5_STATE, 1) * sign)
    bswp = pltpu.roll(bdup, S5_STATE, 1) * sign

    step = lax.broadcasted_iota(jnp.int32, (S5_CHUNK, 1), 0).astype(F32)
    er, ei = row_power(step if reverse else (S5_CHUNK - 1.0 - step))
    to_state = jnp.concatenate(
        [er[s:s + 1, :] * bdup + ei[s:s + 1, :] * bswp for s in range(S5_CHUNK)], axis=0)

    zrc, zic = acol[:, 0:1] * delta, acol[:, 1:2] * delta
    lane_c = lax.broadcasted_iota(jnp.int32, (1, S5_CW), 1)
    t_lane = (lane_c >> S5_GROUP_SHIFT).astype(F32)
    e_k = (S5_CHUNK - 1.0 - t_lane) if reverse else t_lane
    mag = jnp.exp(zrc * e_k)
    pr, pi = mag * jnp.cos(zic * e_k), mag * jnp.sin(zic * e_k)
    ctr, cti = ct[:S5_STATE, :], ct[S5_STATE:, :]
    kr, ki = ctr * pr - cti * pi, ctr * pi + cti * pr
    lam_mag = jnp.exp(zrc)
    lcr, lci = lam_mag * jnp.cos(zic), lam_mag * jnp.sin(zic)
    from_state = jnp.concatenate([kr * lcr - ki * lci, -(kr * lci + ki * lcr)], axis=0)
    krow = jnp.dot(bdup, jnp.concatenate([kr, -ki], axis=0), precision=HIGHEST,
                   preferred_element_type=F32)
    blocks = []
    for s in range(S5_CHUNK):
        if reverse:
            sh = S5_GROUP * (S5_CHUNK - 1 - s)
            blk = pltpu.roll(krow, (S5_CW - sh) % S5_CW, 1) if sh else krow
            blocks.append(jnp.where(lane_c < S5_GROUP * (s + 1), blk, 0.0))
        else:
            sh = S5_GROUP * s
            blk = pltpu.roll(krow, sh, 1) if sh else krow
            blocks.append(jnp.where(lane_c >= sh, blk, 0.0))
    toeplitz = jnp.concatenate(blocks, axis=0)

    wr, wi = row_power(one * float(S5_CHUNK))
    decay = []
    for _ in range(8):
        decay.append((wr, wi * sign))
        wr, wi = wr * wr - wi * wi, 2.0 * wr * wi
    return toeplitz.astype(BF16), to_state.astype(BF16), from_state.astype(BF16), decay


def _cmul(wa, wb, x2d):
    return wa * x2d + wb * pltpu.roll(x2d, S5_STATE, 1)


def _s5_segment_scan(x, h0, decay, lvl0, scr, reverse):
    nb, n, sw = x.shape
    lo = S5_PAD
    first, last = (n - 1, 0) if reverse else (0, n - 1)
    scr[...] = jnp.zeros_like(scr)
    scr[:, lo:lo + n, :] = x
    if h0 is not None:
        carried = _cmul(*decay[lvl0], h0.reshape(nb, sw)).reshape(nb, 1, sw)
        scr[:, lo + first:lo + first + 1, :] = x[:, first:first + 1, :] + carried
    off, lvl = 1, lvl0
    while off < n:
        src = lo + off if reverse else lo - off
        cur = scr[:, lo:lo + n, :]
        sh = scr[:, src:src + n, :].reshape(nb * n, sw)
        scr[:, lo:lo + n, :] = cur + _cmul(*decay[lvl], sh).reshape(nb, n, sw)
        off, lvl = off * 2, lvl + 1
    h_out = scr[:, lo + last:lo + last + 1, :]
    edge = lo + n if reverse else lo - 1
    if h0 is not None:
        scr[:, edge:edge + 1, :] = h0
    start = lo + 1 if reverse else lo - 1
    return scr[:, start:start + n, :], h_out


def _s5_direction(hloc, decay, scr, reverse, nb, n_rq, n_cc):
    n_lat_rows = nb * n_rq * GRID_W
    lat = hloc[:n_lat_rows].reshape(nb, n_rq, GRID_W, S5_SW)
    ctx = hloc[n_lat_rows:].reshape(nb, n_cc, S5_SW)
    h_ctx_rows, h_ctx = _s5_segment_scan(ctx, None, decay, 0, scr, reverse)
    step = lambda h: _cmul(*decay[0], h.reshape(nb * GRID_W, S5_SW)).reshape(nb, GRID_W, S5_SW)
    order = list(range(n_rq))[::-1] if reverse else list(range(n_rq))
    col = lat[:, order[0]]
    for rq in order[1:]:
        col = step(col) + lat[:, rq]
    lvl_col = int(math.log2(n_rq))
    h_col, _ = _s5_segment_scan(col, h_ctx, decay, lvl_col, scr, reverse)
    starts = {order[0]: h_col}
    for prev, rq in zip(order[:-1], order[1:]):
        starts[rq] = step(starts[prev]) + lat[:, prev]
    lat_start = jnp.stack([starts[rq] for rq in range(n_rq)], axis=1)
    return jnp.concatenate([lat_start.reshape(n_lat_rows, S5_SW), h_ctx_rows.reshape(nb * n_cc, S5_SW)], axis=0)


def _s5_kernel(u_ref, arf_ref, arb_ref, acf_ref, acb_ref, ldf_ref, ldb_ref, bt_ref, ct_ref,
               y_ref, ug_ref, scr, *, nb, n_lat, n_ctx):
    j = pl.program_id(0) % S5_TILE_GROUPS
    n_rq = n_lat // (GRID_W * S5_CHUNK)
    n_cc = n_ctx // S5_CHUNK
    lat_rows = n_rq * GRID_W
    lanes = S5_TILE_GROUPS * S5_GROUP
    slot = lax.broadcasted_iota(jnp.int32, (1, lanes), 1) >> S5_GROUP_SHIFT
    wrap = S5_TILE_GROUPS - 1

    runs = []
    for b in range(nb):
        for rq in range(n_rq):
            runs.append((lambda t, b=b, rq=rq: pl.ds(b * n_lat + (rq * S5_CHUNK + t) * GRID_W, GRID_W),
                         b * lat_rows + rq * GRID_W, GRID_W, True))
    for b in range(nb):
        runs.append((lambda t, b=b: pl.ds(nb * n_lat + b * n_ctx + t, n_cc, stride=S5_CHUNK),
                     nb * lat_rows + b * n_cc, n_cc, False))

    for tok_rows, row0, n, _ in runs:
        halves = []
        for th in range(2):
            acc = jnp.zeros((n, lanes), F32)
            for tp in range(S5_TILE_GROUPS):
                x = u_ref[tok_rows(th * S5_TILE_GROUPS + tp), :]
                acc = jnp.where(slot == tp, pltpu.roll(x, ((tp - j) & wrap) * S5_GROUP, 1), acc)
            halves.append(acc)
        ug_ref[pl.ds(row0, n), :] = jnp.concatenate(halves, axis=1).astype(BF16)

    u = ug_ref[...]
    y = None
    for reverse, ar_ref, ac_ref, ld_ref in ((False, arf_ref, acf_ref, ldf_ref), (True, arb_ref, acb_ref, ldb_ref)):
        toeplitz, to_state, from_state, decay = _s5_group_tables(
            ar_ref[0], ac_ref[0], ld_ref[0], bt_ref[0], ct_ref[0], reverse)
        hstart = _s5_direction(_dot(u, to_state), decay, scr, reverse, nb, n_rq, n_cc)
        part = _dot(u, toeplitz) + _dot(hstart.astype(BF16), from_state)
        y = part if y is None else y + part

    @pl.when(j == 0)
    def _():
        y_ref[pl.ds(nb * n_lat, nb * n_ctx), :] = jnp.zeros((nb * n_ctx, lanes), F32)

    for tok_rows, row0, n, contiguous in runs:
        own =jnp.broadcast_to(slot == j, (n, lanes))
        for th in range(2):
            piece = y[row0:row0 + n, th * lanes:(th + 1) * lanes]
            for tp in range(S5_TILE_GROUPS):
                rows = tok_rows(th * S5_TILE_GROUPS + tp)
                placed = pltpu.roll(piece, ((j - tp) & wrap) * S5_GROUP, 1)
                if contiguous:
                    pltpu.store(y_ref.at[rows, :], placed, mask=own)
                else:
                    y_ref[rows, :] = jnp.where(own, placed, y_ref[rows, :])


def _s5(u, params_f, params_b, bt, ct, nb, n_lat, n_ctx):
    t, width = u.shape
    lanes = S5_TILE_GROUPS * S5_GROUP
    n_rows = nb * (n_lat + n_ctx) // S5_CHUNK
    tile = pl.BlockSpec((t, lanes), lambda g: (0, g // S5_TILE_GROUPS))
    grp = lambda a: pl.BlockSpec((1,) + a.shape[1:], lambda g: (g, 0, 0))
    arf, acf, ldf = params_f
    arb, acb, ldb = params_b
    ops = (arf, arb, acf, acb, ldf, ldb, bt, ct)
    return pl.pallas_call(
        functools.partial(_s5_kernel, nb=nb, n_lat=n_lat, n_ctx=n_ctx),
        grid=(S5_GROUPS,),
        in_specs=[tile] + [grp(a) for a in ops],
        out_specs=tile,
        out_shape=jax.ShapeDtypeStruct((t, width), F32),
        scratch_shapes=[pltpu.VMEM((n_rows, S5_CW), BF16),
                        pltpu.VMEM((nb, 2 * S5_PAD + GRID_W, S5_SW), F32)],
        compiler_params=_params(1),
        name="s5_bidir",
    )(u, *ops)


def _s5_param_layout(a_re, a_im, log_dt):
    arow = jnp.stack([jnp.concatenate([a_re, a_re], axis=-1), jnp.concatenate([a_im, a_im], axis=-1)], axis=1)
    acol = jnp.stack([a_re, a_im], axis=-1)
    return arow, acol, log_dt[:, None, None]


def _s5_bc_layout(b_re, b_im, c_re, c_im):
    bt = jnp.concatenate([b_re.transpose(0, 2, 1), b_im.transpose(0, 2, 1)], axis=-1)
    ct = jnp.concatenate([c_re.transpose(0, 2, 1), c_im.transpose(0, 2, 1)], axis=1)
    return bt, jnp.tile(ct, (1, 1, S5_CHUNK))


def _gelu_tanh(x):
    return 0.5 * x * (1.0 + jnp.tanh(math.sqrt(2.0 / math.pi) * (x + 0.044715 * (x * x * x))))


def _merge_kernel(of_ref, ob_ref, y_ref, u_ref, r_ref, ga_ref, gm_ref, x_ref, g1_ref,
                  gnw_ref, d_ref, wgp_ref, wglu_ref, bglu_ref, wsp_ref, wout_ref, o_ref):
    o = of_ref[...] + ob_ref[...]
    heads = []
    for h in range(GLA_HEADS):
        oh = o[:, h * GLA_DV:(h + 1) * GLA_DV]
        heads.append(oh * lax.rsqrt(jnp.mean(oh * oh, axis=-1, keepdims=True) + EPS) * gnw_ref[...])
    og = jnp.concatenate(heads, axis=1) * _silu(r_ref[...].astype(F32))
    ya = _dot(og.astype(BF16), wgp_ref[...])
    s = _gelu_tanh(y_ref[...] + d_ref[...] * u_ref[...].astype(F32))
    s = s * jax.nn.sigmoid(_dot(s.astype(BF16), wglu_ref[...]) + bglu_ref[...])
    yb = _dot(s.astype(BF16), wsp_ref[...])
    m = jax.nn.sigmoid(ga_ref[...].astype(F32)) * ya + jax.nn.sigmoid(gm_ref[...].astype(F32)) * yb
    o_ref[...] = x_ref[...] + g1_ref[...] * _dot(m.astype(BF16), wout_ref[...])


def _merge(of, ob, y, u, r, ga, gm, x_all, mod3, tile_row, tm, n_tiles, weights):
    t, d = x_all.shape
    tok = lambda w: pl.BlockSpec((tm, w), lambda i: (i, 0))
    full = lambda a: pl.BlockSpec(a.shape, lambda i: (0,) * a.ndim)
    return pl.pallas_call(
        _merge_kernel,
        grid=(n_tiles,),
        in_specs=[tok(GLA_VAL), tok(GLA_VAL), tok(S5_WIDTH), tok(S5_WIDTH), tok(GLA_VAL),
                  tok(d), tok(d), tok(d), _mod_spec(2, tile_row)] + [full(a) for a in weights],
        out_specs=tok(d),
        out_shape=jax.ShapeDtypeStruct((n_tiles * tm, d), F32),
        compiler_params=_params(1),
        name="merge",
    )(of, ob, y, u, r, ga, gm, x_all, mod3, *weights)


def _swiglu_hidden(h, wg, wu, width, scale=None):
    parts = []
    for c0 in range(0, width, MXU_DIM):
        cols = slice(c0, c0 + MXU_DIM)
        a = _silu(_dot(h, wg[:, cols])) * _dot(h, wu[:, cols])
        parts.append((a if scale is None else a * scale).astype(BF16))
    return jnp.concatenate(parts, axis=1)


def _ffn_kernel(x_ref, sh_ref, sc_ref, g2_ref, nw_ref, wg_ref, wu_ref, wd_ref, o_ref):
    x = x_ref[...]
    h = _norm_mod(x, nw_ref[...], sh_ref[...], sc_ref[...]).astype(BF16)
    a = _swiglu_hidden(h, wg_ref, wu_ref, wg_ref.shape[1])
    o_ref[...] = x + g2_ref[...] * _dot(a, wd_ref[...])


def _ffn(x_all, mod3, tile_row, tm, norm_w, wg, wu, wd):
    t, d = x_all.shape
    full = lambda a: pl.BlockSpec(a.shape, lambda i: (0,) * a.ndim)
    return pl.pallas_call(
        _ffn_kernel,
        grid=(t // tm,),
        in_specs=[pl.BlockSpec((tm, d), lambda i: (i, 0)),
                  _mod_spec(3, tile_row), _mod_spec(4, tile_row), _mod_spec(5, tile_row),
                  full(norm_w), full(wg), full(wu), full(wd)],
        out_specs=pl.BlockSpec((tm, d), lambda i: (i, 0)),
        out_shape=jax.ShapeDtypeStruct((t, d), F32),
        compiler_params=_params(1),
        name="ffn_dense",
    )(x_all, mod3, mod3, mod3, norm_w, wg, wu, wd)


def _moe_kernel(x_ref, sh_ref, sc_ref, g2_ref, nw_ref, fw_ref, wr_ref, wg_ref, wu_ref, wd_ref,
                o_ref, h_ref, gate_ref, acc_ref):
    e = pl.program_id(1)
    f = pl.program_id(2)
    lane = lax.broadcasted_iota(jnp.int32, gate_ref.shape, 1).astype(F32)

    @pl.when((e == 0) & (f == 0))
    def _():
        hf = _norm_mod(x_ref[...], nw_ref[...], sh_ref[...], sc_ref[...])
        h_ref[...] = hf.astype(BF16)
        acc_ref[...] = jnp.zeros_like(acc_ref)
        logits = jnp.dot(hf, wr_ref[...], precision=HIGHEST, preferred_element_type=F32)
        logits = jnp.where(lane < N_EXPERTS, logits, -jnp.inf)
        m1 = jnp.max(logits, axis=-1, keepdims=True)
        i1 = jnp.min(jnp.where(logits == m1, lane, float(Z_PAD)), axis=-1, keepdims=True)
        rest = jnp.where(lane == i1, -jnp.inf, logits)
        m2 = jnp.max(rest, axis=-1, keepdims=True)
        i2 = jnp.min(jnp.where(rest == m2, lane, float(Z_PAD)), axis=-1, keepdims=True)
        e2 = jnp.exp(m2 - m1)
        w1 = 1.0 / (1.0 + e2)
        gate_ref[...] = jnp.where(lane == i1, w1, 0.0) + jnp.where(lane == i2, e2 * w1, 0.0)

    h = h_ref[...]
    ge = jnp.sum(jnp.where(lane == e.astype(F32), gate_ref[...], 0.0), axis=-1, keepdims=True)
    a = _swiglu_hidden(h, wg_ref.at[0], wu_ref.at[0], wg_ref.shape[2], ge)
    acc_ref[...] += _dot(a, wd_ref[0])

    @pl.when((e == pl.num_programs(1) - 1) & (f == pl.num_programs(2) - 1))
    def _():
        y = x_ref[...] + g2_ref[...] * acc_ref[...]
        o_ref[...] = y * lax.rsqrt(jnp.mean(y * y, axis=-1, keepdims=True) + EPS) * fw_ref[...]


def _moe(x_lat, mod3, tile_row, tm, norm_w, final_w, w_router, wg, wu, wd, tf):
    t, d = x_lat.shape
    n_e, _, f_dim = wg.shape
    mspec = lambda col: pl.BlockSpec((None, 1, D_MODEL), lambda i, e, j: (tile_row(i), 0, col))
    const = lambda a: pl.BlockSpec(a.shape, lambda i, e, j: (0,) * a.ndim)
    return pl.pallas_call(
        _moe_kernel,
        grid=(t // tm, n_e, f_dim // tf),
        in_specs=[pl.BlockSpec((tm, d), lambda i, e, j: (i, 0)),
                  mspec(3), mspec(4), mspec(5), const(norm_w), const(final_w), const(w_router),
                  pl.BlockSpec((1, d, tf), lambda i, e, j: (e, 0, j)),
                  pl.BlockSpec((1, d, tf), lambda i, e, j: (e, 0, j)),
                  pl.BlockSpec((1, tf, d), lambda i, e, j: (e, j, 0))],
        out_specs=pl.BlockSpec((tm, d), lambda i, e, j: (i, 0)),
        out_shape=jax.ShapeDtypeStruct((t, d), F32),
        scratch_shapes=[pltpu.VMEM((tm, d), BF16), pltpu.VMEM((tm, Z_PAD), F32), pltpu.VMEM((tm, d), F32)],
        compiler_params=_params(3),
        name="moe",
    )(x_lat, mod3, mod3, mod3, norm_w, final_w, w_router, wg, wu, wd)


def _permute_w_in(w):
    o = 0
    parts = {}
    for name, width in (("k", GLA_KEY), ("v", GLA_VAL), ("zf", GLA_RANK), ("zb", GLA_RANK), ("u", S5_WIDTH),
                        ("q", GLA_KEY), ("r", GLA_VAL), ("ga", D_MODEL), ("gm", D_MODEL)):
        parts[name] = w[:, o:o + width]
        o += width
    pad = jnp.zeros((w.shape[0], Z_PAD - 2 * GLA_RANK), w.dtype)
    cols = [parts[n] for n in ("k", "v", "u", "q", "r", "ga", "gm", "zf", "zb")] + [pad]
    return jnp.concatenate(cols, axis=1).astype(BF16)


def _low_rank_gate(lr_f, lr_b, bias_f, bias_b):
    lr = jnp.zeros((Z_PAD, 2 * GLA_KEY), F32)
    lr = lr.at[:GLA_RANK, :GLA_KEY].set(lr_f).at[GLA_RANK:2 * GLA_RANK, GLA_KEY:].set(lr_b)
    return lr.astype(BF16), jnp.concatenate([bias_f, bias_b])[None, :]


def kernel(x, c, ctx, c_ctx, w_mod, b_mod, norm1_w, norm2_w, final_norm_w, w_in, gla_lr_f, gla_lr_b, gla_bias_f, gla_bias_b, gla_norm_w, s5_a_re_f, s5_a_im_f, s5_log_dt_f, s5_a_re_b, s5_a_im_b, s5_log_dt_b, s5_b_re, s5_b_im, s5_c_re, s5_c_im, s5_d, s5_w_glu, s5_b_glu, w_gla_proj, w_s5_proj, w_out, ffn_w_gate, ffn_w_up, ffn_w_down, moe_router, moe_w_gate, moe_w_up, moe_w_down):
    batch, n_lat, d = x.shape
    n_ctx = ctx.shape[1]
    depth = w_mod.shape[0]
    assert d == D_MODEL and batch < 8 and depth == 2
    assert n_lat % n_ctx == 0 and n_ctx % GLA_CHUNK == 0 and (n_lat // GRID_W) % S5_CHUNK == 0
    t_lat, t_ctx = batch * n_lat, batch * n_ctx
    t_all = t_lat + t_ctx

    tm = math.gcd(512, t_ctx)
    assert n_lat % tm == 0
    lat_tiles_per_batch = n_lat // tm
    n_lat_tiles = t_lat // tm
    tile_row = lambda i: jnp.where(i < n_lat_tiles, i // lat_tiles_per_batch, batch)

    cond = jnp.zeros((8, d), F32).at[:batch].set(c).at[batch].set(c_ctx)
    mod = _modulation(cond, w_mod, b_mod)

    x_all = jnp.concatenate([x.reshape(t_lat, d), ctx.reshape(t_ctx, d)], axis=0)
    out = None
    for l in range(depth):
        last = l == depth - 1
        mod3 = mod[l].reshape(8, 1, 6 * d)
        lr, lb = _low_rank_gate(gla_lr_f[l], gla_lr_b[l], gla_bias_f[l], gla_bias_b[l])
        k, v, u, q, r, ga, gm, gf, gb = _in_projection(
            x_all, mod3, tile_row, tm, norm1_w[l][None, :], _permute_w_in(w_in[l]), lr, lb)

        of, ob = _gla(q, k, v, gf, gb, batch, n_lat // n_ctx, n_ctx)

        bt, ct = _s5_bc_layout(s5_b_re[l], s5_b_im[l], s5_c_re[l], s5_c_im[l])
        y = _s5(u, _s5_param_layout(s5_a_re_f[l], s5_a_im_f[l], s5_log_dt_f[l]),
                _s5_param_layout(s5_a_re_b[l], s5_a_im_b[l], s5_log_dt_b[l]), bt, ct, batch, n_lat, n_ctx)

        merge_w = (gla_norm_w[l][None, :], s5_d[l][None, :], w_gla_proj[l].astype(BF16),
                   s5_w_glu[l].astype(BF16), s5_b_glu[l][None, :], w_s5_proj[l].astype(BF16),
                   w_out[l].astype(BF16))
        n_tiles = n_lat_tiles if last else t_all // tm
        x_mid = _merge(of, ob, y, u, r, ga, gm, x_all, mod3, tile_row, tm, n_tiles, merge_w)

        if l % 2 == 0:
            i = l // 2
            x_all = _ffn(x_mid, mod3, tile_row, tm, norm2_w[l][None, :], ffn_w_gate[i].astype(BF16),
                         ffn_w_up[i].astype(BF16), ffn_w_down[i].astype(BF16))
        else:
            i = l // 2
            tm_f = math.gcd(1024, n_lat)
            row_f = lambda j, tm_f=tm_f: j // (n_lat // tm_f)
            w_router = jnp.zeros((d, Z_PAD), F32).at[:, :N_EXPERTS].set(moe_router[i])
            res = _moe(x_mid, mod3, row_f, tm_f, norm2_w[l][None, :], final_norm_w[None, :], w_router,
                       moe_w_gate[i].astype(BF16), moe_w_up[i].astype(BF16), moe_w_down[i].astype(BF16), MOE_F_TILE)
            out = res.reshape(batch, n_lat, d)
    return out
```

```python
import functools
import math

import jax
import jax.numpy as jnp
from jax import lax
from jax.experimental import pallas as pl
from jax.experimental.pallas import tpu as pltpu

F32 = jnp.float32
BF16 = jnp.bfloat16
HIGHEST = lax.Precision.HIGHEST

D_MODEL = 1024
GRID_W = 64
GLA_HEADS = 4
GLA_DK = 64
GLA_DV = 128
GLA_KEY = GLA_HEADS * GLA_DK
GLA_VAL = GLA_HEADS * GLA_DV
GLA_RANK = 16
GLA_GATE_NORM = 16.0
GLA_CHUNK = 64
S5_WIDTH = 512
S5_GROUP = 16
S5_GROUPS = S5_WIDTH // S5_GROUP
S5_STATE = 64
S5_CHUNK = 16
S5_CW = S5_CHUNK * S5_GROUP
S5_SW = 2 * S5_STATE
S5_PAD = GRID_W // 2
S5_TILE_GROUPS = 128 // S5_GROUP
S5_GROUP_SHIFT = 4
N_EXPERTS = 8
EPS = 1e-6
MXU_DIM = 256
MOE_F_TILE = 7 * MXU_DIM
Z_PAD = 128

VMEM_LIMIT = 56 * 1024 * 1024

SEC_K = (0, GLA_KEY)
SEC_V = (SEC_K[1], SEC_K[1] + GLA_VAL)
SEC_U = (SEC_V[1], SEC_V[1] + S5_WIDTH)
SEC_Q = (SEC_U[1], SEC_U[1] + GLA_KEY)
SEC_R = (SEC_Q[1], SEC_Q[1] + GLA_VAL)
SEC_GA = (SEC_R[1], SEC_R[1] + D_MODEL)
SEC_GM = (SEC_GA[1], SEC_GA[1] + D_MODEL)
SEC_Z = (SEC_GM[1], SEC_GM[1] + Z_PAD)
IN_PERM_WIDTH = SEC_Z[1]


def _params(n_axes):
    return pltpu.CompilerParams(dimension_semantics=("arbitrary",) * n_axes,
                                vmem_limit_bytes=VMEM_LIMIT)


def _silu(x):
    return x * jax.nn.sigmoid(x)


def _norm_mod(x, nw, sh, sc):
    y = x * lax.rsqrt(jnp.mean(x * x, axis=-1, keepdims=True) + EPS) * nw
    return y * (1.0 + sc) + sh


def _dot(a, b):
    return jnp.dot(a, b, preferred_element_type=F32)


def _dot_nt(a, b):
    return lax.dot_general(a, b, (((1,), (1,)), ((), ())), preferred_element_type=F32)


def _dot_tn(a, b):
    return lax.dot_general(a, b, (((0,), (0,)), ((), ())), preferred_element_type=F32)


def _mod_kernel(c_ref, w_ref, b_ref, o_ref):
    s = _silu(c_ref[...]).astype(BF16)
    o_ref[0] = _dot(s, w_ref[0].astype(BF16)) + b_ref[0]


def _modulation(cond, w_mod, b_mod):
    depth, d, n = w_mod.shape
    tn = 1536
    return pl.pallas_call(
        _mod_kernel,
        grid=(depth, n // tn),
        in_specs=[pl.BlockSpec((8, d), lambda l, j: (0, 0)),
                  pl.BlockSpec((1, d, tn), lambda l, j: (l, 0, j)),
                  pl.BlockSpec((1, 1, tn), lambda l, j: (l, 0, j))],
        out_specs=pl.BlockSpec((1, 8, tn), lambda l, j: (l, 0, j)),
        out_shape=jax.ShapeDtypeStruct((depth, 8, n), F32),
        compiler_params=_params(2),
        name="modulation",
    )(cond, w_mod, b_mod.reshape(depth, 1, n))


def _inproj_kernel(x_ref, sh_ref, sc_ref, nw_ref, w_ref, lr_ref, lb_ref,
                   k_ref, v_ref, u_ref, q_ref, r_ref, ga_ref, gm_ref, gf_ref, gb_ref):
    h = _norm_mod(x_ref[...], nw_ref[...], sh_ref[...], sc_ref[...]).astype(BF16)

    def sec(s):
        return _dot(h, w_ref[:, s[0]:s[1]])

    k_ref[...] = sec(SEC_K).astype(BF16)
    v_ref[...] = sec(SEC_V).astype(BF16)
    u_ref[...] = sec(SEC_U)
    q_ref[...] = (sec(SEC_Q) * GLA_DK ** -0.5).astype(BF16)
    r_ref[...] = sec(SEC_R).astype(BF16)
    ga_ref[...] = sec(SEC_GA).astype(BF16)
    gm_ref[...] = sec(SEC_GM).astype(BF16)
    z = sec(SEC_Z).astype(BF16)
    pre = _dot(z, lr_ref[...]) + lb_ref[...]
    g = (jnp.minimum(pre, 0.0) - jnp.log(1.0 + jnp.exp(-jnp.abs(pre)))) * (1.0 / GLA_GATE_NORM)
    gf_ref[...] = g[:, :GLA_KEY]
    gb_ref[...] = g[:, GLA_KEY:]


def _mod_spec(col, tile_row):
    return pl.BlockSpec((None, 1, D_MODEL), lambda i: (tile_row(i), 0, col))


def _in_projection(x_all, mod3, tile_row, tm, norm_w, w_perm, lr, lb):
    t, d = x_all.shape
    widths = (GLA_KEY, GLA_VAL, S5_WIDTH, GLA_KEY, GLA_VAL, D_MODEL, D_MODEL)
    tok = lambda w: pl.BlockSpec((tm, w), lambda i: (i, 0))
    full = lambda a: pl.BlockSpec(a.shape, lambda i: (0,) * a.ndim)
    return pl.pallas_call(
        _inproj_kernel,
        grid=(t // tm,),
        in_specs=[tok(d), _mod_spec(0, tile_row), _mod_spec(1, tile_row),
                  full(norm_w), full(w_perm), full(lr), full(lb)],
        out_specs=[tok(w) for w in widths] + [tok(GLA_KEY), tok(GLA_KEY)],
        out_shape=[jax.ShapeDtypeStruct((t, w), F32 if i == 2 else BF16) for i, w in enumerate(widths)]
        + [jax.ShapeDtypeStruct((t, GLA_KEY), F32)] * 2,
        compiler_params=_params(1),
        name="in_projection",
    )(x_all, mod3, mod3, norm_w, w_perm, lr, lb)


def _gla_direction(q_ref, k_ref, v_ref, g_ref, o_ref, s_ref, reverse):
    n = q_ref.shape[0]
    nc = n // GLA_CHUNK
    assert nc * GLA_DK == n
    shift = GLA_CHUNK.bit_length() - 1
    row = lax.broadcasted_iota(jnp.int32, (n, n), 0)
    col = lax.broadcasted_iota(jnp.int32, (n, n), 1)
    same = (row >> shift) == (col >> shift)
    tri = same & ((col >= row) if reverse else (col <= row))
    tri_b = tri.astype(BF16)
    g = g_ref[...]
    g_hi = g.astype(BF16)
    g_lo = (g - g_hi.astype(F32)).astype(BF16)
    bc = _dot(tri_b, g_hi) + _dot(tri_b, g_lo)
    last = 0 if reverse else GLA_CHUNK - 1
    bl3 = bc.reshape(nc, GLA_CHUNK, GLA_KEY)[:, last:last + 1, :]
    bl = jnp.broadcast_to(bl3, (nc, GLA_CHUNK, GLA_KEY)).reshape(n, GLA_KEY)
    q = q_ref[...].astype(F32)
    k = k_ref[...].astype(F32)
    qd = (q * jnp.exp(bc)).astype(BF16)
    kd = (k * jnp.exp(-bc)).astype(BF16)
    kl = (k * jnp.exp(bl - bc)).astype(BF16)
    dec = jnp.exp(bl3)
    v = v_ref[...]
    order = list(range(nc))[::-1] if reverse else list(range(nc))
    outs = []
    for h in range(GLA_HEADS):
        ks = slice(h * GLA_DK, (h + 1) * GLA_DK)
        vh = v[:, h * GLA_DV:(h + 1) * GLA_DV]
        qh, klh = qd[:, ks], kl[:, ks]
        att = jnp.where(tri, _dot_nt(qh, kd[:, ks]), 0.0).astype(BF16)
        klm = jnp.where(same, jnp.concatenate([klh] * nc, axis=1), jnp.zeros((), BF16))
        qdm = jnp.where(same, jnp.concatenate([qh] * nc, axis=1), jnp.zeros((), BF16))
        ds = _dot_tn(vh, klm)
        st = s_ref[h]
        entering = {}
        for c in order:
            entering[c] = st
            st = st * dec[c][:, ks] + ds[:, c * GLA_DK:(c + 1) * GLA_DK]
        s_ref[h] = st
        st_all = jnp.concatenate([entering[c] for c in range(nc)], axis=1).astype(BF16)
        outs.append(_dot(att, vh) + _dot_nt(qdm, st_all))
    o_ref[...] = jnp.concatenate(outs, axis=1)


def _gla_kernel(qf_ref, kf_ref, vf_ref, gf_ref, qb_ref, kb_ref, vb_ref, gb_ref,
                of_ref, ob_ref, sf_ref, sb_ref):
    @pl.when(pl.program_id(1) == 0)
    def _():
        sf_ref[...] = jnp.zeros_like(sf_ref)
        sb_ref[...] = jnp.zeros_like(sb_ref)

    _gla_direction(qf_ref, kf_ref, vf_ref, gf_ref, of_ref, sf_ref, False)
    _gla_direction(qb_ref, kb_ref, vb_ref, gb_ref, ob_ref, sb_ref, True)


def _gla(q, k, v, gf, gb, batch, n_lat_blocks, blk):
    t = q.shape[0]
    ctx0 = batch * n_lat_blocks

    def fwd(b, i):
        return (jnp.where(i == 0, ctx0 + b, b * n_lat_blocks + i - 1), 0)

    def bwd(b, i):
        return (jnp.where(i == 0, ctx0 + b, b * n_lat_blocks + n_lat_blocks - i), 0)

    spec = lambda w, m: pl.BlockSpec((blk, w), m)
    state = pltpu.VMEM((GLA_HEADS, GLA_DV, GLA_DK), F32)
    return pl.pallas_call(
        _gla_kernel,
        grid=(batch, n_lat_blocks + 1),
        in_specs=[spec(GLA_KEY, fwd), spec(GLA_KEY, fwd), spec(GLA_VAL, fwd), spec(GLA_KEY, fwd),
                  spec(GLA_KEY, bwd), spec(GLA_KEY, bwd), spec(GLA_VAL, bwd), spec(GLA_KEY, bwd)],
        out_specs=[spec(GLA_VAL, fwd), spec(GLA_VAL, bwd)],
        out_shape=[jax.ShapeDtypeStruct((t, GLA_VAL), F32)] * 2,
        scratch_shapes=[state, state],
        compiler_params=_params(2),
        name="gla_bidir",
    )(q, k, v, gf, q, k, v, gb)


def _s5_group_tables(arow, acol, log_dt, bt, ct, reverse):
    delta = jnp.exp(log_dt)
    lane_s = lax.broadcasted_iota(jnp.int32, (1, S5_SW), 1)
    sign = jnp.where(lane_s < S5_STATE, -1.0, 1.0)
    ar, ai = arow[0:1, :], arow[1:2, :]
    zr, zi = ar * delta, ai * delta

    def row_power(n):
        mag = jnp.exp(zr * n)
        return mag * jnp.cos(zi * n), mag * jnp.sin(zi * n)

    one = jnp.ones((1, 1), F32)
    l_r, l_i = row_power(one)
    den = ar * ar + ai * ai
    xr, xi = l_r - 1.0, l_i
    cr = (xr * ar + xi * ai) / den
    ci = (xi * ar - xr * ai) / den
    bdup = cr * bt + ci * (pltpu.roll(bt, S5_STATE, 1) * sign)
    bswp = pltpu.roll(bdup, S5_STATE, 1) * sign

    step = lax.broadcasted_iota(jnp.int32, (S5_CHUNK, 1), 0).astype(F32)
    er, ei = row_power(step if reverse else (S5_CHUNK - 1.0 - step))
    to_state = jnp.concatenate(
        [er[s:s + 1, :] * bdup + ei[s:s + 1, :] * bswp for s in range(S5_CHUNK)], axis=0)

    zrc, zic = acol[:, 0:1] * delta, acol[:, 1:2] * delta
    lane_c = lax.broadcasted_iota(jnp.int32, (1, S5_CW), 1)
    t_lane = (lane_c >> S5_GROUP_SHIFT).astype(F32)
    e_k = (S5_CHUNK - 1.0 - t_lane) if reverse else t_lane
    mag = jnp.exp(zrc * e_k)
    pr, pi = mag * jnp.cos(zic * e_k), mag * jnp.sin(zic * e_k)
    ctr, cti = ct[:S5_STATE, :], ct[S5_STATE:, :]
    kr, ki = ctr * pr - cti * pi, ctr * pi + cti * pr
    lam_mag = jnp.exp(zrc)
    lcr, lci = lam_mag * jnp.cos(zic), lam_mag * jnp.sin(zic)
    from_state = jnp.concatenate([kr * lcr - ki * lci, -(kr * lci + ki * lcr)], axis=0)
    krow = jnp.dot(bdup, jnp.concatenate([kr, -ki], axis=0), precision=HIGHEST,
                   preferred_element_type=F32)
    blocks = []
    for s in range(S5_CHUNK):
        if reverse:
            sh = S5_GROUP * (S5_CHUNK - 1 - s)
            blk = pltpu.roll(krow, (S5_CW - sh) % S5_CW, 1) if sh else krow
            blocks.append(jnp.where(lane_c < S5_GROUP * (s + 1), blk, 0.0))
        else:
            sh = S5_GROUP * s
            blk = pltpu.roll(krow, sh, 1) if sh else krow
            blocks.append(jnp.where(lane_c >= sh, blk, 0.0))
    toeplitz = jnp.concatenate(blocks, axis=0)

    wr, wi = row_power(one * float(S5_CHUNK))
    decay = []
    for _ in range(8):
        decay.append((wr, wi * sign))
        wr, wi = wr * wr - wi * wi, 2.0 * wr * wi
    return toeplitz.astype(BF16), to_state.astype(BF16), from_state.astype(BF16), decay


def _cmul(wa, wb, x2d):
    return wa * x2d + wb * pltpu.roll(x2d, S5_STATE, 1)


def _s5_segment_scan(x, h0, decay, lvl0, scr, reverse):
    nb, n, sw = x.shape
    lo = S5_PAD
    first, last = (n - 1, 0) if reverse else (0, n - 1)
    scr[...] = jnp.zeros_like(scr)
    scr[:, lo:lo + n, :] = x
    if h0 is not None:
        carried = _cmul(*decay[lvl0], h0.reshape(nb, sw)).reshape(nb, 1, sw)
        scr[:, lo + first:lo + first + 1, :] = x[:, first:first + 1, :] + carried
    off, lvl = 1, lvl0
    while off < n:
        src = lo + off if reverse else lo - off
        cur = scr[:, lo:lo + n, :]
        sh = scr[:, src:src + n, :].reshape(nb * n, sw)
        scr[:, lo:lo + n, :] = cur + _cmul(*decay[lvl], sh).reshape(nb, n, sw)
        off, lvl = off * 2, lvl + 1
    h_out = scr[:, lo + last:lo + last + 1, :]
    edge = lo + n if reverse else lo - 1
    if h0 is not None:
        scr[:, edge:edge + 1, :] = h0
    start = lo + 1 if reverse else lo - 1
    return scr[:, start:start + n, :], h_out


def _s5_direction(hloc, decay, scr, reverse, nb, n_rq, n_cc):
    n_lat_rows = nb * n_rq * GRID_W
    lat = hloc[:n_lat_rows].reshape(nb, n_rq, GRID_W, S5_SW)
    ctx = hloc[n_lat_rows:].reshape(nb, n_cc, S5_SW)
    h_ctx_rows, h_ctx = _s5_segment_scan(ctx, None, decay, 0, scr, reverse)
    step = lambda h: _cmul(*decay[0], h.reshape(nb * GRID_W, S5_SW)).reshape(nb, GRID_W, S5_SW)
    order = list(range(n_rq))[::-1] if reverse else list(range(n_rq))
    col = lat[:, order[0]]
    for rq in order[1:]:
        col = step(col) + lat[:, rq]
    lvl_col = int(math.log2(n_rq))
    h_col, _ = _s5_segment_scan(col, h_ctx, decay, lvl_col, scr, reverse)
    starts = {order[0]: h_col}
    for prev, rq in zip(order[:-1], order[1:]):
        starts[rq] = step(starts[prev]) + lat[:, prev]
    lat_start = jnp.stack([starts[rq] for rq in range(n_rq)], axis=1)
    return jnp.concatenate([lat_start.reshape(n_lat_rows, S5_SW), h_ctx_rows.reshape(nb * n_cc, S5_SW)], axis=0)


def _s5_kernel(u_ref, arf_ref, arb_ref, acf_ref, acb_ref, ldf_ref, ldb_ref, bt_ref, ct_ref,
               y_ref, xrot_ref, zacc_ref, scr, *, nb, n_lat, n_ctx):
    j = pl.program_id(0) % S5_TILE_GROUPS
    n_rq = n_lat // (GRID_W * S5_CHUNK)
    n_cc = n_ctx // S5_CHUNK
    lat_rows = n_rq * GRID_W
    n_rows = nb * (lat_rows + n_cc)
    lanes = S5_TILE_GROUPS * S5_GROUP
    slot = lax.broadcasted_iota(jnp.int32, (1, lanes), 1) >> S5_GROUP_SHIFT
    wrap = S5_TILE_GROUPS - 1

    runs = []
    for b in range(nb):
        for rq in range(n_rq):
            runs.append((lambda t, b=b, rq=rq: pl.ds(b * n_lat + (rq * S5_CHUNK + t) * GRID_W, GRID_W),
                         b * lat_rows + rq * GRID_W, GRID_W))
    for b in range(nb):
        runs.append((lambda t, b=b: pl.ds(nb * n_lat + b * n_ctx + t, n_cc, stride=S5_CHUNK),
                     nb * lat_rows + b * n_cc, n_cc))

    @pl.when(j == 0)
    def _():
        for tok_rows, row0, n in runs:
            for t in range(S5_CHUNK):
                x = pltpu.roll(u_ref[tok_rows(t), :], (t & wrap) * S5_GROUP, 1)
                xrot_ref[t, pl.ds(row0, n), :] = x.astype(BF16)

    halves = []
    for th in range(2):
        acc = xrot_ref[th * S5_TILE_GROUPS]
        for tp in range(1, S5_TILE_GROUPS):
            acc = jnp.where(slot == ((tp + j) & wrap), xrot_ref[th * S5_TILE_GROUPS + tp], acc)
        halves.append(pltpu.roll(acc.astype(F32), ((-j) & wrap) * S5_GROUP, 1))
    u = jnp.concatenate(halves, axis=1).astype(BF16)

    y = None
    for reverse, ar_ref, ac_ref, ld_ref in ((False, arf_ref, acf_ref, ldf_ref), (True, arb_ref, acb_ref, ldb_ref)):
        toeplitz, to_state, from_state, decay = _s5_group_tables(
            ar_ref[0], ac_ref[0], ld_ref[0], bt_ref[0], ct_ref[0], reverse)
        hstart = _s5_direction(_dot(u, to_state), decay, scr, reverse, nb, n_rq, n_cc)
        part = _dot(u, toeplitz) + _dot(hstart.astype(BF16), from_state)
        y = part if y is None else y + part

    for th in range(2):
        rolled = pltpu.roll(y[:, th * lanes:(th + 1) * lanes], j * S5_GROUP, 1)
        for tp in range(S5_TILE_GROUPS):
            base = (th * S5_TILE_GROUPS + tp) * n_rows
            for r0 in range(0, n_rows, lanes):
                nr = min(lanes, n_rows - r0)
                own = jnp.broadcast_to(slot == ((tp + j) & wrap), (nr, lanes))
                pltpu.store(zacc_ref.at[pl.ds(base + r0, nr), :], rolled[r0:r0 + nr], mask=own)

    @pl.when(j == S5_TILE_GROUPS - 1)
    def _():
        for tok_rows, row0, n in runs:
            for t in range(S5_CHUNK):
                z = zacc_ref[pl.ds(t * n_rows + row0, n), :]
                y_ref[tok_rows(t), :] = pltpu.roll(z, ((-t) & wrap) * S5_GROUP, 1)


def _s5(u, params_f, params_b, bt, ct, nb, n_lat, n_ctx):
    t, width = u.shape
    lanes = S5_TILE_GROUPS * S5_GROUP
    n_rows = nb * (n_lat + n_ctx) // S5_CHUNK
    tile = pl.BlockSpec((t, lanes), lambda g: (0, g // S5_TILE_GROUPS))
    grp = lambda a: pl.BlockSpec((1,) + a.shape[1:], lambda g: (g, 0, 0))
    arf, acf, ldf = params_f
    arb, acb, ldb = params_b
    ops = (arf, arb, acf, acb, ldf, ldb, bt, ct)
    return pl.pallas_call(
        functools.partial(_s5_kernel, nb=nb, n_lat=n_lat, n_ctx=n_ctx),
        grid=(S5_GROUPS,),
        in_specs=[tile] + [grp(a) for a in ops],
        out_specs=tile,
        out_shape=jax.ShapeDtypeStruct((t, width), F32),
        scratch_shapes=[pltpu.VMEM((S5_CHUNK, n_rows, lanes), BF16),
                        pltpu.VMEM((S5_CHUNK * n_rows, lanes), F32),
                        pltpu.VMEM((nb, 2 * S5_PAD + GRID_W, S5_SW), F32)],
        compiler_params=_params(1),
        name="s5_bidir",
    )(u, *ops)


def _s5_param_layout(a_re, a_im, log_dt):
    arow = jnp.stack([jnp.concatenate([a_re, a_re], axis=-1), jnp.concatenate([a_im, a_im], axis=-1)], axis=1)
    acol = jnp.stack([a_re, a_im], axis=-1)
    return arow, acol, log_dt[:, None, None]


def _s5_bc_layout(b_re, b_im, c_re, c_im):
    bt = jnp.concatenate([b_re.transpose(0, 2, 1), b_im.transpose(0, 2, 1)], axis=-1)
    ct = jnp.concatenate([c_re.transpose(0, 2, 1), c_im.transpose(0, 2, 1)], axis=1)
    return bt, jnp.tile(ct, (1, 1, S5_CHUNK))


def _gelu_tanh(x):
    return 0.5 * x * (1.0 + jnp.tanh(math.sqrt(2.0 / math.pi) * (x + 0.044715 * (x * x * x))))


def _merge_kernel(of_ref, ob_ref, y_ref, u_ref, r_ref, ga_ref, gm_ref, x_ref, g1_ref,
                  gnw_ref, d_ref, wgp_ref, wglu_ref, bglu_ref, wsp_ref, wout_ref, o_ref):
    o = of_ref[...] + ob_ref[...]
    heads = []
    for h in range(GLA_HEADS):
        oh = o[:, h * GLA_DV:(h + 1) * GLA_DV]
        heads.append(oh * lax.rsqrt(jnp.mean(oh * oh, axis=-1, keepdims=True) + EPS) * gnw_ref[...])
    og = jnp.concatenate(heads, axis=1) * _silu(r_ref[...].astype(F32))
    ya = _dot(og.astype(BF16), wgp_ref[...])
    s = _gelu_tanh(y_ref[...] + d_ref[...] * u_ref[...].astype(F32))
    s = s * jax.nn.sigmoid(_dot(s.astype(BF16), wglu_ref[...]) + bglu_ref[...])
    yb = _dot(s.astype(BF16), wsp_ref[...])
    m = jax.nn.sigmoid(ga_ref[...].astype(F32)) * ya + jax.nn.sigmoid(gm_ref[...].astype(F32)) * yb
    o_ref[...] = x_ref[...] + g1_ref[...] * _dot(m.astype(BF16), wout_ref[...])


def _merge(of, ob, y, u, r, ga, gm, x_all, mod3, tile_row, tm, n_tiles, weights):
    t, d = x_all.shape
    tok = lambda w: pl.BlockSpec((tm, w), lambda i: (i, 0))
    full = lambda a: pl.BlockSpec(a.shape, lambda i: (0,) * a.ndim)
    return pl.pallas_call(
        _merge_kernel,
        grid=(n_tiles,),
        in_specs=[tok(GLA_VAL), tok(GLA_VAL), tok(S5_WIDTH), tok(S5_WIDTH), tok(GLA_VAL),
                  tok(d), tok(d), tok(d), _mod_spec(2, tile_row)] + [full(a) for a in weights],
        out_specs=tok(d),
        out_shape=jax.ShapeDtypeStruct((n_tiles * tm, d), F32),
        compiler_params=_params(1),
        name="merge",
    )(of, ob, y, u, r, ga, gm, x_all, mod3, *weights)


def _swiglu_hidden(h, wg, wu, width, scale=None):
    parts = []
    for c0 in range(0, width, MXU_DIM):
        cols = slice(c0, c0 + MXU_DIM)
        a = _silu(_dot(h, wg[:, cols])) * _dot(h, wu[:, cols])
        parts.append((a if scale is None else a * scale).astype(BF16))
    return jnp.concatenate(parts, axis=1)


def _ffn_kernel(x_ref, sh_ref, sc_ref, g2_ref, nw_ref, wg_ref, wu_ref, wd_ref, o_ref):
    x = x_ref[...]
    h = _norm_mod(x, nw_ref[...], sh_ref[...], sc_ref[...]).astype(BF16)
    a = _swiglu_hidden(h, wg_ref, wu_ref, wg_ref.shape[1])
    o_ref[...] = x + g2_ref[...] * _dot(a, wd_ref[...])


def _ffn(x_all, mod3, tile_row, tm, norm_w, wg, wu, wd):
    t, d = x_all.shape
    full = lambda a: pl.BlockSpec(a.shape, lambda i: (0,) * a.ndim)
    return pl.pallas_call(
        _ffn_kernel,
        grid=(t // tm,),
        in_specs=[pl.BlockSpec((tm, d), lambda i: (i, 0)),
                  _mod_spec(3, tile_row), _mod_spec(4, tile_row), _mod_spec(5, tile_row),
                  full(norm_w), full(wg), full(wu), full(wd)],
        out_specs=pl.BlockSpec((tm, d), lambda i: (i, 0)),
        out_shape=jax.ShapeDtypeStruct((t, d), F32),
        compiler_params=_params(1),
        name="ffn_dense",
    )(x_all, mod3, mod3, mod3, norm_w, wg, wu, wd)


def _moe_kernel(x_ref, sh_ref, sc_ref, g2_ref, nw_ref, fw_ref, wr_ref, wg_ref, wu_ref, wd_ref,
                o_ref, h_ref, gate_ref, acc_ref):
    e = pl.program_id(1)
    f = pl.program_id(2)
    lane = lax.broadcasted_iota(jnp.int32, gate_ref.shape, 1).astype(F32)

    @pl.when((e == 0) & (f == 0))
    def _():
        hf = _norm_mod(x_ref[...], nw_ref[...], sh_ref[...], sc_ref[...])
        h_ref[...] = hf.astype(BF16)
        acc_ref[...] = jnp.zeros_like(acc_ref)
        logits = jnp.dot(hf, wr_ref[...], precision=HIGHEST, preferred_element_type=F32)
        logits = jnp.where(lane < N_EXPERTS, logits, -jnp.inf)
        m1 = jnp.max(logits, axis=-1, keepdims=True)
        i1 = jnp.min(jnp.where(logits == m1, lane, float(Z_PAD)), axis=-1, keepdims=True)
        rest = jnp.where(lane == i1, -jnp.inf, logits)
        m2 = jnp.max(rest, axis=-1, keepdims=True)
        i2 = jnp.min(jnp.where(rest == m2, lane, float(Z_PAD)), axis=-1, keepdims=True)
        e2 = jnp.exp(m2 - m1)
        w1 = 1.0 / (1.0 + e2)
        gate_ref[...] = jnp.where(lane == i1, w1, 0.0) + jnp.where(lane == i2, e2 * w1, 0.0)

    h = h_ref[...]
    ge = jnp.sum(jnp.where(lane == e.astype(F32), gate_ref[...], 0.0), axis=-1, keepdims=True)
    a = _swiglu_hidden(h, wg_ref.at[0], wu_ref.at[0], wg_ref.shape[2], ge)
    acc_ref[...] += _dot(a, wd_ref[0])

    @pl.when((e == pl.num_programs(1) - 1) & (f == pl.num_programs(2) - 1))
    def _():
        y = x_ref[...] + g2_ref[...] * acc_ref[...]
        o_ref[...] = y * lax.rsqrt(jnp.mean(y * y, axis=-1, keepdims=True) + EPS) * fw_ref[...]


def _moe(x_lat, mod3, tile_row, tm, norm_w, final_w, w_router, wg, wu, wd, tf):
    t, d = x_lat.shape
    n_e, _, f_dim = wg.shape
    mspec = lambda col: pl.BlockSpec((None, 1, D_MODEL), lambda i, e, j: (tile_row(i), 0, col))
    const = lambda a: pl.BlockSpec(a.shape, lambda i, e, j: (0,) * a.ndim)
    return pl.pallas_call(
        _moe_kernel,
        grid=(t // tm, n_e, f_dim // tf),
        in_specs=[pl.BlockSpec((tm, d), lambda i, e, j: (i, 0)),
                  mspec(3), mspec(4), mspec(5), const(norm_w), const(final_w), const(w_router),
                  pl.BlockSpec((1, d, tf), lambda i, e, j: (e, 0, j)),
                  pl.BlockSpec((1, d, tf), lambda i, e, j: (e, 0, j)),
                  pl.BlockSpec((1, tf, d), lambda i, e, j: (e, j, 0))],
        out_specs=pl.BlockSpec((tm, d), lambda i, e, j: (i, 0)),
        out_shape=jax.ShapeDtypeStruct((t, d), F32),
        scratch_shapes=[pltpu.VMEM((tm, d), BF16), pltpu.VMEM((tm, Z_PAD), F32), pltpu.VMEM((tm, d), F32)],
        compiler_params=_params(3),
        name="moe",
    )(x_lat, mod3, mod3, mod3, norm_w, final_w, w_router, wg, wu, wd)


def _permute_w_in(w):
    o = 0
    parts = {}
    for name, width in (("k", GLA_KEY), ("v", GLA_VAL), ("zf", GLA_RANK), ("zb", GLA_RANK), ("u", S5_WIDTH),
                        ("q", GLA_KEY), ("r", GLA_VAL), ("ga", D_MODEL), ("gm", D_MODEL)):
        parts[name] = w[:, o:o + width]
        o += width
    pad = jnp.zeros((w.shape[0], Z_PAD - 2 * GLA_RANK), w.dtype)
    cols = [parts[n] for n in ("k", "v", "u", "q", "r", "ga", "gm", "zf", "zb")] + [pad]
    return jnp.concatenate(cols, axis=1).astype(BF16)


def _low_rank_gate(lr_f, lr_b, bias_f, bias_b):
    lr = jnp.zeros((Z_PAD, 2 * GLA_KEY), F32)
    lr = lr.at[:GLA_RANK, :GLA_KEY].set(lr_f).at[GLA_RANK:2 * GLA_RANK, GLA_KEY:].set(lr_b)
    return lr.astype(BF16), jnp.concatenate([bias_f, bias_b])[None, :]


def kernel(x, c, ctx, c_ctx, w_mod, b_mod, norm1_w, norm2_w, final_norm_w, w_in, gla_lr_f, gla_lr_b, gla_bias_f, gla_bias_b, gla_norm_w, s5_a_re_f, s5_a_im_f, s5_log_dt_f, s5_a_re_b, s5_a_im_b, s5_log_dt_b, s5_b_re, s5_b_im, s5_c_re, s5_c_im, s5_d, s5_w_glu, s5_b_glu, w_gla_proj, w_s5_proj, w_out, ffn_w_gate, ffn_w_up, ffn_w_down, moe_router, moe_w_gate, moe_w_up, moe_w_down):
    batch, n_lat, d = x.shape
    n_ctx = ctx.shape[1]
    depth = w_mod.shape[0]
    assert d == D_MODEL and batch < 8 and depth == 2
    assert n_lat % n_ctx == 0 and n_ctx % GLA_CHUNK == 0 and (n_lat // GRID_W) % S5_CHUNK == 0
    t_lat, t_ctx = batch * n_lat, batch * n_ctx
    t_all = t_lat + t_ctx

    tm = math.gcd(512, t_ctx)
    assert n_lat % tm == 0
    lat_tiles_per_batch = n_lat // tm
    n_lat_tiles = t_lat // tm
    tile_row = lambda i: jnp.where(i < n_lat_tiles, i // lat_tiles_per_batch, batch)

    cond = jnp.zeros((8, d), F32).at[:batch].set(c).at[batch].set(c_ctx)
    mod = _modulation(cond, w_mod, b_mod)

    x_all = jnp.concatenate([x.reshape(t_lat, d), ctx.reshape(t_ctx, d)], axis=0)
    out = None
    for l in range(depth):
        last = l == depth - 1
        mod3 = mod[l].reshape(8, 1, 6 * d)
        lr, lb = _low_rank_gate(gla_lr_f[l], gla_lr_b[l], gla_bias_f[l], gla_bias_b[l])
        k, v, u, q, r, ga, gm, gf, gb = _in_projection(
            x_all, mod3, tile_row, tm, norm1_w[l][None, :], _permute_w_in(w_in[l]), lr, lb)

        of, ob = _gla(q, k, v, gf, gb, batch, n_lat // n_ctx, n_ctx)

        bt, ct = _s5_bc_layout(s5_b_re[l], s5_b_im[l], s5_c_re[l], s5_c_im[l])
        y = _s5(u, _s5_param_layout(s5_a_re_f[l], s5_a_im_f[l], s5_log_dt_f[l]),
                _s5_param_layout(s5_a_re_b[l], s5_a_im_b[l], s5_log_dt_b[l]), bt, ct, batch, n_lat, n_ctx)

        merge_w = (gla_norm_w[l][None, :], s5_d[l][None, :], w_gla_proj[l].astype(BF16),
                   s5_w_glu[l].astype(BF16), s5_b_glu[l][None, :], w_s5_proj[l].astype(BF16),
                   w_out[l].astype(BF16))
        n_tiles = n_lat_tiles if last else t_all // tm
        x_mid = _merge(of, ob, y, u, r, ga, gm, x_all, mod3, tile_row, tm, n_tiles, merge_w)

        if l % 2 == 0:
            i = l // 2
            x_all = _ffn(x_mid, mod3, tile_row, tm, norm2_w[l][None, :], ffn_w_gate[i].astype(BF16),
                         ffn_w_up[i].astype(BF16), ffn_w_down[i].astype(BF16))
        else:
            i = l // 2
            tm_f = math.gcd(1024, n_lat)
            row_f = lambda j, tm_f=tm_f: j // (n_lat // tm_f)
            w_router = jnp.zeros((d, Z_PAD), F32).at[:, :N_EXPERTS].set(moe_router[i])
            res = _moe(x_mid, mod3, row_f, tm_f, norm2_w[l][None, :], final_norm_w[None, :], w_router,
                       moe_w_gate[i].astype(BF16), moe_w_up[i].astype(BF16), moe_w_down[i].astype(BF16), MOE_F_TILE)
            out = res.reshape(batch, n_lat, d)
    return out
```

```python
import functools
import math

import jax
import jax.numpy as jnp
from jax import lax
from jax.experimental import pallas as pl
from jax.experimental.pallas import tpu as pltpu

F32 = jnp.float32
BF16 = jnp.bfloat16
HIGHEST = lax.Precision.HIGHEST

D_MODEL = 1024
GRID_W = 64
GLA_HEADS = 4
GLA_DK = 64
GLA_DV = 128
GLA_KEY = GLA_HEADS * GLA_DK
GLA_VAL = GLA_HEADS * GLA_DV
GLA_RANK = 16
GLA_GATE_NORM = 16.0
GLA_CHUNK = 64
S5_WIDTH = 512
S5_GROUP = 16
S5_GROUPS = S5_WIDTH // S5_GROUP
S5_STATE = 64
S5_CHUNK = 16
S5_CW = S5_CHUNK * S5_GROUP
S5_SW = 2 * S5_STATE
S5_PAD = GRID_W // 2
S5_TILE_GROUPS = 128 // S5_GROUP
S5_GROUP_SHIFT = 4
N_EXPERTS = 8
EPS = 1e-6
MXU_DIM = 256
MOE_F_TILE = 7 * MXU_DIM
Z_PAD = 128

VMEM_LIMIT = 56 * 1024 * 1024

SEC_K = (0, GLA_KEY)
SEC_V = (SEC_K[1], SEC_K[1] + GLA_VAL)
SEC_U = (SEC_V[1], SEC_V[1] + S5_WIDTH)
SEC_Q = (SEC_U[1], SEC_U[1] + GLA_KEY)
SEC_R = (SEC_Q[1], SEC_Q[1] + GLA_VAL)
SEC_GA = (SEC_R[1], SEC_R[1] + D_MODEL)
SEC_GM = (SEC_GA[1], SEC_GA[1] + D_MODEL)
SEC_Z = (SEC_GM[1], SEC_GM[1] + Z_PAD)
IN_PERM_WIDTH = SEC_Z[1]


def _params(n_axes):
    return pltpu.CompilerParams(dimension_semantics=("arbitrary",) * n_axes,
                                vmem_limit_bytes=VMEM_LIMIT)


def _silu(x):
    return x * jax.nn.sigmoid(x)


def _norm_mod(x, nw, sh, sc):
    y = x * lax.rsqrt(jnp.mean(x * x, axis=-1, keepdims=True) + EPS) * nw
    return y * (1.0 + sc) + sh


def _dot(a, b):
    return jnp.dot(a, b, preferred_element_type=F32)


def _dot_nt(a, b):
    return lax.dot_general(a, b, (((1,), (1,)), ((), ())), preferred_element_type=F32)


def _dot_tn(a, b):
    return lax.dot_general(a, b, (((0,), (0,)), ((), ())), preferred_element_type=F32)


def _mod_kernel(c_ref, w_ref, b_ref, o_ref):
    s = _silu(c_ref[...]).astype(BF16)
    o_ref[0] = _dot(s, w_ref[0].astype(BF16)) + b_ref[0]


def _modulation(cond, w_mod, b_mod):
    depth, d, n = w_mod.shape
    tn = 1536
    return pl.pallas_call(
        _mod_kernel,
        grid=(depth, n // tn),
        in_specs=[pl.BlockSpec((8, d), lambda l, j: (0, 0)),
                  pl.BlockSpec((1, d, tn), lambda l, j: (l, 0, j)),
                  pl.BlockSpec((1, 1, tn), lambda l, j: (l, 0, j))],
        out_specs=pl.BlockSpec((1, 8, tn), lambda l, j: (l, 0, j)),
        out_shape=jax.ShapeDtypeStruct((depth, 8, n), F32),
        compiler_params=_params(2),
        name="modulation",
    )(cond, w_mod, b_mod.reshape(depth, 1, n))


def _stream_specs(tm, n_lat_tiles, ctx_off):
    lat = pl.BlockSpec((tm, D_MODEL), lambda i: (jnp.minimum(i, n_lat_tiles - 1), 0))
    ctx = pl.BlockSpec((tm, D_MODEL), lambda i: (jnp.maximum(i - n_lat_tiles, 0) + ctx_off, 0))
    return [lat, ctx]


def _stream_tile(lat_ref, ctx_ref, n_lat_tiles):
    return jnp.where(pl.program_id(0) < n_lat_tiles, lat_ref[...], ctx_ref[...])


def _inproj_kernel(xl_ref, xc_ref, sh_ref, sc_ref, nw_ref, w_ref, lr_ref, lb_ref,
                   k_ref, v_ref, u_ref, q_ref, r_ref, ga_ref, gm_ref, gf_ref, gb_ref, *, n_lat_tiles):
    x = _stream_tile(xl_ref, xc_ref, n_lat_tiles)
    h = _norm_mod(x, nw_ref[...], sh_ref[...], sc_ref[...]).astype(BF16)

    def sec(s):
        return _dot(h, w_ref[:, s[0]:s[1]])

    z = sec(SEC_Z).astype(BF16)
    pre = _dot(z, lr_ref[...]) + lb_ref[...]
    k_ref[...] = sec(SEC_K).astype(BF16)
    v_ref[...] = sec(SEC_V).astype(BF16)
    u_ref[...] = sec(SEC_U)
    q_ref[...] = (sec(SEC_Q) * GLA_DK ** -0.5).astype(BF16)
    r_ref[...] = sec(SEC_R).astype(BF16)
    ga_ref[...] = sec(SEC_GA).astype(BF16)
    gm_ref[...] = sec(SEC_GM).astype(BF16)
    g = (jnp.minimum(pre, 0.0) - jnp.log(1.0 + jnp.exp(-jnp.abs(pre)))) * (1.0 / GLA_GATE_NORM)
    gf_ref[...] = g[:, :GLA_KEY]
    gb_ref[...] = g[:, GLA_KEY:]


def _mod_spec(col, tile_row):
    return pl.BlockSpec((None, 1, D_MODEL), lambda i: (tile_row(i), 0, col))


def _in_projection(stream, t, mod3, tile_row, tm, norm_w, w_perm, lr, lb):
    x_lat, x_ctx, n_lat_tiles, ctx_off = stream
    widths = (GLA_KEY, GLA_VAL, S5_WIDTH, GLA_KEY, GLA_VAL, D_MODEL, D_MODEL)
    tok = lambda w: pl.BlockSpec((tm, w), lambda i: (i, 0))
    full = lambda a: pl.BlockSpec(a.shape, lambda i: (0,) * a.ndim)
    return pl.pallas_call(
        functools.partial(_inproj_kernel, n_lat_tiles=n_lat_tiles),
        grid=(t // tm,),
        in_specs=_stream_specs(tm, n_lat_tiles, ctx_off) + [_mod_spec(0, tile_row), _mod_spec(1, tile_row),
                  full(norm_w), full(w_perm), full(lr), full(lb)],
        out_specs=[tok(w) for w in widths] + [tok(GLA_KEY), tok(GLA_KEY)],
        out_shape=[jax.ShapeDtypeStruct((t, w), F32 if i == 2 else BF16) for i, w in enumerate(widths)]
        + [jax.ShapeDtypeStruct((t, GLA_KEY), F32)] * 2,
        compiler_params=_params(1),
        name="in_projection",
    )(x_lat, x_ctx, mod3, mod3, norm_w, w_perm, lr, lb)


def _gla_kernel(qf_ref, kf_ref, vf_ref, gf_ref, qb_ref, kb_ref, vb_ref, gb_ref,
                of_ref, ob_ref, sf_ref, sb_ref):
    @pl.when(pl.program_id(1) == 0)
    def _():
        sf_ref[...] = jnp.zeros_like(sf_ref)
        sb_ref[...] = jnp.zeros_like(sb_ref)

    n = qf_ref.shape[0]
    nc = n // GLA_CHUNK
    assert nc * GLA_DK == n
    shift = GLA_CHUNK.bit_length() - 1
    row = lax.broadcasted_iota(jnp.int32, (n, n), 0)
    col = lax.broadcasted_iota(jnp.int32, (n, n), 1)
    same = (row >> shift) == (col >> shift)
    dirs = ((qf_ref, kf_ref, vf_ref, gf_ref, of_ref, sf_ref, False),
            (qb_ref, kb_ref, vb_ref, gb_ref, ob_ref, sb_ref, True))
    heads = range(GLA_HEADS)
    ks = [slice(h * GLA_DK, (h + 1) * GLA_DK) for h in heads]
    vs = [slice(h * GLA_DV, (h + 1) * GLA_DV) for h in heads]

    tris, bcs = [], []
    for _, _, _, g_ref, _, _, reverse in dirs:
        tri = same & ((col >= row) if reverse else (col <= row))
        g = g_ref[...]
        g_hi = g.astype(BF16)
        g_lo = (g - g_hi.astype(F32)).astype(BF16)
        tri_b = tri.astype(BF16)
        tris.append(tri)
        bcs.append(_dot(tri_b, g_hi) + _dot(tri_b, g_lo))

    qds, kds, kls, decs, vals = [], [], [], [], []
    for (q_ref, k_ref, v_ref, _, _, _, reverse), bc in zip(dirs, bcs):
        last = 0 if reverse else GLA_CHUNK - 1
        bl3 = bc.reshape(nc, GLA_CHUNK, GLA_KEY)[:, last:last + 1, :]
        bl = jnp.broadcast_to(bl3, (nc, GLA_CHUNK, GLA_KEY)).reshape(n, GLA_KEY)
        q = q_ref[...].astype(F32)
        k = k_ref[...].astype(F32)
        qds.append((q * jnp.exp(bc)).astype(BF16))
        kds.append((k * jnp.exp(-bc)).astype(BF16))
        kls.append((k * jnp.exp(bl - bc)).astype(BF16))
        decs.append(jnp.exp(bl3))
        vals.append(v_ref[...])

    zero = jnp.zeros((), BF16)
    scores = [[_dot_nt(qds[d][:, ks[h]], kds[d][:, ks[h]]) for h in heads] for d in range(2)]
    incs = [[_dot_tn(vals[d][:, vs[h]], jnp.where(same, jnp.concatenate([kls[d][:, ks[h]]] * nc, axis=1), zero))
             for h in heads] for d in range(2)]
    intra = [[_dot(jnp.where(tris[d], scores[d][h], 0.0).astype(BF16), vals[d][:, vs[h]]) for h in heads]
             for d in range(2)]

    st_alls = []
    for d, (_, _, _, _, _, s_ref, reverse) in enumerate(dirs):
        order = list(range(nc))[::-1] if reverse else list(range(nc))
        st_alls.append([])
        for h in heads:
            st = s_ref[h]
            entering = {}
            for c in order:
                entering[c] = st
                st = st * decs[d][c][:, ks[h]] + incs[d][h][:, c * GLA_DK:(c + 1) * GLA_DK]
            s_ref[h] = st
            st_alls[d].append(jnp.concatenate([entering[c] for c in range(nc)], axis=1).astype(BF16))

    inter = [[_dot_nt(jnp.where(same, jnp.concatenate([qds[d][:, ks[h]]] * nc, axis=1), zero), st_alls[d][h])
              for h in heads] for d in range(2)]
    for d, (_, _, _, _, o_ref, _, _) in enumerate(dirs):
        o_ref[...] = jnp.concatenate([intra[d][h] + inter[d][h] for h in heads], axis=1)


def _gla(q, k, v, gf, gb, batch, n_lat_blocks, blk):
    t = q.shape[0]
    ctx0 = batch * n_lat_blocks

    def fwd(b, i):
        return (jnp.where(i == 0, ctx0 + b, b * n_lat_blocks + i - 1), 0)

    def bwd(b, i):
        return (jnp.where(i == 0, ctx0 + b, b * n_lat_blocks + n_lat_blocks - i), 0)

    spec = lambda w, m: pl.BlockSpec((blk, w), m)
    state = pltpu.VMEM((GLA_HEADS, GLA_DV, GLA_DK), F32)
    return pl.pallas_call(
        _gla_kernel,
        grid=(batch, n_lat_blocks + 1),
        in_specs=[spec(GLA_KEY, fwd), spec(GLA_KEY, fwd), spec(GLA_VAL, fwd), spec(GLA_KEY, fwd),
                  spec(GLA_KEY, bwd), spec(GLA_KEY, bwd), spec(GLA_VAL, bwd), spec(GLA_KEY, bwd)],
        out_specs=[spec(GLA_VAL, fwd), spec(GLA_VAL, bwd)],
        out_shape=[jax.ShapeDtypeStruct((t, GLA_VAL), F32)] * 2,
        scratch_shapes=[state, state],
        compiler_params=_params(2),
        name="gla_bidir",
    )(q, k, v, gf, q, k, v, gb)


def _s5_group_tables(arow, acol, log_dt, bt, ct, reverse):
    delta = jnp.exp(log_dt)
    lane_s = lax.broadcasted_iota(jnp.int32, (1, S5_SW), 1)
    sign = jnp.where(lane_s < S5_STATE, -1.0, 1.0)
    ar, ai = arow[0:1, :], arow[1:2, :]
    zr, zi = ar * delta, ai * delta

    def row_power(n):
        mag = jnp.exp(zr * n)
        return mag * jnp.cos(zi * n), mag * jnp.sin(zi * n)

    one = jnp.ones((1, 1), F32)
    l_r, l_i = row_power(one)
    den = ar * ar + ai * ai
    xr, xi = l_r - 1.0, l_i
    cr = (xr * ar + xi * ai) / den
    ci = (xi * ar - xr * ai) / den
    bdup = cr * bt + ci * (pltpu.roll(bt, S5_STATE, 1) * sign)
    bswp = pltpu.roll(bdup, S5_STATE, 1) * sign

    step = lax.broadcasted_iota(jnp.int32, (S5_CHUNK, 1), 0).astype(F32)
    er, ei = row_power(step if reverse else (S5_CHUNK - 1.0 - step))
    to_state = jnp.concatenate(
        [er[s:s + 1, :] * bdup + ei[s:s + 1, :] * bswp for s in range(S5_CHUNK)], axis=0)

    zrc, zic = acol[:, 0:1] * delta, acol[:, 1:2] * delta
    lane_c = lax.broadcasted_iota(jnp.int32, (1, S5_CW), 1)
    t_lane = (lane_c >> S5_GROUP_SHIFT).astype(F32)
    e_k = (S5_CHUNK - 1.0 - t_lane) if reverse else t_lane
    mag = jnp.exp(zrc * e_k)
    pr, pi = mag * jnp.cos(zic * e_k), mag * jnp.sin(zic * e_k)
    ctr, cti = ct[:S5_STATE, :], ct[S5_STATE:, :]
    kr, ki = ctr * pr - cti * pi, ctr * pi + cti * pr
    lam_mag = jnp.exp(zrc)
    lcr, lci = lam_mag * jnp.cos(zic), lam_mag * jnp.sin(zic)
    from_state = jnp.concatenate([kr * lcr - ki * lci, -(kr * lci + ki * lcr)], axis=0)
    krow = jnp.dot(bdup, jnp.concatenate([kr, -ki], axis=0), precision=HIGHEST,
                   preferred_element_type=F32)
    blocks = []
    for s in range(S5_CHUNK):
        if reverse:
            sh = S5_GROUP * (S5_CHUNK - 1 - s)
            blk = pltpu.roll(krow, (S5_CW - sh) % S5_CW, 1) if sh else krow
            blocks.append(jnp.where(lane_c < S5_GROUP * (s + 1), blk, 0.0))
        else:
            sh = S5_GROUP * s
            blk = pltpu.roll(krow, sh, 1) if sh else krow
            blocks.append(jnp.where(lane_c >= sh, blk, 0.0))
    toeplitz = jnp.concatenate(blocks, axis=0)

    wr, wi = row_power(one * float(S5_CHUNK))
    decay = []
    for _ in range(8):
        decay.append((wr, wi * sign))
        wr, wi = wr * wr - wi * wi, 2.0 * wr * wi
    return toeplitz.astype(BF16), to_state.astype(BF16), from_state.astype(BF16), decay


def _cmul(wa, wb, x2d):
    return wa * x2d + wb * pltpu.roll(x2d, S5_STATE, 1)


def _s5_segment_scan(x, h0, decay, lvl0, scr, reverse):
    nb, n, sw = x.shape
    lo = S5_PAD
    first, last = (n - 1, 0) if reverse else (0, n - 1)
    scr[...] = jnp.zeros_like(scr)
    scr[:, lo:lo + n, :] = x
    if h0 is not None:
        carried = _cmul(*decay[lvl0], h0.reshape(nb, sw)).reshape(nb, 1, sw)
        scr[:, lo + first:lo + first + 1, :] = x[:, first:first + 1, :] + carried
    off, lvl = 1, lvl0
    while off < n:
        src = lo + off if reverse else lo - off
        cur = scr[:, lo:lo + n, :]
        sh = scr[:, src:src + n, :].reshape(nb * n, sw)
        scr[:, lo:lo + n, :] = cur + _cmul(*decay[lvl], sh).reshape(nb, n, sw)
        off, lvl = off * 2, lvl + 1
    h_out = scr[:, lo + last:lo + last + 1, :]
    edge = lo + n if reverse else lo - 1
    if h0 is not None:
        scr[:, edge:edge + 1, :] = h0
    start = lo + 1 if reverse else lo - 1
    return scr[:, start:start + n, :], h_out


def _s5_direction(hloc, decay, scr, reverse, nb, n_rq, n_cc):
    n_lat_rows = nb * n_rq * GRID_W
    lat = hloc[:n_lat_rows].reshape(nb, n_rq, GRID_W, S5_SW)
    ctx = hloc[n_lat_rows:].reshape(nb, n_cc, S5_SW)
    h_ctx_rows, h_ctx = _s5_segment_scan(ctx, None, decay, 0, scr, reverse)
    step = lambda h: _cmul(*decay[0], h.reshape(nb * GRID_W, S5_SW)).reshape(nb, GRID_W, S5_SW)
    order = list(range(n_rq))[::-1] if reverse else list(range(n_rq))
    col = lat[:, order[0]]
    for rq in order[1:]:
        col = step(col) + lat[:, rq]
    lvl_col = int(math.log2(n_rq))
    h_col, _ = _s5_segment_scan(col, h_ctx, decay, lvl_col, scr, reverse)
    starts = {order[0]: h_col}
    for prev, rq in zip(order[:-1], order[1:]):
        starts[rq] = step(starts[prev]) + lat[:, prev]
    lat_start = jnp.stack([starts[rq] for rq in range(n_rq)], axis=1)
    return jnp.concatenate([lat_start.reshape(n_lat_rows, S5_SW), h_ctx_rows.reshape(nb * n_cc, S5_SW)], axis=0)


def _s5_kernel(u_ref, arf_ref, arb_ref, acf_ref, acb_ref, ldf_ref, ldb_ref, bt_ref, ct_ref,
               y_ref, xrot_ref, zacc_ref, scr, *, nb, n_lat, n_ctx):
    j = pl.program_id(0) % S5_TILE_GROUPS
    n_rq = n_lat // (GRID_W * S5_CHUNK)
    n_cc = n_ctx // S5_CHUNK
    lat_rows = n_rq * GRID_W
    n_rows = nb * (lat_rows + n_cc)
    lanes = S5_TILE_GROUPS * S5_GROUP
    slot = lax.broadcasted_iota(jnp.int32, (1, lanes), 1) >> S5_GROUP_SHIFT
    wrap = S5_TILE_GROUPS - 1

    runs = []
    for b in range(nb):
        for rq in range(n_rq):
            runs.append((lambda t, b=b, rq=rq: pl.ds(b * n_lat + (rq * S5_CHUNK + t) * GRID_W, GRID_W),
                         b * lat_rows + rq * GRID_W, GRID_W))
    for b in range(nb):
        runs.append((lambda t, b=b: pl.ds(nb * n_lat + b * n_ctx + t, n_cc, stride=S5_CHUNK),
                     nb * lat_rows + b * n_cc, n_cc))

    @pl.when(j == 0)
    def _():
        for tok_rows, row0, n in runs:
            for t in range(S5_CHUNK):
                x = pltpu.roll(u_ref[tok_rows(t), :], (t & wrap) * S5_GROUP, 1)
                xrot_ref[t, pl.ds(row0, n), :] = x.astype(BF16)

    halves = []
    for th in range(2):
        acc = xrot_ref[th * S5_TILE_GROUPS]
        for tp in range(1, S5_TILE_GROUPS):
            acc = jnp.where(slot == ((tp + j) & wrap), xrot_ref[th * S5_TILE_GROUPS + tp], acc)
        halves.append(pltpu.roll(acc.astype(F32), ((-j) & wrap) * S5_GROUP, 1))
    u = jnp.concatenate(halves, axis=1).astype(BF16)

    tabs = [_s5_group_tables(ar_ref[0], ac_ref[0], ld_ref[0], bt_ref[0], ct_ref[0], reverse)
            for reverse, ar_ref, ac_ref, ld_ref in ((False, arf_ref, acf_ref, ldf_ref), (True, arb_ref, acb_ref, ldb_ref))]
    hlocs = [_dot(u, tab[1]) for tab in tabs]
    y = _dot(u, tabs[0][0]) + _dot(u, tabs[1][0])
    hstarts = [_s5_direction(hlocs[d], tabs[d][3], scr, d == 1, nb, n_rq, n_cc).astype(BF16) for d in range(2)]
    y = y + _dot(hstarts[0], tabs[0][2]) + _dot(hstarts[1], tabs[1][2])

    for th in range(2):
        rolled = pltpu.roll(y[:, th * lanes:(th + 1) * lanes], j * S5_GROUP, 1)
        for tp in range(S5_TILE_GROUPS):
            base = (th * S5_TILE_GROUPS + tp) * n_rows
            for r0 in range(0, n_rows, lanes):
                nr = min(lanes, n_rows - r0)
                own = jnp.broadcast_to(slot == ((tp + j) & wrap), (nr, lanes))
                pltpu.store(zacc_ref.at[pl.ds(base + r0, nr), :], rolled[r0:r0 + nr], mask=own)

    @pl.when(j == S5_TILE_GROUPS - 1)
    def _():
        for tok_rows, row0, n in runs:
            for t in range(S5_CHUNK):
                z = zacc_ref[pl.ds(t * n_rows + row0, n), :]
                y_ref[tok_rows(t), :] = pltpu.roll(z, ((-t) & wrap) * S5_GROUP, 1)


def _s5(u, params_f, params_b, bt, ct, nb, n_lat, n_ctx):
    t, width = u.shape
    lanes = S5_TILE_GROUPS * S5_GROUP
    n_rows = nb * (n_lat + n_ctx) // S5_CHUNK
    tile = pl.BlockSpec((t, lanes), lambda g: (0, g // S5_TILE_GROUPS))
    grp = lambda a: pl.BlockSpec((1,) + a.shape[1:], lambda g: (g, 0, 0))
    arf, acf, ldf = params_f
    arb, acb, ldb = params_b
    ops = (arf, arb, acf, acb, ldf, ldb, bt, ct)
    return pl.pallas_call(
        functools.partial(_s5_kernel, nb=nb, n_lat=n_lat, n_ctx=n_ctx),
        grid=(S5_GROUPS,),
        in_specs=[tile] + [grp(a) for a in ops],
        out_specs=tile,
        out_shape=jax.ShapeDtypeStruct((t, width), F32),
        scratch_shapes=[pltpu.VMEM((S5_CHUNK, n_rows, lanes), BF16),
                        pltpu.VMEM((S5_CHUNK * n_rows, lanes), F32),
                        pltpu.VMEM((nb, 2 * S5_PAD + GRID_W, S5_SW), F32)],
        compiler_params=_params(1),
        name="s5_bidir",
    )(u, *ops)


def _s5_param_layout(a_re, a_im, log_dt):
    arow = jnp.stack([jnp.concatenate([a_re, a_re], axis=-1), jnp.concatenate([a_im, a_im], axis=-1)], axis=1)
    acol = jnp.stack([a_re, a_im], axis=-1)
    return arow, acol, log_dt[:, None, None]


def _s5_bc_layout(b_re, b_im, c_re, c_im):
    bt = jnp.concatenate([b_re.transpose(0, 2, 1), b_im.transpose(0, 2, 1)], axis=-1)
    ct = jnp.concatenate([c_re.transpose(0, 2, 1), c_im.transpose(0, 2, 1)], axis=1)
    return bt, jnp.tile(ct, (1, 1, S5_CHUNK))


def _gelu_tanh(x):
    return 0.5 * x * (1.0 + jnp.tanh(math.sqrt(2.0 / math.pi) * (x + 0.044715 * (x * x * x))))


def _merge_kernel(of_ref, ob_ref, y_ref, u_ref, r_ref, ga_ref, gm_ref, xl_ref, xc_ref, g1_ref,
                  gnw_ref, d_ref, wgp_ref, wglu_ref, bglu_ref, wsp_ref, wout_ref, o_ref, *, n_lat_tiles):
    o = of_ref[...] + ob_ref[...]
    heads = []
    for h in range(GLA_HEADS):
        oh = o[:, h * GLA_DV:(h + 1) * GLA_DV]
        heads.append(oh * lax.rsqrt(jnp.mean(oh * oh, axis=-1, keepdims=True) + EPS) * gnw_ref[...])
    og = jnp.concatenate(heads, axis=1) * _silu(r_ref[...].astype(F32))
    ya = _dot(og.astype(BF16), wgp_ref[...])
    s = _gelu_tanh(y_ref[...] + d_ref[...] * u_ref[...])
    s = s * jax.nn.sigmoid(_dot(s.astype(BF16), wglu_ref[...]) + bglu_ref[...])
    yb = _dot(s.astype(BF16), wsp_ref[...])
    m = jax.nn.sigmoid(ga_ref[...].astype(F32)) * ya + jax.nn.sigmoid(gm_ref[...].astype(F32)) * yb
    x = _stream_tile(xl_ref, xc_ref, n_lat_tiles)
    o_ref[...] = x + g1_ref[...] * _dot(m.astype(BF16), wout_ref[...])


def _merge(of, ob, y, u, r, ga, gm, stream, mod3, tile_row, tm, n_tiles, weights):
    x_lat, x_ctx, n_lat_tiles, ctx_off = stream
    d = D_MODEL
    tok = lambda w: pl.BlockSpec((tm, w), lambda i: (i, 0))
    full = lambda a: pl.BlockSpec(a.shape, lambda i: (0,) * a.ndim)
    return pl.pallas_call(
        functools.partial(_merge_kernel, n_lat_tiles=n_lat_tiles),
        grid=(n_tiles,),
        in_specs=[tok(GLA_VAL), tok(GLA_VAL), tok(S5_WIDTH), tok(S5_WIDTH), tok(GLA_VAL), tok(d), tok(d)]
        + _stream_specs(tm, n_lat_tiles, ctx_off) + [_mod_spec(2, tile_row)] + [full(a) for a in weights],
        out_specs=tok(d),
        out_shape=jax.ShapeDtypeStruct((n_tiles * tm, d), F32),
        compiler_params=_params(1),
        name="merge",
    )(of, ob, y, u, r, ga, gm, x_lat, x_ctx, mod3, *weights)


def _swiglu_hidden(h, wg, wu, width, scale=None):
    parts = []
    for c0 in range(0, width, MXU_DIM):
        cols = slice(c0, c0 + MXU_DIM)
        a = _silu(_dot(h, wg[:, cols])) * _dot(h, wu[:, cols])
        parts.append((a if scale is None else a * scale).astype(BF16))
    return jnp.concatenate(parts, axis=1)


def _ffn_kernel(x_ref, sh_ref, sc_ref, g2_ref, nw_ref, wg_ref, wu_ref, wd_ref, o_ref):
    x = x_ref[...]
    h = _norm_mod(x, nw_ref[...], sh_ref[...], sc_ref[...]).astype(BF16)
    a = _swiglu_hidden(h, wg_ref, wu_ref, wg_ref.shape[1])
    o_ref[...] = x + g2_ref[...] * _dot(a, wd_ref[...])


def _ffn(x_all, mod3, tile_row, tm, norm_w, wg, wu, wd):
    t, d = x_all.shape
    full = lambda a: pl.BlockSpec(a.shape, lambda i: (0,) * a.ndim)
    return pl.pallas_call(
        _ffn_kernel,
        grid=(t // tm,),
        in_specs=[pl.BlockSpec((tm, d), lambda i: (i, 0)),
                  _mod_spec(3, tile_row), _mod_spec(4, tile_row), _mod_spec(5, tile_row),
                  full(norm_w), full(wg), full(wu), full(wd)],
        out_specs=pl.BlockSpec((tm, d), lambda i: (i, 0)),
        out_shape=jax.ShapeDtypeStruct((t, d), F32),
        compiler_params=_params(1),
        name="ffn_dense",
    )(x_all, mod3, mod3, mod3, norm_w, wg, wu, wd)


def _moe_kernel(x_ref, sh_ref, sc_ref, g2_ref, nw_ref, fw_ref, wr_ref, wg_ref, wu_ref, wd_ref,
                o_ref, h_ref, gate_ref, acc_ref):
    e = pl.program_id(1)
    f = pl.program_id(2)
    lane = lax.broadcasted_iota(jnp.int32, gate_ref.shape, 1).astype(F32)

    @pl.when((e == 0) & (f == 0))
    def _():
        hf = _norm_mod(x_ref[...], nw_ref[...], sh_ref[...], sc_ref[...])
        h_ref[...] = hf.astype(BF16)
        acc_ref[...] = jnp.zeros_like(acc_ref)
        logits = jnp.dot(hf, wr_ref[...], precision=HIGHEST, preferred_element_type=F32)
        logits = jnp.where(lane < N_EXPERTS, logits, -jnp.inf)
        m1 = jnp.max(logits, axis=-1, keepdims=True)
        i1 = jnp.min(jnp.where(logits == m1, lane, float(Z_PAD)), axis=-1, keepdims=True)
        rest = jnp.where(lane == i1, -jnp.inf, logits)
        m2 = jnp.max(rest, axis=-1, keepdims=True)
        i2 = jnp.min(jnp.where(rest == m2, lane, float(Z_PAD)), axis=-1, keepdims=True)
        e2 = jnp.exp(m2 - m1)
        w1 = 1.0 / (1.0 + e2)
        gate_ref[...] = jnp.where(lane == i1, w1, 0.0) + jnp.where(lane == i2, e2 * w1, 0.0)

    h = h_ref[...]
    ge = jnp.sum(jnp.where(lane == e.astype(F32), gate_ref[...], 0.0), axis=-1, keepdims=True)
    a = _swiglu_hidden(h, wg_ref.at[0], wu_ref.at[0], wg_ref.shape[2], ge)
    acc_ref[...] += _dot(a, wd_ref[0])

    @pl.when((e == pl.num_programs(1) - 1) & (f == pl.num_programs(2) - 1))
    def _():
        y = x_ref[...] + g2_ref[...] * acc_ref[...]
        o_ref[...] = y * lax.rsqrt(jnp.mean(y * y, axis=-1, keepdims=True) + EPS) * fw_ref[...]


def _moe(x_lat, mod3, tile_row, tm, norm_w, final_w, w_router, wg, wu, wd, tf):
    t, d = x_lat.shape
    n_e, _, f_dim = wg.shape
    mspec = lambda col: pl.BlockSpec((None, 1, D_MODEL), lambda i, e, j: (tile_row(i), 0, col))
    const = lambda a: pl.BlockSpec(a.shape, lambda i, e, j: (0,) * a.ndim)
    return pl.pallas_call(
        _moe_kernel,
        grid=(t // tm, n_e, f_dim // tf),
        in_specs=[pl.BlockSpec((tm, d), lambda i, e, j: (i, 0)),
                  mspec(3), mspec(4), mspec(5), const(norm_w), const(final_w), const(w_router),
                  pl.BlockSpec((1, d, tf), lambda i, e, j: (e, 0, j)),
                  pl.BlockSpec((1, d, tf), lambda i, e, j: (e, 0, j)),
                  pl.BlockSpec((1, tf, d), lambda i, e, j: (e, j, 0))],
        out_specs=pl.BlockSpec((tm, d), lambda i, e, j: (i, 0)),
        out_shape=jax.ShapeDtypeStruct((t, d), F32),
        scratch_shapes=[pltpu.VMEM((tm, d), BF16), pltpu.VMEM((tm, Z_PAD), F32), pltpu.VMEM((tm, d), F32)],
        compiler_params=_params(3),
        name="moe",
    )(x_lat, mod3, mod3, mod3, norm_w, final_w, w_router, wg, wu, wd)


def _permute_w_in(w):
    o = 0
    parts = {}
    for name, width in (("k", GLA_KEY), ("v", GLA_VAL), ("zf", GLA_RANK), ("zb", GLA_RANK), ("u", S5_WIDTH),
                        ("q", GLA_KEY), ("r", GLA_VAL), ("ga", D_MODEL), ("gm", D_MODEL)):
        parts[name] = w[:, o:o + width]
        o += width
    pad = jnp.zeros((w.shape[0], Z_PAD - 2 * GLA_RANK), w.dtype)
    cols = [parts[n] for n in ("k", "v", "u", "q", "r", "ga", "gm", "zf", "zb")] + [pad]
    return jnp.concatenate(cols, axis=1).astype(BF16)


def _low_rank_gate(lr_f, lr_b, bias_f, bias_b):
    lr = jnp.zeros((Z_PAD, 2 * GLA_KEY), F32)
    lr = lr.at[:GLA_RANK, :GLA_KEY].set(lr_f).at[GLA_RANK:2 * GLA_RANK, GLA_KEY:].set(lr_b)
    return lr.astype(BF16), jnp.concatenate([bias_f, bias_b])[None, :]


def kernel(x, c, ctx, c_ctx, w_mod, b_mod, norm1_w, norm2_w, final_norm_w, w_in, gla_lr_f, gla_lr_b, gla_bias_f, gla_bias_b, gla_norm_w, s5_a_re_f, s5_a_im_f, s5_log_dt_f, s5_a_re_b, s5_a_im_b, s5_log_dt_b, s5_b_re, s5_b_im, s5_c_re, s5_c_im, s5_d, s5_w_glu, s5_b_glu, w_gla_proj, w_s5_proj, w_out, ffn_w_gate, ffn_w_up, ffn_w_down, moe_router, moe_w_gate, moe_w_up, moe_w_down):
    batch, n_lat, d = x.shape
    n_ctx = ctx.shape[1]
    depth = w_mod.shape[0]
    assert d == D_MODEL and batch < 8 and depth == 2
    assert n_lat % n_ctx == 0 and n_ctx % GLA_CHUNK == 0 and (n_lat // GRID_W) % S5_CHUNK == 0
    t_lat, t_ctx = batch * n_lat, batch * n_ctx
    t_all = t_lat + t_ctx

    tm = math.gcd(512, t_ctx)
    assert n_lat % tm == 0
    lat_tiles_per_batch = n_lat // tm
    n_lat_tiles = t_lat // tm
    tile_row = lambda i: jnp.where(i < n_lat_tiles, i // lat_tiles_per_batch, batch)

    cond = jnp.zeros((8, d), F32).at[:batch].set(c).at[batch].set(c_ctx)
    mod = _modulation(cond, w_mod, b_mod)

    stream = (x.reshape(t_lat, d), ctx.reshape(t_ctx, d), n_lat_tiles, 0)
    out = None
    for l in range(depth):
        last = l == depth - 1
        mod3 = mod[l].reshape(8, 1, 6 * d)
        lr, lb = _low_rank_gate(gla_lr_f[l], gla_lr_b[l], gla_bias_f[l], gla_bias_b[l])
        k, v, u, q, r, ga, gm, gf, gb = _in_projection(
            stream, t_all, mod3, tile_row, tm, norm1_w[l][None, :], _permute_w_in(w_in[l]), lr, lb)

        of, ob = _gla(q, k, v, gf, gb, batch, n_lat // n_ctx, n_ctx)

        bt, ct = _s5_bc_layout(s5_b_re[l], s5_b_im[l], s5_c_re[l], s5_c_im[l])
        y = _s5(u, _s5_param_layout(s5_a_re_f[l], s5_a_im_f[l], s5_log_dt_f[l]),
                _s5_param_layout(s5_a_re_b[l], s5_a_im_b[l], s5_log_dt_b[l]), bt, ct, batch, n_lat, n_ctx)

        merge_w = (gla_norm_w[l][None, :], s5_d[l][None, :], w_gla_proj[l].astype(BF16),
                   s5_w_glu[l].astype(BF16), s5_b_glu[l][None, :], w_s5_proj[l].astype(BF16),
                   w_out[l].astype(BF16))
        n_tiles = n_lat_tiles if last else t_all // tm
        x_mid = _merge(of, ob, y, u, r, ga, gm, stream, mod3, tile_row, tm, n_tiles, merge_w)

        if l % 2 == 0:
            i = l // 2
            x_all = _ffn(x_mid, mod3, tile_row, tm, norm2_w[l][None, :], ffn_w_gate[i].astype(BF16),
                         ffn_w_up[i].astype(BF16), ffn_w_down[i].astype(BF16))
            stream = (x_all, x_all, n_lat_tiles, n_lat_tiles)
        else:
            i = l // 2
            tm_f = math.gcd(1024, n_lat)
            row_f = lambda j, tm_f=tm_f: j // (n_lat // tm_f)
            w_router = jnp.zeros((d, Z_PAD), F32).at[:, :N_EXPERTS].set(moe_router[i])
            res = _moe(x_mid, mod3, row_f, tm_f, norm2_w[l][None, :], final_norm_w[None, :], w_router,
                       moe_w_gate[i].astype(BF16), moe_w_up[i].astype(BF16), moe_w_down[i].astype(BF16), MOE_F_TILE)
            out = res.reshape(batch, n_lat, d)
    return out
```

```python
import functools
import math

import jax
import jax.numpy as jnp
from jax import lax
from jax.experimental import pallas as pl
from jax.experimental.pallas import tpu as pltpu

F32 = jnp.float32
BF16 = jnp.bfloat16
HIGHEST = lax.Precision.HIGHEST

D_MODEL = 1024
GRID_W = 64
GLA_HEADS = 4
GLA_DK = 64
GLA_DV = 128
GLA_KEY = GLA_HEADS * GLA_DK
GLA_VAL = GLA_HEADS * GLA_DV
GLA_RANK = 16
GLA_GATE_NORM = 16.0
GLA_CHUNK = 64
S5_WIDTH = 512
S5_GROUP = 16
S5_GROUPS = S5_WIDTH // S5_GROUP
S5_STATE = 64
S5_CHUNK = 16
S5_CW = S5_CHUNK * S5_GROUP
S5_SW = 2 * S5_STATE
S5_PAD = GRID_W // 2
S5_TILE_GROUPS = 128 // S5_GROUP
S5_GROUP_SHIFT = 4
N_EXPERTS = 8
EPS = 1e-6
MXU_DIM = 256
MOE_F_TILE = 7 * MXU_DIM
Z_PAD = 128

VMEM_LIMIT = 56 * 1024 * 1024

SEC_K = (0, GLA_KEY)
SEC_V = (SEC_K[1], SEC_K[1] + GLA_VAL)
SEC_U = (SEC_V[1], SEC_V[1] + S5_WIDTH)
SEC_Q = (SEC_U[1], SEC_U[1] + GLA_KEY)
SEC_R = (SEC_Q[1], SEC_Q[1] + GLA_VAL)
SEC_GA = (SEC_R[1], SEC_R[1] + D_MODEL)
SEC_GM = (SEC_GA[1], SEC_GA[1] + D_MODEL)
SEC_Z = (SEC_GM[1], SEC_GM[1] + Z_PAD)
IN_PERM_WIDTH = SEC_Z[1]


def _params(n_axes):
    return pltpu.CompilerParams(dimension_semantics=("arbitrary",) * n_axes,
                                vmem_limit_bytes=VMEM_LIMIT)


def _silu(x):
    return x * jax.nn.sigmoid(x)


def _norm_mod(x, nw, sh, sc):
    y = x * lax.rsqrt(jnp.mean(x * x, axis=-1, keepdims=True) + EPS) * nw
    return y * (1.0 + sc) + sh


def _dot(a, b):
    return jnp.dot(a, b, preferred_element_type=F32)


def _dot_nt(a, b):
    return lax.dot_general(a, b, (((1,), (1,)), ((), ())), preferred_element_type=F32)


def _dot_tn(a, b):
    return lax.dot_general(a, b, (((0,), (0,)), ((), ())), preferred_element_type=F32)


def _mod_kernel(c_ref, w_ref, b_ref, o_ref):
    s = _silu(c_ref[...]).astype(BF16)
    o_ref[0] = _dot(s, w_ref[0].astype(BF16)) + b_ref[0]


def _modulation(cond, w_mod, b_mod):
    depth, d, n = w_mod.shape
    tn = 1536
    return pl.pallas_call(
        _mod_kernel,
        grid=(depth, n // tn),
        in_specs=[pl.BlockSpec((8, d), lambda l, j: (0, 0)),
                  pl.BlockSpec((1, d, tn), lambda l, j: (l, 0, j)),
                  pl.BlockSpec((1, 1, tn), lambda l, j: (l, 0, j))],
        out_specs=pl.BlockSpec((1, 8, tn), lambda l, j: (l, 0, j)),
        out_shape=jax.ShapeDtypeStruct((depth, 8, n), F32),
        compiler_params=_params(2),
        name="modulation",
    )(cond, w_mod, b_mod.reshape(depth, 1, n))


def _stream_specs(tm, n_lat_tiles, ctx_off):
    lat = pl.BlockSpec((tm, D_MODEL), lambda i: (jnp.minimum(i, n_lat_tiles - 1), 0))
    ctx = pl.BlockSpec((tm, D_MODEL), lambda i: (jnp.maximum(i - n_lat_tiles, 0) + ctx_off, 0))
    return [lat, ctx]


def _stream_tile(lat_ref, ctx_ref, n_lat_tiles):
    return jnp.where(pl.program_id(0) < n_lat_tiles, lat_ref[...], ctx_ref[...])


def _inproj_kernel(xl_ref, xc_ref, sh_ref, sc_ref, nw_ref, w_ref, lr_ref, lb_ref,
                   k_ref, v_ref, u_ref, q_ref, r_ref, ga_ref, gm_ref, gf_ref, gb_ref, *, n_lat_tiles):
    x = _stream_tile(xl_ref, xc_ref, n_lat_tiles)
    h = _norm_mod(x, nw_ref[...], sh_ref[...], sc_ref[...]).astype(BF16)

    def sec(s):
        return _dot(h, w_ref[:, s[0]:s[1]])

    z = sec(SEC_Z).astype(BF16)
    pre = _dot(z, lr_ref[...]) + lb_ref[...]
    k_ref[...] = sec(SEC_K).astype(BF16)
    v_ref[...] = sec(SEC_V).astype(BF16)
    u_ref[...] = sec(SEC_U)
    q_ref[...] = (sec(SEC_Q) * GLA_DK ** -0.5).astype(BF16)
    r_ref[...] = _silu(sec(SEC_R)).astype(BF16)
    ga_ref[...] = jax.nn.sigmoid(sec(SEC_GA)).astype(BF16)
    gm_ref[...] = jax.nn.sigmoid(sec(SEC_GM)).astype(BF16)
    g = (jnp.minimum(pre, 0.0) - jnp.log(1.0 + jnp.exp(-jnp.abs(pre)))) * (1.0 / GLA_GATE_NORM)
    gf_ref[...] = g[:, :GLA_KEY]
    gb_ref[...] = g[:, GLA_KEY:]


def _mod_spec(col, tile_row):
    return pl.BlockSpec((None, 1, D_MODEL), lambda i: (tile_row(i), 0, col))


def _in_projection(stream, t, mod3, tile_row, tm, norm_w, w_perm, lr, lb):
    x_lat, x_ctx, n_lat_tiles, ctx_off = stream
    widths = (GLA_KEY, GLA_VAL, S5_WIDTH, GLA_KEY, GLA_VAL, D_MODEL, D_MODEL)
    tok = lambda w: pl.BlockSpec((tm, w), lambda i: (i, 0))
    full = lambda a: pl.BlockSpec(a.shape, lambda i: (0,) * a.ndim)
    return pl.pallas_call(
        functools.partial(_inproj_kernel, n_lat_tiles=n_lat_tiles),
        grid=(t // tm,),
        in_specs=_stream_specs(tm, n_lat_tiles, ctx_off) + [_mod_spec(0, tile_row), _mod_spec(1, tile_row),
                  full(norm_w), full(w_perm), full(lr), full(lb)],
        out_specs=[tok(w) for w in widths] + [tok(GLA_KEY), tok(GLA_KEY)],
        out_shape=[jax.ShapeDtypeStruct((t, w), F32 if i == 2 else BF16) for i, w in enumerate(widths)]
        + [jax.ShapeDtypeStruct((t, GLA_KEY), F32)] * 2,
        compiler_params=_params(1),
        name="in_projection",
    )(x_lat, x_ctx, mod3, mod3, norm_w, w_perm, lr, lb)


def _gla_kernel(qf_ref, kf_ref, vf_ref, gf_ref, qb_ref, kb_ref, vb_ref, gb_ref,
                of_ref, ob_ref, sf_ref, sb_ref):
    @pl.when(pl.program_id(1) == 0)
    def _():
        sf_ref[...] = jnp.zeros_like(sf_ref)
        sb_ref[...] = jnp.zeros_like(sb_ref)

    n = qf_ref.shape[0]
    nc = n // GLA_CHUNK
    assert nc * GLA_DK == n
    shift = GLA_CHUNK.bit_length() - 1
    row = lax.broadcasted_iota(jnp.int32, (n, n), 0)
    col = lax.broadcasted_iota(jnp.int32, (n, n), 1)
    same = (row >> shift) == (col >> shift)
    dirs = ((qf_ref, kf_ref, vf_ref, gf_ref, of_ref, sf_ref, False),
            (qb_ref, kb_ref, vb_ref, gb_ref, ob_ref, sb_ref, True))
    heads = range(GLA_HEADS)
    ks = [slice(h * GLA_DK, (h + 1) * GLA_DK) for h in heads]
    vs = [slice(h * GLA_DV, (h + 1) * GLA_DV) for h in heads]

    tris, bcs = [], []
    for _, _, _, g_ref, _, _, reverse in dirs:
        tri = same & ((col >= row) if reverse else (col <= row))
        g = g_ref[...]
        g_hi = g.astype(BF16)
        g_lo = (g - g_hi.astype(F32)).astype(BF16)
        tri_b = tri.astype(BF16)
        tris.append(tri)
        bcs.append(_dot(tri_b, g_hi) + _dot(tri_b, g_lo))

    qds, kds, kls, decs, vals = [], [], [], [], []
    for (q_ref, k_ref, v_ref, _, _, _, reverse), bc in zip(dirs, bcs):
        last = 0 if reverse else GLA_CHUNK - 1
        bl3 = bc.reshape(nc, GLA_CHUNK, GLA_KEY)[:, last:last + 1, :]
        bl = jnp.broadcast_to(bl3, (nc, GLA_CHUNK, GLA_KEY)).reshape(n, GLA_KEY)
        q = q_ref[...].astype(F32)
        k = k_ref[...].astype(F32)
        qds.append((q * jnp.exp(bc)).astype(BF16))
        kds.append((k * jnp.exp(-bc)).astype(BF16))
        kls.append((k * jnp.exp(bl - bc)).astype(BF16))
        decs.append(jnp.exp(bl3))
        vals.append(v_ref[...])

    zero = jnp.zeros((), BF16)
    scores = [[_dot_nt(qds[d][:, ks[h]], kds[d][:, ks[h]]) for h in heads] for d in range(2)]
    incs = [[_dot_tn(vals[d][:, vs[h]], jnp.where(same, jnp.concatenate([kls[d][:, ks[h]]] * nc, axis=1), zero))
             for h in heads] for d in range(2)]
    intra = [[_dot(jnp.where(tris[d], scores[d][h], 0.0).astype(BF16), vals[d][:, vs[h]]) for h in heads]
             for d in range(2)]

    st_alls = []
    for d, (_, _, _, _, _, s_ref, reverse) in enumerate(dirs):
        order = list(range(nc))[::-1] if reverse else list(range(nc))
        st_alls.append([])
        for h in heads:
            st = s_ref[h]
            entering = {}
            for c in order:
                entering[c] = st
                st = st * decs[d][c][:, ks[h]] + incs[d][h][:, c * GLA_DK:(c + 1) * GLA_DK]
            s_ref[h] = st
            st_alls[d].append(jnp.concatenate([entering[c] for c in range(nc)], axis=1).astype(BF16))

    inter = [[_dot_nt(jnp.where(same, jnp.concatenate([qds[d][:, ks[h]]] * nc, axis=1), zero), st_alls[d][h])
              for h in heads] for d in range(2)]
    for d, (_, _, _, _, o_ref, _, _) in enumerate(dirs):
        o_ref[...] = jnp.concatenate([intra[d][h] + inter[d][h] for h in heads], axis=1)


def _gla(q, k, v, gf, gb, batch, n_lat_blocks, blk):
    t = q.shape[0]
    ctx0 = batch * n_lat_blocks

    def fwd(b, i):
        return (jnp.where(i == 0, ctx0 + b, b * n_lat_blocks + i - 1), 0)

    def bwd(b, i):
        return (jnp.where(i == 0, ctx0 + b, b * n_lat_blocks + n_lat_blocks - i), 0)

    spec = lambda w, m: pl.BlockSpec((blk, w), m)
    state = pltpu.VMEM((GLA_HEADS, GLA_DV, GLA_DK), F32)
    return pl.pallas_call(
        _gla_kernel,
        grid=(batch, n_lat_blocks + 1),
        in_specs=[spec(GLA_KEY, fwd), spec(GLA_KEY, fwd), spec(GLA_VAL, fwd), spec(GLA_KEY, fwd),
                  spec(GLA_KEY, bwd), spec(GLA_KEY, bwd), spec(GLA_VAL, bwd), spec(GLA_KEY, bwd)],
        out_specs=[spec(GLA_VAL, fwd), spec(GLA_VAL, bwd)],
        out_shape=[jax.ShapeDtypeStruct((t, GLA_VAL), F32)] * 2,
        scratch_shapes=[state, state],
        compiler_params=_params(2),
        name="gla_bidir",
    )(q, k, v, gf, q, k, v, gb)


def _s5_group_tables(arow, acol, log_dt, bt, ct, reverse):
    delta = jnp.exp(log_dt)
    lane_s = lax.broadcasted_iota(jnp.int32, (1, S5_SW), 1)
    sign = jnp.where(lane_s < S5_STATE, -1.0, 1.0)
    ar, ai = arow[0:1, :], arow[1:2, :]
    zr, zi = ar * delta, ai * delta

    def row_power(n):
        mag = jnp.exp(zr * n)
        return mag * jnp.cos(zi * n), mag * jnp.sin(zi * n)

    one = jnp.ones((1, 1), F32)
    l_r, l_i = row_power(one)
    den = ar * ar + ai * ai
    xr, xi = l_r - 1.0, l_i
    cr = (xr * ar + xi * ai) / den
    ci = (xi * ar - xr * ai) / den
    bdup = cr * bt + ci * (pltpu.roll(bt, S5_STATE, 1) * sign)
    bswp = pltpu.roll(bdup, S5_STATE, 1) * sign

    step = lax.broadcasted_iota(jnp.int32, (S5_CHUNK, 1), 0).astype(F32)
    er, ei = row_power(step if reverse else (S5_CHUNK - 1.0 - step))
    to_state = jnp.concatenate(
        [er[s:s + 1, :] * bdup + ei[s:s + 1, :] * bswp for s in range(S5_CHUNK)], axis=0)
    bdup_sw, bswp_sw = pltpu.roll(bdup, S5_STATE, 1), pltpu.roll(bswp, S5_STATE, 1)
    to_state_sw = jnp.concatenate(
        [er[s:s + 1, :] * bdup_sw + ei[s:s + 1, :] * bswp_sw for s in range(S5_CHUNK)], axis=0)

    zrc, zic = acol[:, 0:1] * delta, acol[:, 1:2] * delta
    lane_c = lax.broadcasted_iota(jnp.int32, (1, S5_CW), 1)
    src_row = S5_CHUNK - 1 - lax.broadcasted_iota(jnp.int32, (S5_CHUNK, 1), 0)
    spread = ((lane_c >> S5_GROUP_SHIFT) == src_row).astype(BF16)

    def to_columns(a):
        hi = a.astype(BF16)
        lo = (a - hi.astype(F32)).astype(BF16)
        return (_dot_tn(hi, spread) + _dot_tn(lo, spread))[:S5_STATE]

    pr, pi = to_columns(er), to_columns(ei)
    ctr, cti = ct[:S5_STATE, :], ct[S5_STATE:, :]
    kr, ki = ctr * pr - cti * pi, ctr * pi + cti * pr
    lam_mag = jnp.exp(zrc)
    lcr, lci = lam_mag * jnp.cos(zic), lam_mag * jnp.sin(zic)
    from_state = jnp.concatenate([kr * lcr - ki * lci, -(kr * lci + ki * lcr)], axis=0)
    krow = jnp.dot(bdup, jnp.concatenate([kr, -ki], axis=0), precision=HIGHEST,
                   preferred_element_type=F32)
    blocks = []
    for s in range(S5_CHUNK):
        if reverse:
            sh = S5_GROUP * (S5_CHUNK - 1 - s)
            blk = pltpu.roll(krow, (S5_CW - sh) % S5_CW, 1) if sh else krow
            blocks.append(jnp.where(lane_c < S5_GROUP * (s + 1), blk, 0.0))
        else:
            sh = S5_GROUP * s
            blk = pltpu.roll(krow, sh, 1) if sh else krow
            blocks.append(jnp.where(lane_c >= sh, blk, 0.0))
    toeplitz = jnp.concatenate(blocks, axis=0)

    wr, wi = row_power(one * float(S5_CHUNK))
    decay = []
    for _ in range(8):
        decay.append((wr, wi * sign))
        wr, wi = wr * wr - wi * wi, 2.0 * wr * wi
    to_state = jnp.concatenate([to_state, to_state_sw], axis=1)
    return toeplitz.astype(BF16), to_state.astype(BF16), from_state.astype(BF16), decay


def _cmul_pair(wa, wb, z):
    nb = z.shape[0] // 2
    partner = jnp.concatenate([z[nb:], z[:nb]], axis=0)
    wbs = jnp.concatenate([jnp.broadcast_to(wb, (nb, 1, S5_SW)), jnp.broadcast_to(-wb, (nb, 1, S5_SW))], axis=0)
    return wa * z + wbs * partner


def _s5_segment_scan(x, h0, decay, lvl0, scr, reverse):
    n = x.shape[1]
    lo = S5_PAD
    first, last = (n - 1, 0) if reverse else (0, n - 1)
    scr[...] = jnp.zeros_like(scr)
    scr[:, lo:lo + n, :] = x
    if h0 is not None:
        scr[:, lo + first:lo + first + 1, :] = x[:, first:first + 1, :] + _cmul_pair(*decay[lvl0], h0)
    off, lvl = 1, lvl0
    while off < n:
        src = lo + off if reverse else lo - off
        scr[:, lo:lo + n, :] = scr[:, lo:lo + n, :] + _cmul_pair(*decay[lvl], scr[:, src:src + n, :])
        off, lvl = off * 2, lvl + 1
    h_out = scr[:, lo + last:lo + last + 1, :]
    edge = lo + n if reverse else lo - 1
    if h0 is not None:
        scr[:, edge:edge + 1, :] = h0
    start = lo + 1 if reverse else lo - 1
    return scr[:, start:start + n, :], h_out


def _s5_direction(hloc, hloc_sw, decay, scr, reverse, nb, n_rq, n_cc):
    n_lat_rows = nb * n_rq * GRID_W
    pair = lambda f: jnp.concatenate([f(hloc), f(hloc_sw)], axis=0)
    lat = pair(lambda a: a[:n_lat_rows].reshape(nb, n_rq, GRID_W, S5_SW))
    ctx = pair(lambda a: a[n_lat_rows:].reshape(nb, n_cc, S5_SW))
    h_ctx_rows, h_ctx = _s5_segment_scan(ctx, None, decay, 0, scr, reverse)
    step = lambda h: _cmul_pair(*decay[0], h)
    order = list(range(n_rq))[::-1] if reverse else list(range(n_rq))
    col = lat[:, order[0]]
    for rq in order[1:]:
        col = step(col) + lat[:, rq]
    lvl_col = int(math.log2(n_rq))
    h_col, _ = _s5_segment_scan(col, h_ctx, decay, lvl_col, scr, reverse)
    starts = {order[0]: h_col}
    for prev, rq in zip(order[:-1], order[1:]):
        starts[rq] = step(starts[prev]) + lat[:, prev]
    lat_start = jnp.stack([starts[rq][:nb] for rq in range(n_rq)], axis=1)
    return jnp.concatenate([lat_start.reshape(n_lat_rows, S5_SW), h_ctx_rows[:nb].reshape(nb * n_cc, S5_SW)], axis=0)


def _s5_kernel(u_ref, arf_ref, arb_ref, acf_ref, acb_ref, ldf_ref, ldb_ref, bt_ref, ct_ref,
               y_ref, xrot_ref, zacc_ref, scr, *, nb, n_lat, n_ctx):
    j = pl.program_id(0) % S5_TILE_GROUPS
    n_rq = n_lat // (GRID_W * S5_CHUNK)
    n_cc = n_ctx // S5_CHUNK
    lat_rows = n_rq * GRID_W
    n_rows = nb * (lat_rows + n_cc)
    lanes = S5_TILE_GROUPS * S5_GROUP
    slot = lax.broadcasted_iota(jnp.int32, (1, lanes), 1) >> S5_GROUP_SHIFT
    wrap = S5_TILE_GROUPS - 1

    runs = []
    for b in range(nb):
        for rq in range(n_rq):
            runs.append((lambda t, b=b, rq=rq: pl.ds(b * n_lat + (rq * S5_CHUNK + t) * GRID_W, GRID_W),
                         b * lat_rows + rq * GRID_W, GRID_W))
    for b in range(nb):
        runs.append((lambda t, b=b: pl.ds(nb * n_lat + b * n_ctx + t, n_cc, stride=S5_CHUNK),
                     nb * lat_rows + b * n_cc, n_cc))

    @pl.when(j == 0)
    def _():
        for tok_rows, row0, n in runs:
            for t in range(S5_CHUNK):
                x = pltpu.roll(u_ref[tok_rows(t), :], (t & wrap) * S5_GROUP, 1)
                xrot_ref[t, pl.ds(row0, n), :] = x.astype(BF16)

    halves = []
    for th in range(2):
        acc = xrot_ref[th * S5_TILE_GROUPS]
        for tp in range(1, S5_TILE_GROUPS):
            acc = jnp.where(slot == ((tp + j) & wrap), xrot_ref[th * S5_TILE_GROUPS + tp], acc)
        halves.append(pltpu.roll(acc.astype(F32), ((-j) & wrap) * S5_GROUP, 1))
    u = jnp.concatenate(halves, axis=1).astype(BF16)

    tabs = [_s5_group_tables(ar_ref[0], ac_ref[0], ld_ref[0], bt_ref[0], ct_ref[0], reverse)
            for reverse, ar_ref, ac_ref, ld_ref in ((False, arf_ref, acf_ref, ldf_ref), (True, arb_ref, acb_ref, ldb_ref))]
    hlocs = [_dot(u, tab[1]) for tab in tabs]
    y = _dot(u, tabs[0][0]) + _dot(u, tabs[1][0])
    hstarts = [_s5_direction(hlocs[d][:, :S5_SW], hlocs[d][:, S5_SW:], tabs[d][3], scr, d == 1, nb, n_rq, n_cc
                             ).astype(BF16) for d in range(2)]
    y = y + _dot(hstarts[0], tabs[0][2]) + _dot(hstarts[1], tabs[1][2])

    for th in range(2):
        rolled = pltpu.roll(y[:, th * lanes:(th + 1) * lanes], j * S5_GROUP, 1)
        for tp in range(S5_TILE_GROUPS):
            base = (th * S5_TILE_GROUPS + tp) * n_rows
            for r0 in range(0, n_rows, lanes):
                nr = min(lanes, n_rows - r0)
                own = jnp.broadcast_to(slot == ((tp + j) & wrap), (nr, lanes))
                pltpu.store(zacc_ref.at[pl.ds(base + r0, nr), :], rolled[r0:r0 + nr], mask=own)

    @pl.when(j == S5_TILE_GROUPS - 1)
    def _():
        for tok_rows, row0, n in runs:
            for t in range(S5_CHUNK):
                z = zacc_ref[pl.ds(t * n_rows + row0, n), :]
                y_ref[tok_rows(t), :] = pltpu.roll(z, ((-t) & wrap) * S5_GROUP, 1)


def _s5(u, params_f, params_b, bt, ct, nb, n_lat, n_ctx):
    t, width = u.shape
    lanes = S5_TILE_GROUPS * S5_GROUP
    n_rows = nb * (n_lat + n_ctx) // S5_CHUNK
    tile = pl.BlockSpec((t, lanes), lambda g: (0, g // S5_TILE_GROUPS))
    grp = lambda a: pl.BlockSpec((1,) + a.shape[1:], lambda g: (g, 0, 0))
    arf, acf, ldf = params_f
    arb, acb, ldb = params_b
    ops = (arf, arb, acf, acb, ldf, ldb, bt, ct)
    return pl.pallas_call(
        functools.partial(_s5_kernel, nb=nb, n_lat=n_lat, n_ctx=n_ctx),
        grid=(S5_GROUPS,),
        in_specs=[tile] + [grp(a) for a in ops],
        out_specs=tile,
        out_shape=jax.ShapeDtypeStruct((t, width), F32),
        scratch_shapes=[pltpu.VMEM((S5_CHUNK, n_rows, lanes), BF16),
                        pltpu.VMEM((S5_CHUNK * n_rows, lanes), F32),
                        pltpu.VMEM((2 * nb, 2 * S5_PAD + GRID_W, S5_SW), F32)],
        compiler_params=_params(1),
        name="s5_bidir",
    )(u, *ops)


def _s5_param_layout(a_re, a_im, log_dt):
    arow = jnp.stack([jnp.concatenate([a_re, a_re], axis=-1), jnp.concatenate([a_im, a_im], axis=-1)], axis=1)
    acol = jnp.stack([a_re, a_im], axis=-1)
    return arow, acol, log_dt[:, None, None]


def _s5_bc_layout(b_re, b_im, c_re, c_im):
    bt = jnp.concatenate([b_re.transpose(0, 2, 1), b_im.transpose(0, 2, 1)], axis=-1)
    ct = jnp.concatenate([c_re.transpose(0, 2, 1), c_im.transpose(0, 2, 1)], axis=1)
    return bt, jnp.tile(ct, (1, 1, S5_CHUNK))


def _gelu_tanh(x):
    return 0.5 * x * (1.0 + jnp.tanh(math.sqrt(2.0 / math.pi) * (x + 0.044715 * (x * x * x))))


def _merge_kernel(of_ref, ob_ref, y_ref, u_ref, r_ref, ga_ref, gm_ref, xl_ref, xc_ref, g1_ref,
                  gnw_ref, d_ref, wgp_ref, wglu_ref, bglu_ref, wsp_ref, wout_ref, o_ref, *, n_lat_tiles):
    o = of_ref[...] + ob_ref[...]
    heads = []
    for h in range(GLA_HEADS):
        oh = o[:, h * GLA_DV:(h + 1) * GLA_DV]
        heads.append(oh * lax.rsqrt(jnp.mean(oh * oh, axis=-1, keepdims=True) + EPS) * gnw_ref[...])
    og = jnp.concatenate(heads, axis=1) * r_ref[...].astype(F32)
    ya = _dot(og.astype(BF16), wgp_ref[...])
    s = _gelu_tanh(y_ref[...] + d_ref[...] * u_ref[...])
    s = s * jax.nn.sigmoid(_dot(s.astype(BF16), wglu_ref[...]) + bglu_ref[...])
    yb = _dot(s.astype(BF16), wsp_ref[...])
    m = ga_ref[...].astype(F32) * ya + gm_ref[...].astype(F32) * yb
    x = _stream_tile(xl_ref, xc_ref, n_lat_tiles)
    o_ref[...] = x + g1_ref[...] * _dot(m.astype(BF16), wout_ref[...])


def _merge(of, ob, y, u, r, ga, gm, stream, mod3, tile_row, tm, n_tiles, weights):
    x_lat, x_ctx, n_lat_tiles, ctx_off = stream
    d = D_MODEL
    tok = lambda w: pl.BlockSpec((tm, w), lambda i: (i, 0))
    full = lambda a: pl.BlockSpec(a.shape, lambda i: (0,) * a.ndim)
    return pl.pallas_call(
        functools.partial(_merge_kernel, n_lat_tiles=n_lat_tiles),
        grid=(n_tiles,),
        in_specs=[tok(GLA_VAL), tok(GLA_VAL), tok(S5_WIDTH), tok(S5_WIDTH), tok(GLA_VAL), tok(d), tok(d)]
        + _stream_specs(tm, n_lat_tiles, ctx_off) + [_mod_spec(2, tile_row)] + [full(a) for a in weights],
        out_specs=tok(d),
        out_shape=jax.ShapeDtypeStruct((n_tiles * tm, d), F32),
        compiler_params=_params(1),
        name="merge",
    )(of, ob, y, u, r, ga, gm, x_lat, x_ctx, mod3, *weights)


def _swiglu_hidden(h, wg, wu, width, scale=None):
    parts = []
    for c0 in range(0, width, MXU_DIM):
        cols = slice(c0, c0 + MXU_DIM)
        a = _silu(_dot(h, wg[:, cols])) * _dot(h, wu[:, cols])
        parts.append((a if scale is None else a * scale).astype(BF16))
    return jnp.concatenate(parts, axis=1)


def _ffn_kernel(x_ref, sh_ref, sc_ref, g2_ref, nw_ref, wg_ref, wu_ref, wd_ref, o_ref):
    x = x_ref[...]
    h = _norm_mod(x, nw_ref[...], sh_ref[...], sc_ref[...]).astype(BF16)
    a = _swiglu_hidden(h, wg_ref, wu_ref, wg_ref.shape[1])
    o_ref[...] = x + g2_ref[...] * _dot(a, wd_ref[...])


def _ffn(x_all, mod3, tile_row, tm, norm_w, wg, wu, wd):
    t, d = x_all.shape
    full = lambda a: pl.BlockSpec(a.shape, lambda i: (0,) * a.ndim)
    return pl.pallas_call(
        _ffn_kernel,
        grid=(t // tm,),
        in_specs=[pl.BlockSpec((tm, d), lambda i: (i, 0)),
                  _mod_spec(3, tile_row), _mod_spec(4, tile_row), _mod_spec(5, tile_row),
                  full(norm_w), full(wg), full(wu), full(wd)],
        out_specs=pl.BlockSpec((tm, d), lambda i: (i, 0)),
        out_shape=jax.ShapeDtypeStruct((t, d), F32),
        compiler_params=_params(1),
        name="ffn_dense",
    )(x_all, mod3, mod3, mod3, norm_w, wg, wu, wd)


def _moe_kernel(x_ref, sh_ref, sc_ref, g2_ref, nw_ref, fw_ref, wr_ref, wg_ref, wu_ref, wd_ref,
                o_ref, h_ref, gate_ref, acc_ref):
    e = pl.program_id(1)
    f = pl.program_id(2)
    lane = lax.broadcasted_iota(jnp.int32, gate_ref.shape, 1).astype(F32)

    @pl.when((e == 0) & (f == 0))
    def _():
        hf = _norm_mod(x_ref[...], nw_ref[...], sh_ref[...], sc_ref[...])
        hb = hf.astype(BF16)
        h_ref[...] = hb
        acc_ref[...] = jnp.zeros_like(acc_ref)
        h_lo = (hf - hb.astype(F32)).astype(BF16)
        wr = wr_ref[...]
        w_hi = wr.astype(BF16)
        w_lo = (wr - w_hi.astype(F32)).astype(BF16)
        logits = _dot(hb, w_hi) + _dot(h_lo, w_hi) + _dot(hb, w_lo)
        logits = jnp.where(lane < N_EXPERTS, logits, -jnp.inf)
        m1 = jnp.max(logits, axis=-1, keepdims=True)
        i1 = jnp.min(jnp.where(logits == m1, lane, float(Z_PAD)), axis=-1, keepdims=True)
        rest = jnp.where(lane == i1, -jnp.inf, logits)
        m2 = jnp.max(rest, axis=-1, keepdims=True)
        i2 = jnp.min(jnp.where(rest == m2, lane, float(Z_PAD)), axis=-1, keepdims=True)
        e2 = jnp.exp(m2 - m1)
        w1 = 1.0 / (1.0 + e2)
        gate_ref[...] = jnp.where(lane == i1, w1, 0.0) + jnp.where(lane == i2, e2 * w1, 0.0)

    h = h_ref[...]
    ge = jnp.sum(jnp.where(lane == e.astype(F32), gate_ref[...], 0.0), axis=-1, keepdims=True)
    a = _swiglu_hidden(h, wg_ref.at[0], wu_ref.at[0], wg_ref.shape[2], ge)
    acc_ref[...] += _dot(a, wd_ref[0])

    @pl.when((e == pl.num_programs(1) - 1) & (f == pl.num_programs(2) - 1))
    def _():
        y = x_ref[...] + g2_ref[...] * acc_ref[...]
        o_ref[...] = y * lax.rsqrt(jnp.mean(y * y, axis=-1, keepdims=True) + EPS) * fw_ref[...]


def _moe(x_lat, mod3, tile_row, tm, norm_w, final_w, w_router, wg, wu, wd, tf):
    t, d = x_lat.shape
    n_e, _, f_dim = wg.shape
    mspec = lambda col: pl.BlockSpec((None, 1, D_MODEL), lambda i, e, j: (tile_row(i), 0, col))
    const = lambda a: pl.BlockSpec(a.shape, lambda i, e, j: (0,) * a.ndim)
    return pl.pallas_call(
        _moe_kernel,
        grid=(t // tm, n_e, f_dim // tf),
        in_specs=[pl.BlockSpec((tm, d), lambda i, e, j: (i, 0)),
                  mspec(3), mspec(4), mspec(5), const(norm_w), const(final_w), const(w_router),
                  pl.BlockSpec((1, d, tf), lambda i, e, j: (e, 0, j)),
                  pl.BlockSpec((1, d, tf), lambda i, e, j: (e, 0, j)),
                  pl.BlockSpec((1, tf, d), lambda i, e, j: (e, j, 0))],
        out_specs=pl.BlockSpec((tm, d), lambda i, e, j: (i, 0)),
        out_shape=jax.ShapeDtypeStruct((t, d), F32),
        scratch_shapes=[pltpu.VMEM((tm, d), BF16), pltpu.VMEM((tm, Z_PAD), F32), pltpu.VMEM((tm, d), F32)],
        compiler_params=_params(3),
        name="moe",
    )(x_lat, mod3, mod3, mod3, norm_w, final_w, w_router, wg, wu, wd)


def _permute_w_in(w):
    o = 0
    parts = {}
    for name, width in (("k", GLA_KEY), ("v", GLA_VAL), ("zf", GLA_RANK), ("zb", GLA_RANK), ("u", S5_WIDTH),
                        ("q", GLA_KEY), ("r", GLA_VAL), ("ga", D_MODEL), ("gm", D_MODEL)):
        parts[name] = w[:, o:o + width]
        o += width
    pad = jnp.zeros((w.shape[0], Z_PAD - 2 * GLA_RANK), w.dtype)
    cols = [parts[n] for n in ("k", "v", "u", "q", "r", "ga", "gm", "zf", "zb")] + [pad]
    return jnp.concatenate(cols, axis=1).astype(BF16)


def _low_rank_gate(lr_f, lr_b, bias_f, bias_b):
    lr = jnp.zeros((Z_PAD, 2 * GLA_KEY), F32)
    lr = lr.at[:GLA_RANK, :GLA_KEY].set(lr_f).at[GLA_RANK:2 * GLA_RANK, GLA_KEY:].set(lr_b)
    return lr.astype(BF16), jnp.concatenate([bias_f, bias_b])[None, :]


def kernel(x, c, ctx, c_ctx, w_mod, b_mod, norm1_w, norm2_w, final_norm_w, w_in, gla_lr_f, gla_lr_b, gla_bias_f, gla_bias_b, gla_norm_w, s5_a_re_f, s5_a_im_f, s5_log_dt_f, s5_a_re_b, s5_a_im_b, s5_log_dt_b, s5_b_re, s5_b_im, s5_c_re, s5_c_im, s5_d, s5_w_glu, s5_b_glu, w_gla_proj, w_s5_proj, w_out, ffn_w_gate, ffn_w_up, ffn_w_down, moe_router, moe_w_gate, moe_w_up, moe_w_down):
    batch, n_lat, d = x.shape
    n_ctx = ctx.shape[1]
    depth = w_mod.shape[0]
    assert d == D_MODEL and batch < 8 and depth == 2
    assert n_lat % n_ctx == 0 and n_ctx % GLA_CHUNK == 0 and (n_lat // GRID_W) % S5_CHUNK == 0
    t_lat, t_ctx = batch * n_lat, batch * n_ctx
    t_all = t_lat + t_ctx

    tm = math.gcd(512, t_ctx)
    assert n_lat % tm == 0
    lat_tiles_per_batch = n_lat // tm
    n_lat_tiles = t_lat // tm
    tile_row = lambda i: jnp.where(i < n_lat_tiles, i // lat_tiles_per_batch, batch)

    cond = jnp.zeros((8, d), F32).at[:batch].set(c).at[batch].set(c_ctx)
    mod = _modulation(cond, w_mod, b_mod)

    stream = (x.reshape(t_lat, d), ctx.reshape(t_ctx, d), n_lat_tiles, 0)
    out = None
    for l in range(depth):
        last = l == depth - 1
        mod3 = mod[l].reshape(8, 1, 6 * d)
        lr, lb = _low_rank_gate(gla_lr_f[l], gla_lr_b[l], gla_bias_f[l], gla_bias_b[l])
        k, v, u, q, r, ga, gm, gf, gb = _in_projection(
            stream, t_all, mod3, tile_row, tm, norm1_w[l][None, :], _permute_w_in(w_in[l]), lr, lb)

        of, ob = _gla(q, k, v, gf, gb, batch, n_lat // n_ctx, n_ctx)

        bt, ct = _s5_bc_layout(s5_b_re[l], s5_b_im[l], s5_c_re[l], s5_c_im[l])
        y = _s5(u, _s5_param_layout(s5_a_re_f[l], s5_a_im_f[l], s5_log_dt_f[l]),
                _s5_param_layout(s5_a_re_b[l], s5_a_im_b[l], s5_log_dt_b[l]), bt, ct, batch, n_lat, n_ctx)

        merge_w = (gla_norm_w[l][None, :], s5_d[l][None, :], w_gla_proj[l].astype(BF16),
                   s5_w_glu[l].astype(BF16), s5_b_glu[l][None, :], w_s5_proj[l].astype(BF16),
                   w_out[l].astype(BF16))
        n_tiles = n_lat_tiles if last else t_all // tm
        x_mid = _merge(of, ob, y, u, r, ga, gm, stream, mod3, tile_row, tm, n_tiles, merge_w)

        if l % 2 == 0:
            i = l // 2
            x_all = _ffn(x_mid, mod3, tile_row, tm, norm2_w[l][None, :], ffn_w_gate[i].astype(BF16),
                         ffn_w_up[i].astype(BF16), ffn_w_down[i].astype(BF16))
            stream = (x_all, x_all, n_lat_tiles, n_lat_tiles)
        else:
            i = l // 2
            tm_f = math.gcd(1024, n_lat)
            row_f = lambda j, tm_f=tm_f: j // (n_lat // tm_f)
            w_router = jnp.zeros((d, Z_PAD), F32).at[:, :N_EXPERTS].set(moe_router[i])
            res = _moe(x_mid, mod3, row_f, tm_f, norm2_w[l][None, :], final_norm_w[None, :], w_router,
                       moe_w_gate[i].astype(BF16), moe_w_up[i].astype(BF16), moe_w_down[i].astype(BF16), MOE_F_TILE)
            out = res.reshape(batch, n_lat, d)
    return out
```

```python
import functools
import math

import jax
import jax.numpy as jnp
from jax import lax
from jax.experimental import pallas as pl
from jax.experimental.pallas import tpu as pltpu

F32 = jnp.float32
BF16 = jnp.bfloat16
HIGHEST = lax.Precision.HIGHEST

D_MODEL = 1024
GRID_W = 64
GLA_HEADS = 4
GLA_DK = 64
GLA_DV = 128
GLA_KEY = GLA_HEADS * GLA_DK
GLA_VAL = GLA_HEADS * GLA_DV
GLA_RANK = 16
GLA_GATE_NORM = 16.0
GLA_CHUNK = 64
S5_WIDTH = 512
S5_GROUP = 16
S5_GROUPS = S5_WIDTH // S5_GROUP
S5_STATE = 64
S5_CHUNK = 16
S5_CW = S5_CHUNK * S5_GROUP
S5_SW = 2 * S5_STATE
S5_PAD = GRID_W // 2
S5_TILE_GROUPS = 128 // S5_GROUP
S5_GROUP_SHIFT = 4
N_EXPERTS = 8
EPS = 1e-6
MXU_DIM = 256
MOE_F_TILE = 7 * MXU_DIM
SIDE_CAST_BLOCKS = 32
Z_PAD = 128

VMEM_LIMIT = 56 * 1024 * 1024

SEC_K = (0, GLA_KEY)
SEC_V = (SEC_K[1], SEC_K[1] + GLA_VAL)
SEC_U = (SEC_V[1], SEC_V[1] + S5_WIDTH)
SEC_Q = (SEC_U[1], SEC_U[1] + GLA_KEY)
SEC_R = (SEC_Q[1], SEC_Q[1] + GLA_VAL)
SEC_GA = (SEC_R[1], SEC_R[1] + D_MODEL)
SEC_GM = (SEC_GA[1], SEC_GA[1] + D_MODEL)
SEC_Z = (SEC_GM[1], SEC_GM[1] + Z_PAD)
IN_PERM_WIDTH = SEC_Z[1]


def _params(n_axes):
    return pltpu.CompilerParams(dimension_semantics=("arbitrary",) * n_axes,
                                vmem_limit_bytes=VMEM_LIMIT)


def _silu(x):
    return x * jax.nn.sigmoid(x)


def _norm_mod(x, nw, sh, sc):
    y = x * lax.rsqrt(jnp.mean(x * x, axis=-1, keepdims=True) + EPS) * nw
    return y * (1.0 + sc) + sh


def _dot(a, b):
    return jnp.dot(a, b, preferred_element_type=F32)


def _dot_nt(a, b):
    return lax.dot_general(a, b, (((1,), (1,)), ((), ())), preferred_element_type=F32)


def _dot_tn(a, b):
    return lax.dot_general(a, b, (((0,), (0,)), ((), ())), preferred_element_type=F32)


def _mod_kernel(c_ref, w_ref, b_ref, o_ref):
    s = _silu(c_ref[...]).astype(BF16)
    o_ref[0] = _dot(s, w_ref[0].astype(BF16)) + b_ref[0]


def _modulation(cond, w_mod, b_mod):
    depth, d, n = w_mod.shape
    tn = 1536
    return pl.pallas_call(
        _mod_kernel,
        grid=(depth, n // tn),
        in_specs=[pl.BlockSpec((8, d), lambda l, j: (0, 0)),
                  pl.BlockSpec((1, d, tn), lambda l, j: (l, 0, j)),
                  pl.BlockSpec((1, 1, tn), lambda l, j: (l, 0, j))],
        out_specs=pl.BlockSpec((1, 8, tn), lambda l, j: (l, 0, j)),
        out_shape=jax.ShapeDtypeStruct((depth, 8, n), F32),
        compiler_params=_params(2),
        name="modulation",
    )(cond, w_mod, b_mod.reshape(depth, 1, n))


def _hosted_call(body, name, n_steps, in_specs, out_specs, out_shape, operands, side=None):
    if side is None:
        outs = pl.pallas_call(body, grid=(n_steps,), in_specs=in_specs, out_specs=out_specs,
                              out_shape=out_shape, compiler_params=_params(1), name=name)(*operands)
        return outs, None
    rows, cols = side.shape
    n_blk = 1 << (min(n_steps, SIDE_CAST_BLOCKS).bit_length() - 1)
    assert rows % (16 * n_blk) == 0
    spec = pl.BlockSpec((rows // n_blk, cols), lambda i: (jnp.minimum(i, n_blk - 1), 0))
    n_in = len(in_specs)

    def kernel(*refs):
        body(*refs[:n_in], *refs[n_in + 1:-1])
        refs[-1][...] = refs[n_in][...].astype(BF16)

    *outs, cast = pl.pallas_call(
        kernel, grid=(n_steps,), in_specs=list(in_specs) + [spec], out_specs=list(out_specs) + [spec],
        out_shape=list(out_shape) + [jax.ShapeDtypeStruct((rows, cols), BF16)],
        compiler_params=_params(1), name=name)(*operands, side)
    return outs, cast


def _stream_specs(tm, n_lat_tiles, ctx_off):
    lat = pl.BlockSpec((tm, D_MODEL), lambda i: (jnp.minimum(i, n_lat_tiles - 1), 0))
    ctx = pl.BlockSpec((tm, D_MODEL), lambda i: (jnp.maximum(i - n_lat_tiles, 0) + ctx_off, 0))
    return [lat, ctx]


def _stream_tile(lat_ref, ctx_ref, n_lat_tiles):
    return jnp.where(pl.program_id(0) < n_lat_tiles, lat_ref[...], ctx_ref[...])


def _inproj_kernel(xl_ref, xc_ref, sh_ref, sc_ref, nw_ref, w_ref, lr_ref, lb_ref,
                   k_ref, v_ref, u_ref, q_ref, r_ref, ga_ref, gm_ref, gf_ref, gb_ref, *, n_lat_tiles):
    x = _stream_tile(xl_ref, xc_ref, n_lat_tiles)
    h = _norm_mod(x, nw_ref[...], sh_ref[...], sc_ref[...]).astype(BF16)

    def sec(s):
        return _dot(h, w_ref[:, s[0]:s[1]])

    z = sec(SEC_Z).astype(BF16)
    pre = _dot(z, lr_ref[...]) + lb_ref[...]
    k_ref[...] = sec(SEC_K).astype(BF16)
    v_ref[...] = sec(SEC_V).astype(BF16)
    u_ref[...] = sec(SEC_U)
    q_ref[...] = (sec(SEC_Q) * GLA_DK ** -0.5).astype(BF16)
    r_ref[...] = _silu(sec(SEC_R)).astype(BF16)
    ga_ref[...] = jax.nn.sigmoid(sec(SEC_GA)).astype(BF16)
    gm_ref[...] = jax.nn.sigmoid(sec(SEC_GM)).astype(BF16)
    g = (jnp.minimum(pre, 0.0) - jnp.log(1.0 + jnp.exp(-jnp.abs(pre)))) * (1.0 / GLA_GATE_NORM)
    gf_ref[...] = g[:, :GLA_KEY]
    gb_ref[...] = g[:, GLA_KEY:]


def _mod_spec(col, tile_row):
    return pl.BlockSpec((None, 1, D_MODEL), lambda i: (tile_row(i), 0, col))


def _in_projection(stream, t, mod3, tile_row, tm, norm_w, w_perm, lr, lb, side=None):
    x_lat, x_ctx, n_lat_tiles, ctx_off = stream
    widths = (GLA_KEY, GLA_VAL, S5_WIDTH, GLA_KEY, GLA_VAL, D_MODEL, D_MODEL)
    tok = lambda w: pl.BlockSpec((tm, w), lambda i: (i, 0))
    full = lambda a: pl.BlockSpec(a.shape, lambda i: (0,) * a.ndim)
    return _hosted_call(
        functools.partial(_inproj_kernel, n_lat_tiles=n_lat_tiles), "in_projection", t // tm,
        _stream_specs(tm, n_lat_tiles, ctx_off) + [_mod_spec(0, tile_row), _mod_spec(1, tile_row),
                                                   full(norm_w), full(w_perm), full(lr), full(lb)],
        [tok(w) for w in widths] + [tok(GLA_KEY), tok(GLA_KEY)],
        [jax.ShapeDtypeStruct((t, w), F32 if i == 2 else BF16) for i, w in enumerate(widths)]
        + [jax.ShapeDtypeStruct((t, GLA_KEY), F32)] * 2,
        (x_lat, x_ctx, mod3, mod3, norm_w, w_perm, lr, lb), side)


def _gla_kernel(qf_ref, kf_ref, vf_ref, gf_ref, qb_ref, kb_ref, vb_ref, gb_ref,
                of_ref, ob_ref, sf_ref, sb_ref):
    @pl.when(pl.program_id(1) == 0)
    def _():
        sf_ref[...] = jnp.zeros_like(sf_ref)
        sb_ref[...] = jnp.zeros_like(sb_ref)

    n = qf_ref.shape[0]
    nc = n // GLA_CHUNK
    assert nc * GLA_DK == n
    shift = GLA_CHUNK.bit_length() - 1
    row = lax.broadcasted_iota(jnp.int32, (n, n), 0)
    col = lax.broadcasted_iota(jnp.int32, (n, n), 1)
    same = (row >> shift) == (col >> shift)
    dirs = ((qf_ref, kf_ref, vf_ref, gf_ref, of_ref, sf_ref, False),
            (qb_ref, kb_ref, vb_ref, gb_ref, ob_ref, sb_ref, True))
    heads = range(GLA_HEADS)
    ks = [slice(h * GLA_DK, (h + 1) * GLA_DK) for h in heads]
    vs = [slice(h * GLA_DV, (h + 1) * GLA_DV) for h in heads]

    tris, bcs = [], []
    for _, _, _, g_ref, _, _, reverse in dirs:
        tri = same & ((col >= row) if reverse else (col <= row))
        g = g_ref[...]
        g_hi = g.astype(BF16)
        g_lo = (g - g_hi.astype(F32)).astype(BF16)
        tri_b = tri.astype(BF16)
        tris.append(tri)
        bcs.append(_dot(tri_b, g_hi) + _dot(tri_b, g_lo))

    qds, kds, kls, decs, vals = [], [], [], [], []
    for (q_ref, k_ref, v_ref, _, _, _, reverse), bc in zip(dirs, bcs):
        last = 0 if reverse else GLA_CHUNK - 1
        bl3 = bc.reshape(nc, GLA_CHUNK, GLA_KEY)[:, last:last + 1, :]
        bl = jnp.broadcast_to(bl3, (nc, GLA_CHUNK, GLA_KEY)).reshape(n, GLA_KEY)
        q = q_ref[...].astype(F32)
        k = k_ref[...].astype(F32)
        qds.append((q * jnp.exp(bc)).astype(BF16))
        kds.append((k * jnp.exp(-bc)).astype(BF16))
        kls.append((k * jnp.exp(bl - bc)).astype(BF16))
        decs.append(jnp.exp(bl3))
        vals.append(v_ref[...])

    zero = jnp.zeros((), BF16)
    scores = [[_dot_nt(qds[d][:, ks[h]], kds[d][:, ks[h]]) for h in heads] for d in range(2)]
    incs = [[_dot_tn(vals[d][:, vs[h]], jnp.where(same, jnp.concatenate([kls[d][:, ks[h]]] * nc, axis=1), zero))
             for h in heads] for d in range(2)]
    intra = [[_dot(jnp.where(tris[d], scores[d][h], 0.0).astype(BF16), vals[d][:, vs[h]]) for h in heads]
             for d in range(2)]

    st_alls = []
    for d, (_, _, _, _, _, s_ref, reverse) in enumerate(dirs):
        order = list(range(nc))[::-1] if reverse else list(range(nc))
        st_alls.append([])
        for h in heads:
            st = s_ref[h]
            entering = {}
            for c in order:
                entering[c] = st
                st = st * decs[d][c][:, ks[h]] + incs[d][h][:, c * GLA_DK:(c + 1) * GLA_DK]
            s_ref[h] = st
            st_alls[d].append(jnp.concatenate([entering[c] for c in range(nc)], axis=1).astype(BF16))

    inter = [[_dot_nt(jnp.where(same, jnp.concatenate([qds[d][:, ks[h]]] * nc, axis=1), zero), st_alls[d][h])
              for h in heads] for d in range(2)]
    for d, (_, _, _, _, o_ref, _, _) in enumerate(dirs):
        o_ref[...] = jnp.concatenate([intra[d][h] + inter[d][h] for h in heads], axis=1)


def _gla(q, k, v, gf, gb, batch, n_lat_blocks, blk):
    t = q.shape[0]
    ctx0 = batch * n_lat_blocks

    def fwd(b, i):
        return (jnp.where(i == 0, ctx0 + b, b * n_lat_blocks + i - 1), 0)

    def bwd(b, i):
        return (jnp.where(i == 0, ctx0 + b, b * n_lat_blocks + n_lat_blocks - i), 0)

    spec = lambda w, m: pl.BlockSpec((blk, w), m)
    state = pltpu.VMEM((GLA_HEADS, GLA_DV, GLA_DK), F32)
    return pl.pallas_call(
        _gla_kernel,
        grid=(batch, n_lat_blocks + 1),
        in_specs=[spec(GLA_KEY, fwd), spec(GLA_KEY, fwd), spec(GLA_VAL, fwd), spec(GLA_KEY, fwd),
                  spec(GLA_KEY, bwd), spec(GLA_KEY, bwd), spec(GLA_VAL, bwd), spec(GLA_KEY, bwd)],
        out_specs=[spec(GLA_VAL, fwd), spec(GLA_VAL, bwd)],
        out_shape=[jax.ShapeDtypeStruct((t, GLA_VAL), F32)] * 2,
        scratch_shapes=[state, state],
        compiler_params=_params(2),
        name="gla_bidir",
    )(q, k, v, gf, q, k, v, gb)


def _s5_group_tables(arow, acol, log_dt, bt, ct, reverse):
    delta = jnp.exp(log_dt)
    lane_s = lax.broadcasted_iota(jnp.int32, (1, S5_SW), 1)
    sign = jnp.where(lane_s < S5_STATE, -1.0, 1.0)
    ar, ai = arow[0:1, :], arow[1:2, :]
    zr, zi = ar * delta, ai * delta

    def row_power(n):
        mag = jnp.exp(zr * n)
        return mag * jnp.cos(zi * n), mag * jnp.sin(zi * n)

    one = jnp.ones((1, 1), F32)
    l_r, l_i = row_power(one)
    den = ar * ar + ai * ai
    xr, xi = l_r - 1.0, l_i
    cr = (xr * ar + xi * ai) / den
    ci = (xi * ar - xr * ai) / den
    bdup = cr * bt + ci * (pltpu.roll(bt, S5_STATE, 1) * sign)
    bswp = pltpu.roll(bdup, S5_STATE, 1) * sign

    step = lax.broadcasted_iota(jnp.int32, (S5_CHUNK, 1), 0).astype(F32)
    er, ei = row_power(step if reverse else (S5_CHUNK - 1.0 - step))
    to_state = jnp.concatenate(
        [er[s:s + 1, :] * bdup + ei[s:s + 1, :] * bswp for s in range(S5_CHUNK)], axis=0)
    bdup_sw, bswp_sw = pltpu.roll(bdup, S5_STATE, 1), pltpu.roll(bswp, S5_STATE, 1)
    to_state_sw = jnp.concatenate(
        [er[s:s + 1, :] * bdup_sw + ei[s:s + 1, :] * bswp_sw for s in range(S5_CHUNK)], axis=0)

    zrc, zic = acol[:, 0:1] * delta, acol[:, 1:2] * delta
    lane_c = lax.broadcasted_iota(jnp.int32, (1, S5_CW), 1)
    src_row = S5_CHUNK - 1 - lax.broadcasted_iota(jnp.int32, (S5_CHUNK, 1), 0)
    spread = ((lane_c >> S5_GROUP_SHIFT) == src_row).astype(BF16)

    def to_columns(a):
        hi = a.astype(BF16)
        lo = (a - hi.astype(F32)).astype(BF16)
        return (_dot_tn(hi, spread) + _dot_tn(lo, spread))[:S5_STATE]

    pr, pi = to_columns(er), to_columns(ei)
    ctr, cti = ct[:S5_STATE, :], ct[S5_STATE:, :]
    kr, ki = ctr * pr - cti * pi, ctr * pi + cti * pr
    lam_mag = jnp.exp(zrc)
    lcr, lci = lam_mag * jnp.cos(zic), lam_mag * jnp.sin(zic)
    from_state = jnp.concatenate([kr * lcr - ki * lci, -(kr * lci + ki * lcr)], axis=0)
    krow = jnp.dot(bdup, jnp.concatenate([kr, -ki], axis=0), precision=HIGHEST,
                   preferred_element_type=F32)
    blocks = []
    for s in range(S5_CHUNK):
        if reverse:
            sh = S5_GROUP * (S5_CHUNK - 1 - s)
            blk = pltpu.roll(krow, (S5_CW - sh) % S5_CW, 1) if sh else krow
            blocks.append(jnp.where(lane_c < S5_GROUP * (s + 1), blk, 0.0))
        else:
            sh = S5_GROUP * s
            blk = pltpu.roll(krow, sh, 1) if sh else krow
            blocks.append(jnp.where(lane_c >= sh, blk, 0.0))
    toeplitz = jnp.concatenate(blocks, axis=0)

    wr, wi = row_power(one * float(S5_CHUNK))
    decay = []
    for _ in range(8):
        decay.append((wr, wi * sign))
        wr, wi = wr * wr - wi * wi, 2.0 * wr * wi
    to_state = jnp.concatenate([to_state, to_state_sw], axis=1)
    return toeplitz.astype(BF16), to_state.astype(BF16), from_state.astype(BF16), decay


def _cmul_pair(wa, wb, z):
    nb = z.shape[0] // 2
    partner = jnp.concatenate([z[nb:], z[:nb]], axis=0)
    wbs = jnp.concatenate([jnp.broadcast_to(wb, (nb, 1, S5_SW)), jnp.broadcast_to(-wb, (nb, 1, S5_SW))], axis=0)
    return wa * z + wbs * partner


def _s5_segment_scan(x, h0, decay, lvl0, scr, reverse):
    n = x.shape[1]
    lo = S5_PAD
    first, last = (n - 1, 0) if reverse else (0, n - 1)
    scr[...] = jnp.zeros_like(scr)
    scr[:, lo:lo + n, :] = x
    if h0 is not None:
        scr[:, lo + first:lo + first + 1, :] = x[:, first:first + 1, :] + _cmul_pair(*decay[lvl0], h0)
    off, lvl = 1, lvl0
    while off < n:
        src = lo + off if reverse else lo - off
        scr[:, lo:lo + n, :] = scr[:, lo:lo + n, :] + _cmul_pair(*decay[lvl], scr[:, src:src + n, :])
        off, lvl = off * 2, lvl + 1
    h_out = scr[:, lo + last:lo + last + 1, :]
    edge = lo + n if reverse else lo - 1
    if h0 is not None:
        scr[:, edge:edge + 1, :] = h0
    start = lo + 1 if reverse else lo - 1
    return scr[:, start:start + n, :], h_out


def _s5_direction(hloc, hloc_sw, decay, scr, reverse, nb, n_rq, n_cc):
    n_lat_rows = nb * n_rq * GRID_W
    pair = lambda f: jnp.concatenate([f(hloc), f(hloc_sw)], axis=0)
    lat = pair(lambda a: a[:n_lat_rows].reshape(nb, n_rq, GRID_W, S5_SW))
    ctx = pair(lambda a: a[n_lat_rows:].reshape(nb, n_cc, S5_SW))
    h_ctx_rows, h_ctx = _s5_segment_scan(ctx, None, decay, 0, scr, reverse)
    step = lambda h: _cmul_pair(*decay[0], h)
    order = list(range(n_rq))[::-1] if reverse else list(range(n_rq))
    col = lat[:, order[0]]
    for rq in order[1:]:
        col = step(col) + lat[:, rq]
    lvl_col = int(math.log2(n_rq))
    h_col, _ = _s5_segment_scan(col, h_ctx, decay, lvl_col, scr, reverse)
    starts = {order[0]: h_col}
    for prev, rq in zip(order[:-1], order[1:]):
        starts[rq] = step(starts[prev]) + lat[:, prev]
    lat_start = jnp.stack([starts[rq][:nb] for rq in range(n_rq)], axis=1)
    return jnp.concatenate([lat_start.reshape(n_lat_rows, S5_SW), h_ctx_rows[:nb].reshape(nb * n_cc, S5_SW)], axis=0)


def _s5_kernel(u_ref, arf_ref, arb_ref, acf_ref, acb_ref, ldf_ref, ldb_ref, bt_ref, ct_ref,
               y_ref, xrot_ref, zacc_ref, scr, *, nb, n_lat, n_ctx):
    j = pl.program_id(0) % S5_TILE_GROUPS
    n_rq = n_lat // (GRID_W * S5_CHUNK)
    n_cc = n_ctx // S5_CHUNK
    lat_rows = n_rq * GRID_W
    n_rows = nb * (lat_rows + n_cc)
    lanes = S5_TILE_GROUPS * S5_GROUP
    slot = lax.broadcasted_iota(jnp.int32, (1, lanes), 1) >> S5_GROUP_SHIFT
    wrap = S5_TILE_GROUPS - 1

    runs = []
    for b in range(nb):
        for rq in range(n_rq):
            runs.append((lambda t, b=b, rq=rq: pl.ds(b * n_lat + (rq * S5_CHUNK + t) * GRID_W, GRID_W),
                         b * lat_rows + rq * GRID_W, GRID_W))
    for b in range(nb):
        runs.append((lambda t, b=b: pl.ds(nb * n_lat + b * n_ctx + t, n_cc, stride=S5_CHUNK),
                     nb * lat_rows + b * n_cc, n_cc))

    @pl.when(j == 0)
    def _():
        for tok_rows, row0, n in runs:
            for t in range(S5_CHUNK):
                x = pltpu.roll(u_ref[tok_rows(t), :], (t & wrap) * S5_GROUP, 1)
                xrot_ref[t, pl.ds(row0, n), :] = x.astype(BF16)

    halves = []
    for th in range(2):
        acc = xrot_ref[th * S5_TILE_GROUPS]
        for tp in range(1, S5_TILE_GROUPS):
            acc = jnp.where(slot == ((tp + j) & wrap), xrot_ref[th * S5_TILE_GROUPS + tp], acc)
        halves.append(pltpu.roll(acc.astype(F32), ((-j) & wrap) * S5_GROUP, 1))
    u = jnp.concatenate(halves, axis=1).astype(BF16)

    tabs = [_s5_group_tables(ar_ref[0], ac_ref[0], ld_ref[0], bt_ref[0], ct_ref[0], reverse)
            for reverse, ar_ref, ac_ref, ld_ref in ((False, arf_ref, acf_ref, ldf_ref), (True, arb_ref, acb_ref, ldb_ref))]
    hlocs = [_dot(u, tab[1]) for tab in tabs]
    y = _dot(u, tabs[0][0]) + _dot(u, tabs[1][0])
    hstarts = [_s5_direction(hlocs[d][:, :S5_SW], hlocs[d][:, S5_SW:], tabs[d][3], scr, d == 1, nb, n_rq, n_cc
                             ).astype(BF16) for d in range(2)]
    y = y + _dot(hstarts[0], tabs[0][2]) + _dot(hstarts[1], tabs[1][2])

    for th in range(2):
        rolled = pltpu.roll(y[:, th * lanes:(th + 1) * lanes], j * S5_GROUP, 1)
        for tp in range(S5_TILE_GROUPS):
            base = (th * S5_TILE_GROUPS + tp) * n_rows
            for r0 in range(0, n_rows, lanes):
                nr = min(lanes, n_rows - r0)
                own = jnp.broadcast_to(slot == ((tp + j) & wrap), (nr, lanes))
                pltpu.store(zacc_ref.at[pl.ds(base + r0, nr), :], rolled[r0:r0 + nr], mask=own)

    @pl.when(j == S5_TILE_GROUPS - 1)
    def _():
        for tok_rows, row0, n in runs:
            for t in range(S5_CHUNK):
                z = zacc_ref[pl.ds(t * n_rows + row0, n), :]
                y_ref[tok_rows(t), :] = pltpu.roll(z, ((-t) & wrap) * S5_GROUP, 1)


def _s5(u, params_f, params_b, bt, ct, nb, n_lat, n_ctx):
    t, width = u.shape
    lanes = S5_TILE_GROUPS * S5_GROUP
    n_rows = nb * (n_lat + n_ctx) // S5_CHUNK
    tile = pl.BlockSpec((t, lanes), lambda g: (0, g // S5_TILE_GROUPS))
    grp = lambda a: pl.BlockSpec((1,) + a.shape[1:], lambda g: (g, 0, 0))
    arf, acf, ldf = params_f
    arb, acb, ldb = params_b
    ops = (arf, arb, acf, acb, ldf, ldb, bt, ct)
    return pl.pallas_call(
        functools.partial(_s5_kernel, nb=nb, n_lat=n_lat, n_ctx=n_ctx),
        grid=(S5_GROUPS,),
        in_specs=[tile] + [grp(a) for a in ops],
        out_specs=tile,
        out_shape=jax.ShapeDtypeStruct((t, width), F32),
        scratch_shapes=[pltpu.VMEM((S5_CHUNK, n_rows, lanes), BF16),
                        pltpu.VMEM((S5_CHUNK * n_rows, lanes), F32),
                        pltpu.VMEM((2 * nb, 2 * S5_PAD + GRID_W, S5_SW), F32)],
        compiler_params=_params(1),
        name="s5_bidir",
    )(u, *ops)


def _s5_param_layout(a_re, a_im, log_dt):
    arow = jnp.stack([jnp.concatenate([a_re, a_re], axis=-1), jnp.concatenate([a_im, a_im], axis=-1)], axis=1)
    acol = jnp.stack([a_re, a_im], axis=-1)
    return arow, acol, log_dt[:, None, None]


def _s5_bc_layout(b_re, b_im, c_re, c_im):
    bt = jnp.concatenate([b_re.transpose(0, 2, 1), b_im.transpose(0, 2, 1)], axis=-1)
    ct = jnp.concatenate([c_re.transpose(0, 2, 1), c_im.transpose(0, 2, 1)], axis=1)
    return bt, jnp.tile(ct, (1, 1, S5_CHUNK))


def _gelu_tanh(x):
    return 0.5 * x * (1.0 + jnp.tanh(math.sqrt(2.0 / math.pi) * (x + 0.044715 * (x * x * x))))


def _merge_kernel(of_ref, ob_ref, y_ref, u_ref, r_ref, ga_ref, gm_ref, xl_ref, xc_ref, g1_ref,
                  gnw_ref, d_ref, wgp_ref, wglu_ref, bglu_ref, wsp_ref, wout_ref, o_ref, *, n_lat_tiles):
    o = of_ref[...] + ob_ref[...]
    heads = []
    for h in range(GLA_HEADS):
        oh = o[:, h * GLA_DV:(h + 1) * GLA_DV]
        heads.append(oh * lax.rsqrt(jnp.mean(oh * oh, axis=-1, keepdims=True) + EPS) * gnw_ref[...])
    og = jnp.concatenate(heads, axis=1) * r_ref[...].astype(F32)
    ya = _dot(og.astype(BF16), wgp_ref[...])
    s = _gelu_tanh(y_ref[...] + d_ref[...] * u_ref[...])
    s = s * jax.nn.sigmoid(_dot(s.astype(BF16), wglu_ref[...]) + bglu_ref[...])
    yb = _dot(s.astype(BF16), wsp_ref[...])
    m = ga_ref[...].astype(F32) * ya + gm_ref[...].astype(F32) * yb
    x = _stream_tile(xl_ref, xc_ref, n_lat_tiles)
    o_ref[...] = x + g1_ref[...] * _dot(m.astype(BF16), wout_ref[...])


def _merge(of, ob, y, u, r, ga, gm, stream, mod3, tile_row, tm, n_tiles, weights, side=None):
    x_lat, x_ctx, n_lat_tiles, ctx_off = stream
    d = D_MODEL
    tok = lambda w: pl.BlockSpec((tm, w), lambda i: (i, 0))
    full = lambda a: pl.BlockSpec(a.shape, lambda i: (0,) * a.ndim)
    (x_mid,), cast = _hosted_call(
        functools.partial(_merge_kernel, n_lat_tiles=n_lat_tiles), "merge", n_tiles,
        [tok(GLA_VAL), tok(GLA_VAL), tok(S5_WIDTH), tok(S5_WIDTH), tok(GLA_VAL), tok(d), tok(d)]
        + _stream_specs(tm, n_lat_tiles, ctx_off) + [_mod_spec(2, tile_row)] + [full(a) for a in weights],
        [tok(d)], [jax.ShapeDtypeStruct((n_tiles * tm, d), F32)],
        (of, ob, y, u, r, ga, gm, x_lat, x_ctx, mod3, *weights), side)
    return x_mid, cast


def _swiglu_hidden(h, wg, wu, width, scale=None):
    parts = []
    for c0 in range(0, width, MXU_DIM):
        cols = slice(c0, c0 + MXU_DIM)
        a = _silu(_dot(h, wg[:, cols])) * _dot(h, wu[:, cols])
        parts.append((a if scale is None else a * scale).astype(BF16))
    return jnp.concatenate(parts, axis=1)


def _ffn_kernel(x_ref, sh_ref, sc_ref, g2_ref, nw_ref, wg_ref, wu_ref, wd_ref, o_ref):
    x = x_ref[...]
    h = _norm_mod(x, nw_ref[...], sh_ref[...], sc_ref[...]).astype(BF16)
    a = _swiglu_hidden(h, wg_ref, wu_ref, wg_ref.shape[1])
    o_ref[...] = x + g2_ref[...] * _dot(a, wd_ref[...])


def _ffn(x_all, mod3, tile_row, tm, norm_w, wg, wu, wd, side=None):
    t, d = x_all.shape
    full = lambda a: pl.BlockSpec(a.shape, lambda i: (0,) * a.ndim)
    once = lambda a: pl.BlockSpec(a.shape, lambda i: (0,) * a.ndim, pipeline_mode=pl.Buffered(1))
    (out,), cast = _hosted_call(
        _ffn_kernel, "ffn_dense", t // tm,
        [pl.BlockSpec((tm, d), lambda i: (i, 0)),
         _mod_spec(3, tile_row), _mod_spec(4, tile_row), _mod_spec(5, tile_row),
         full(norm_w), once(wg), once(wu), once(wd)],
        [pl.BlockSpec((tm, d), lambda i: (i, 0))], [jax.ShapeDtypeStruct((t, d), F32)],
        (x_all, mod3, mod3, mod3, norm_w, wg, wu, wd), side)
    return out, cast


def _moe_kernel(x_ref, sh_ref, sc_ref, g2_ref, nw_ref, fw_ref, wr_ref, wg_ref, wu_ref, wd_ref,
                o_ref, h_ref, gate_ref, acc_ref):
    e = pl.program_id(1)
    f = pl.program_id(2)
    lane = lax.broadcasted_iota(jnp.int32, gate_ref.shape, 1).astype(F32)

    @pl.when((e == 0) & (f == 0))
    def _():
        hf = _norm_mod(x_ref[...], nw_ref[...], sh_ref[...], sc_ref[...])
        hb = hf.astype(BF16)
        h_ref[...] = hb
        acc_ref[...] = jnp.zeros_like(acc_ref)
        h_lo = (hf - hb.astype(F32)).astype(BF16)
        wr = wr_ref[...]
        w_hi = wr.astype(BF16)
        w_lo = (wr - w_hi.astype(F32)).astype(BF16)
        logits = _dot(hb, w_hi) + _dot(h_lo, w_hi) + _dot(hb, w_lo)
        logits = jnp.where(lane < N_EXPERTS, logits, -jnp.inf)
        m1 = jnp.max(logits, axis=-1, keepdims=True)
        i1 = jnp.min(jnp.where(logits == m1, lane, float(Z_PAD)), axis=-1, keepdims=True)
        rest = jnp.where(lane == i1, -jnp.inf, logits)
        m2 = jnp.max(rest, axis=-1, keepdims=True)
        i2 = jnp.min(jnp.where(rest == m2, lane, float(Z_PAD)), axis=-1, keepdims=True)
        e2 = jnp.exp(m2 - m1)
        w1 = 1.0 / (1.0 + e2)
        gate_ref[...] = jnp.where(lane == i1, w1, 0.0) + jnp.where(lane == i2, e2 * w1, 0.0)

    h = h_ref[...]
    ge = jnp.sum(jnp.where(lane == e.astype(F32), gate_ref[...], 0.0), axis=-1, keepdims=True)
    a = _swiglu_hidden(h, wg_ref.at[0], wu_ref.at[0], wg_ref.shape[2], ge)
    acc_ref[...] += _dot(a, wd_ref[0])

    @pl.when((e == pl.num_programs(1) - 1) & (f == pl.num_programs(2) - 1))
    def _():
        y = x_ref[...] + g2_ref[...] * acc_ref[...]
        o_ref[...] = y * lax.rsqrt(jnp.mean(y * y, axis=-1, keepdims=True) + EPS) * fw_ref[...]


def _moe(x_lat, mod3, tile_row, tm, norm_w, final_w, w_router, wg, wu, wd, tf):
    t, d = x_lat.shape
    n_e, _, f_dim = wg.shape
    mspec = lambda col: pl.BlockSpec((None, 1, D_MODEL), lambda i, e, j: (tile_row(i), 0, col))
    const = lambda a: pl.BlockSpec(a.shape, lambda i, e, j: (0,) * a.ndim)
    return pl.pallas_call(
        _moe_kernel,
        grid=(t // tm, n_e, f_dim // tf),
        in_specs=[pl.BlockSpec((tm, d), lambda i, e, j: (i, 0)),
                  mspec(3), mspec(4), mspec(5), const(norm_w), const(final_w), const(w_router),
                  pl.BlockSpec((1, d, tf), lambda i, e, j: (e, 0, j)),
                  pl.BlockSpec((1, d, tf), lambda i, e, j: (e, 0, j)),
                  pl.BlockSpec((1, tf, d), lambda i, e, j: (e, j, 0))],
        out_specs=pl.BlockSpec((tm, d), lambda i, e, j: (i, 0)),
        out_shape=jax.ShapeDtypeStruct((t, d), F32),
        scratch_shapes=[pltpu.VMEM((tm, d), BF16), pltpu.VMEM((tm, Z_PAD), F32), pltpu.VMEM((tm, d), F32)],
        compiler_params=_params(3),
        name="moe",
    )(x_lat, mod3, mod3, mod3, norm_w, final_w, w_router, wg, wu, wd)


def _permute_w_in(w):
    o = 0
    parts = {}
    for name, width in (("k", GLA_KEY), ("v", GLA_VAL), ("zf", GLA_RANK), ("zb", GLA_RANK), ("u", S5_WIDTH),
                        ("q", GLA_KEY), ("r", GLA_VAL), ("ga", D_MODEL), ("gm", D_MODEL)):
        parts[name] = w[:, o:o + width]
        o += width
    pad = jnp.zeros((w.shape[0], Z_PAD - 2 * GLA_RANK), w.dtype)
    cols = [parts[n] for n in ("k", "v", "u", "q", "r", "ga", "gm", "zf", "zb")] + [pad]
    return jnp.concatenate(cols, axis=1).astype(BF16)


def _low_rank_gate(lr_f, lr_b, bias_f, bias_b):
    lr = jnp.zeros((Z_PAD, 2 * GLA_KEY), F32)
    lr = lr.at[:GLA_RANK, :GLA_KEY].set(lr_f).at[GLA_RANK:2 * GLA_RANK, GLA_KEY:].set(lr_b)
    return lr.astype(BF16), jnp.concatenate([bias_f, bias_b])[None, :]


def kernel(x, c, ctx, c_ctx, w_mod, b_mod, norm1_w, norm2_w, final_norm_w, w_in, gla_lr_f, gla_lr_b, gla_bias_f, gla_bias_b, gla_norm_w, s5_a_re_f, s5_a_im_f, s5_log_dt_f, s5_a_re_b, s5_a_im_b, s5_log_dt_b, s5_b_re, s5_b_im, s5_c_re, s5_c_im, s5_d, s5_w_glu, s5_b_glu, w_gla_proj, w_s5_proj, w_out, ffn_w_gate, ffn_w_up, ffn_w_down, moe_router, moe_w_gate, moe_w_up, moe_w_down):
    batch, n_lat, d = x.shape
    n_ctx = ctx.shape[1]
    depth = w_mod.shape[0]
    assert d == D_MODEL and batch < 8 and depth == 2
    assert n_lat % n_ctx == 0 and n_ctx % GLA_CHUNK == 0 and (n_lat // GRID_W) % S5_CHUNK == 0
    t_lat, t_ctx = batch * n_lat, batch * n_ctx
    t_all = t_lat + t_ctx

    tm = math.gcd(512, t_ctx)
    assert n_lat % tm == 0
    lat_tiles_per_batch = n_lat // tm
    n_lat_tiles = t_lat // tm
    tile_row = lambda i: jnp.where(i < n_lat_tiles, i // lat_tiles_per_batch, batch)

    cond = jnp.zeros((8, d), F32).at[:batch].set(c).at[batch].set(c_ctx)
    mod = _modulation(cond, w_mod, b_mod)

    stream = (x.reshape(t_lat, d), ctx.reshape(t_ctx, d), n_lat_tiles, 0)
    out = None
    n_e, _, f_e = moe_w_gate.shape[1:]
    moe_f32 = {"gate": moe_w_gate[0].reshape(n_e * d, f_e), "up": moe_w_up[0].reshape(n_e * d, f_e),
               "down": moe_w_down[0].reshape(n_e * f_e, d)}
    moe_bf16 = {}
    for l in range(depth):
        last = l == depth - 1
        first = l == 0
        mod3 = mod[l].reshape(8, 1, 6 * d)
        lr, lb = _low_rank_gate(gla_lr_f[l], gla_lr_b[l], gla_bias_f[l], gla_bias_b[l])
        (k, v, u, q, r, ga, gm, gf, gb), cast = _in_projection(
            stream, t_all, mod3, tile_row, tm, norm1_w[l][None, :], _permute_w_in(w_in[l]), lr, lb,
            moe_f32["gate"] if first else None)
        if first:
            moe_bf16["gate"] = cast

        of, ob = _gla(q, k, v, gf, gb, batch, n_lat // n_ctx, n_ctx)

        bt, ct = _s5_bc_layout(s5_b_re[l], s5_b_im[l], s5_c_re[l], s5_c_im[l])
        y = _s5(u, _s5_param_layout(s5_a_re_f[l], s5_a_im_f[l], s5_log_dt_f[l]),
                _s5_param_layout(s5_a_re_b[l], s5_a_im_b[l], s5_log_dt_b[l]), bt, ct, batch, n_lat, n_ctx)

        merge_w = (gla_norm_w[l][None, :], s5_d[l][None, :], w_gla_proj[l].astype(BF16),
                   s5_w_glu[l].astype(BF16), s5_b_glu[l][None, :], w_s5_proj[l].astype(BF16),
                   w_out[l].astype(BF16))
        n_tiles = n_lat_tiles if last else t_all // tm
        x_mid, cast = _merge(
            of, ob, y, u, r, ga, gm, stream, mod3, tile_row, tm, n_tiles, merge_w, moe_f32["up"] if first else None)
        if first:
            moe_bf16["up"] = cast

        if l % 2 == 0:
            i = l // 2
            x_all, moe_bf16["down"] = _ffn(
                x_mid, mod3, tile_row, tm, norm2_w[l][None, :], ffn_w_gate[i].astype(BF16),
                ffn_w_up[i].astype(BF16), ffn_w_down[i].astype(BF16), moe_f32["down"])
            stream = (x_all, x_all, n_lat_tiles, n_lat_tiles)
        else:
            i = l // 2
            tm_f = math.gcd(1024, n_lat)
            row_f = lambda j, tm_f=tm_f: j // (n_lat // tm_f)
            w_router = jnp.zeros((d, Z_PAD), F32).at[:, :N_EXPERTS].set(moe_router[i])
            res = _moe(x_mid, mod3, row_f, tm_f, norm2_w[l][None, :], final_norm_w[None, :], w_router,
                       moe_bf16["gate"].reshape(n_e, d, f_e), moe_bf16["up"].reshape(n_e, d, f_e),
                       moe_bf16["down"].reshape(n_e, f_e, d), MOE_F_TILE)
            out = res.reshape(batch, n_lat, d)
    return out
```

```python
import functools
import math

import jax
import jax.numpy as jnp
from jax import lax
from jax.experimental import pallas as pl
from jax.experimental.pallas import tpu as pltpu

F32 = jnp.float32
BF16 = jnp.bfloat16
HIGHEST = lax.Precision.HIGHEST

D_MODEL = 1024
GRID_W = 64
GLA_HEADS = 4
GLA_DK = 64
GLA_DV = 128
GLA_KEY = GLA_HEADS * GLA_DK
GLA_VAL = GLA_HEADS * GLA_DV
GLA_RANK = 16
GLA_GATE_NORM = 16.0
GLA_CHUNK = 64
S5_WIDTH = 512
S5_GROUP = 16
S5_GROUPS = S5_WIDTH // S5_GROUP
S5_STATE = 64
S5_CHUNK = 16
S5_CW = S5_CHUNK * S5_GROUP
S5_SW = 2 * S5_STATE
S5_PAD = GRID_W // 2
S5_TILE_GROUPS = 128 // S5_GROUP
S5_GROUP_SHIFT = 4
N_EXPERTS = 8
EPS = 1e-6
MXU_DIM = 256
MOE_F_TILE = 7 * MXU_DIM
SIDE_CAST_BLOCKS = 32
Z_PAD = 128

VMEM_LIMIT = 56 * 1024 * 1024

SEC_K = (0, GLA_KEY)
SEC_V = (SEC_K[1], SEC_K[1] + GLA_VAL)
SEC_U = (SEC_V[1], SEC_V[1] + S5_WIDTH)
SEC_Q = (SEC_U[1], SEC_U[1] + GLA_KEY)
SEC_R = (SEC_Q[1], SEC_Q[1] + GLA_VAL)
SEC_GA = (SEC_R[1], SEC_R[1] + D_MODEL)
SEC_GM = (SEC_GA[1], SEC_GA[1] + D_MODEL)
SEC_Z = (SEC_GM[1], SEC_GM[1] + Z_PAD)
IN_PERM_WIDTH = SEC_Z[1]


def _params(n_axes):
    return pltpu.CompilerParams(dimension_semantics=("arbitrary",) * n_axes,
                                vmem_limit_bytes=VMEM_LIMIT)


def _silu(x):
    return x * jax.nn.sigmoid(x)


def _norm_mod(x, nw, sh, sc):
    y = x * lax.rsqrt(jnp.mean(x * x, axis=-1, keepdims=True) + EPS) * nw
    return y * (1.0 + sc) + sh


def _dot(a, b):
    return jnp.dot(a, b, preferred_element_type=F32)


def _dot_nt(a, b):
    return lax.dot_general(a, b, (((1,), (1,)), ((), ())), preferred_element_type=F32)


def _dot_tn(a, b):
    return lax.dot_general(a, b, (((0,), (0,)), ((), ())), preferred_element_type=F32)


def _mod_kernel(c_ref, w_ref, b_ref, o_ref):
    s = _silu(c_ref[...]).astype(BF16)
    o_ref[0] = _dot(s, w_ref[0].astype(BF16)) + b_ref[0]


def _modulation(cond, w_mod, b_mod):
    depth, d, n = w_mod.shape
    tn = 1536
    return pl.pallas_call(
        _mod_kernel,
        grid=(depth, n // tn),
        in_specs=[pl.BlockSpec((8, d), lambda l, j: (0, 0)),
                  pl.BlockSpec((1, d, tn), lambda l, j: (l, 0, j)),
                  pl.BlockSpec((1, 1, tn), lambda l, j: (l, 0, j))],
        out_specs=pl.BlockSpec((1, 8, tn), lambda l, j: (l, 0, j)),
        out_shape=jax.ShapeDtypeStruct((depth, 8, n), F32),
        compiler_params=_params(2),
        name="modulation",
    )(cond, w_mod, b_mod.reshape(depth, 1, n))


def _hosted_call(body, name, n_steps, in_specs, out_specs, out_shape, operands, side=None):
    if side is None:
        outs = pl.pallas_call(body, grid=(n_steps,), in_specs=in_specs, out_specs=out_specs,
                              out_shape=out_shape, compiler_params=_params(1), name=name)(*operands)
        return outs, None
    rows, cols = side.shape
    n_blk = 1 << (min(n_steps, SIDE_CAST_BLOCKS).bit_length() - 1)
    assert rows % (16 * n_blk) == 0
    spec = pl.BlockSpec((rows // n_blk, cols), lambda i: (jnp.minimum(i, n_blk - 1), 0))
    n_in = len(in_specs)

    def kernel(*refs):
        body(*refs[:n_in], *refs[n_in + 1:-1])
        refs[-1][...] = refs[n_in][...].astype(BF16)

    *outs, cast = pl.pallas_call(
        kernel, grid=(n_steps,), in_specs=list(in_specs) + [spec], out_specs=list(out_specs) + [spec],
        out_shape=list(out_shape) + [jax.ShapeDtypeStruct((rows, cols), BF16)],
        compiler_params=_params(1), name=name)(*operands, side)
    return outs, cast


def _stream_specs(tm, n_lat_tiles, ctx_off):
    lat = pl.BlockSpec((tm, D_MODEL), lambda i: (jnp.minimum(i, n_lat_tiles - 1), 0))
    ctx = pl.BlockSpec((tm, D_MODEL), lambda i: (jnp.maximum(i - n_lat_tiles, 0) + ctx_off, 0))
    return [lat, ctx]


def _stream_tile(lat_ref, ctx_ref, n_lat_tiles):
    return jnp.where(pl.program_id(0) < n_lat_tiles, lat_ref[...], ctx_ref[...])


def _inproj_kernel(xl_ref, xc_ref, sh_ref, sc_ref, nw_ref, w_ref, lr_ref, lb_ref,
                   k_ref, v_ref, u_ref, q_ref, r_ref, ga_ref, gm_ref, gf_ref, gb_ref, *, n_lat_tiles):
    x = _stream_tile(xl_ref, xc_ref, n_lat_tiles)
    h = _norm_mod(x, nw_ref[...], sh_ref[...], sc_ref[...]).astype(BF16)

    def sec(s):
        return _dot(h, w_ref[:, s[0]:s[1]])

    z = sec(SEC_Z).astype(BF16)
    pre = _dot(z, lr_ref[...]) + lb_ref[...]
    k_ref[...] = sec(SEC_K).astype(BF16)
    v_ref[...] = sec(SEC_V).astype(BF16)
    u_ref[...] = sec(SEC_U)
    q_ref[...] = (sec(SEC_Q) * GLA_DK ** -0.5).astype(BF16)
    r_ref[...] = _silu(sec(SEC_R)).astype(BF16)
    ga_ref[...] = jax.nn.sigmoid(sec(SEC_GA)).astype(BF16)
    gm_ref[...] = jax.nn.sigmoid(sec(SEC_GM)).astype(BF16)
    g = (jnp.minimum(pre, 0.0) - jnp.log(1.0 + jnp.exp(-jnp.abs(pre)))) * (1.0 / GLA_GATE_NORM)
    gf_ref[...] = g[:, :GLA_KEY]
    gb_ref[...] = g[:, GLA_KEY:]


def _mod_spec(col, tile_row):
    return pl.BlockSpec((None, 1, D_MODEL), lambda i: (tile_row(i), 0, col))


def _in_projection(stream, t, mod3, tile_row, tm, norm_w, w_perm, lr, lb, side=None):
    x_lat, x_ctx, n_lat_tiles, ctx_off = stream
    widths = (GLA_KEY, GLA_VAL, S5_WIDTH, GLA_KEY, GLA_VAL, D_MODEL, D_MODEL)
    tok = lambda w: pl.BlockSpec((tm, w), lambda i: (i, 0))
    full = lambda a: pl.BlockSpec(a.shape, lambda i: (0,) * a.ndim)
    return _hosted_call(
        functools.partial(_inproj_kernel, n_lat_tiles=n_lat_tiles), "in_projection", t // tm,
        _stream_specs(tm, n_lat_tiles, ctx_off) + [_mod_spec(0, tile_row), _mod_spec(1, tile_row),
                                                   full(norm_w), full(w_perm), full(lr), full(lb)],
        [tok(w) for w in widths] + [tok(GLA_KEY), tok(GLA_KEY)],
        [jax.ShapeDtypeStruct((t, w), F32 if i == 2 else BF16) for i, w in enumerate(widths)]
        + [jax.ShapeDtypeStruct((t, GLA_KEY), F32)] * 2,
        (x_lat, x_ctx, mod3, mod3, norm_w, w_perm, lr, lb), side)


def _gla_kernel(*refs, n_streams):
    s_ref = refs[-1]

    @pl.when(pl.program_id(1) == 0)
    def _():
        s_ref[...] = jnp.zeros_like(s_ref)

    n = refs[0].shape[0]
    nc = n // GLA_CHUNK
    assert nc * GLA_DK == n
    shift = GLA_CHUNK.bit_length() - 1
    row = lax.broadcasted_iota(jnp.int32, (n, n), 0)
    col = lax.broadcasted_iota(jnp.int32, (n, n), 1)
    same = (row >> shift) == (col >> shift)
    dirs = tuple(refs[4 * s:4 * s + 4] + (refs[4 * n_streams + s], s_ref.at[s], s % 2 == 1)
                 for s in range(n_streams))
    nd = range(n_streams)
    heads = range(GLA_HEADS)
    ks = [slice(h * GLA_DK, (h + 1) * GLA_DK) for h in heads]
    vs = [slice(h * GLA_DV, (h + 1) * GLA_DV) for h in heads]

    tris, bcs = [], []
    for _, _, _, g_ref, _, _, reverse in dirs:
        tri = same & ((col >= row) if reverse else (col <= row))
        g = g_ref[...]
        g_hi = g.astype(BF16)
        g_lo = (g - g_hi.astype(F32)).astype(BF16)
        tri_b = tri.astype(BF16)
        tris.append(tri)
        bcs.append(_dot(tri_b, g_hi) + _dot(tri_b, g_lo))

    qds, kds, kls, decs, vals = [], [], [], [], []
    for (q_ref, k_ref, v_ref, _, _, _, reverse), bc in zip(dirs, bcs):
        last = 0 if reverse else GLA_CHUNK - 1
        bl3 = bc.reshape(nc, GLA_CHUNK, GLA_KEY)[:, last:last + 1, :]
        bl = jnp.broadcast_to(bl3, (nc, GLA_CHUNK, GLA_KEY)).reshape(n, GLA_KEY)
        q = q_ref[...].astype(F32)
        k = k_ref[...].astype(F32)
        qds.append((q * jnp.exp(bc)).astype(BF16))
        kds.append((k * jnp.exp(-bc)).astype(BF16))
        kls.append((k * jnp.exp(bl - bc)).astype(BF16))
        decs.append(jnp.exp(bl3))
        vals.append(v_ref[...])

    zero = jnp.zeros((), BF16)
    scores = [[_dot_nt(qds[d][:, ks[h]], kds[d][:, ks[h]]) for h in heads] for d in nd]
    incs = [[_dot_tn(vals[d][:, vs[h]], jnp.where(same, jnp.concatenate([kls[d][:, ks[h]]] * nc, axis=1), zero))
             for h in heads] for d in nd]
    intra = [[_dot(jnp.where(tris[d], scores[d][h], 0.0).astype(BF16), vals[d][:, vs[h]]) for h in heads]
             for d in nd]

    st_alls = []
    for d, (_, _, _, _, _, s_ref, reverse) in enumerate(dirs):
        order = list(range(nc))[::-1] if reverse else list(range(nc))
        st_alls.append([])
        for h in heads:
            st = s_ref[h]
            entering = {}
            for c in order:
                entering[c] = st
                st = st * decs[d][c][:, ks[h]] + incs[d][h][:, c * GLA_DK:(c + 1) * GLA_DK]
            s_ref[h] = st
            st_alls[d].append(jnp.concatenate([entering[c] for c in range(nc)], axis=1).astype(BF16))

    inter = [[_dot_nt(jnp.where(same, jnp.concatenate([qds[d][:, ks[h]]] * nc, axis=1), zero), st_alls[d][h])
              for h in heads] for d in nd]
    for d, (_, _, _, _, o_ref, _, _) in enumerate(dirs):
        o_ref[...] = jnp.concatenate([intra[d][h] + inter[d][h] for h in heads], axis=1).astype(o_ref.dtype)


def _gla(q, k, v, gf, gb, batch, n_lat_blocks, blk):
    t = q.shape[0]
    ctx0 = batch * n_lat_blocks

    def fwd(b, i):
        return (jnp.where(i == 0, ctx0 + b, b * n_lat_blocks + i - 1), 0)

    def bwd(b, i):
        return (jnp.where(i == 0, ctx0 + b, b * n_lat_blocks + n_lat_blocks - i), 0)

    spec = lambda w, m: pl.BlockSpec((blk, w), m)
    return pl.pallas_call(
        functools.partial(_gla_kernel, n_streams=2),
        grid=(batch, n_lat_blocks + 1),
        in_specs=[spec(GLA_KEY, fwd), spec(GLA_KEY, fwd), spec(GLA_VAL, fwd), spec(GLA_KEY, fwd),
                  spec(GLA_KEY, bwd), spec(GLA_KEY, bwd), spec(GLA_VAL, bwd), spec(GLA_KEY, bwd)],
        out_specs=[spec(GLA_VAL, fwd), spec(GLA_VAL, bwd)],
        out_shape=[jax.ShapeDtypeStruct((t, GLA_VAL), BF16)] * 2,
        scratch_shapes=[pltpu.VMEM((2, GLA_HEADS, GLA_DV, GLA_DK), F32)],
        compiler_params=_params(2),
        name="gla_bidir",
    )(q, k, v, gf, q, k, v, gb)


def _s5_group_tables(arow, acol, log_dt, bt, ct, reverse):
    delta = jnp.exp(log_dt)
    lane_s = lax.broadcasted_iota(jnp.int32, (1, S5_SW), 1)
    sign = jnp.where(lane_s < S5_STATE, -1.0, 1.0)
    ar, ai = arow[0:1, :], arow[1:2, :]
    zr, zi = ar * delta, ai * delta

    def row_power(n):
        mag = jnp.exp(zr * n)
        return mag * jnp.cos(zi * n), mag * jnp.sin(zi * n)

    one = jnp.ones((1, 1), F32)
    l_r, l_i = row_power(one)
    den = ar * ar + ai * ai
    xr, xi = l_r - 1.0, l_i
    cr = (xr * ar + xi * ai) / den
    ci = (xi * ar - xr * ai) / den
    bdup = cr * bt + ci * (pltpu.roll(bt, S5_STATE, 1) * sign)
    bswp = pltpu.roll(bdup, S5_STATE, 1) * sign

    step = lax.broadcasted_iota(jnp.int32, (S5_CHUNK, 1), 0).astype(F32)
    er, ei = row_power(step if reverse else (S5_CHUNK - 1.0 - step))
    to_state = jnp.concatenate(
        [er[s:s + 1, :] * bdup + ei[s:s + 1, :] * bswp for s in range(S5_CHUNK)], axis=0)
    bdup_sw, bswp_sw = pltpu.roll(bdup, S5_STATE, 1), pltpu.roll(bswp, S5_STATE, 1)
    to_state_sw = jnp.concatenate(
        [er[s:s + 1, :] * bdup_sw + ei[s:s + 1, :] * bswp_sw for s in range(S5_CHUNK)], axis=0)

    zrc, zic = acol[:, 0:1] * delta, acol[:, 1:2] * delta
    lane_c = lax.broadcasted_iota(jnp.int32, (1, S5_CW), 1)
    src_row = S5_CHUNK - 1 - lax.broadcasted_iota(jnp.int32, (S5_CHUNK, 1), 0)
    spread = ((lane_c >> S5_GROUP_SHIFT) == src_row).astype(BF16)

    def to_columns(a):
        hi = a.astype(BF16)
        lo = (a - hi.astype(F32)).astype(BF16)
        return (_dot_tn(hi, spread) + _dot_tn(lo, spread))[:S5_STATE]

    pr, pi = to_columns(er), to_columns(ei)
    ctr, cti = ct[:S5_STATE, :], ct[S5_STATE:, :]
    kr, ki = ctr * pr - cti * pi, ctr * pi + cti * pr
    lam_mag = jnp.exp(zrc)
    lcr, lci = lam_mag * jnp.cos(zic), lam_mag * jnp.sin(zic)
    from_state = jnp.concatenate([kr * lcr - ki * lci, -(kr * lci + ki * lcr)], axis=0)
    krow = jnp.dot(bdup, jnp.concatenate([kr, -ki], axis=0), precision=HIGHEST,
                   preferred_element_type=F32)
    blocks = []
    for s in range(S5_CHUNK):
        if reverse:
            sh = S5_GROUP * (S5_CHUNK - 1 - s)
            blk = pltpu.roll(krow, (S5_CW - sh) % S5_CW, 1) if sh else krow
            blocks.append(jnp.where(lane_c < S5_GROUP * (s + 1), blk, 0.0))
        else:
            sh = S5_GROUP * s
            blk = pltpu.roll(krow, sh, 1) if sh else krow
            blocks.append(jnp.where(lane_c >= sh, blk, 0.0))
    toeplitz = jnp.concatenate(blocks, axis=0)

    wr, wi = row_power(one * float(S5_CHUNK))
    decay = []
    for _ in range(8):
        decay.append((wr, wi * sign))
        wr, wi = wr * wr - wi * wi, 2.0 * wr * wi
    to_state = jnp.concatenate([to_state, to_state_sw], axis=1)
    return toeplitz.astype(BF16), to_state.astype(BF16), from_state.astype(BF16), decay


def _cmul_pair(wa, wb, z):
    nb = z.shape[0] // 2
    partner = jnp.concatenate([z[nb:], z[:nb]], axis=0)
    wbs = jnp.concatenate([jnp.broadcast_to(wb, (nb, 1, S5_SW)), jnp.broadcast_to(-wb, (nb, 1, S5_SW))], axis=0)
    return wa * z + wbs * partner


def _s5_segment_scan(x, h0, decay, lvl0, scr, reverse):
    n = x.shape[1]
    lo = S5_PAD
    first, last = (n - 1, 0) if reverse else (0, n - 1)
    scr[...] = jnp.zeros_like(scr)
    scr[:, lo:lo + n, :] = x
    if h0 is not None:
        scr[:, lo + first:lo + first + 1, :] = x[:, first:first + 1, :] + _cmul_pair(*decay[lvl0], h0)
    off, lvl = 1, lvl0
    while off < n:
        src = lo + off if reverse else lo - off
        scr[:, lo:lo + n, :] = scr[:, lo:lo + n, :] + _cmul_pair(*decay[lvl], scr[:, src:src + n, :])
        off, lvl = off * 2, lvl + 1
    h_out = scr[:, lo + last:lo + last + 1, :]
    edge = lo + n if reverse else lo - 1
    if h0 is not None:
        scr[:, edge:edge + 1, :] = h0
    start = lo + 1 if reverse else lo - 1
    return scr[:, start:start + n, :], h_out


def _s5_direction(hloc, hloc_sw, decay, scr, reverse, nb, n_rq, n_cc):
    n_lat_rows = nb * n_rq * GRID_W
    pair = lambda f: jnp.concatenate([f(hloc), f(hloc_sw)], axis=0)
    lat = pair(lambda a: a[:n_lat_rows].reshape(nb, n_rq, GRID_W, S5_SW))
    ctx = pair(lambda a: a[n_lat_rows:].reshape(nb, n_cc, S5_SW))
    h_ctx_rows, h_ctx = _s5_segment_scan(ctx, None, decay, 0, scr, reverse)
    step = lambda h: _cmul_pair(*decay[0], h)
    order = list(range(n_rq))[::-1] if reverse else list(range(n_rq))
    col = lat[:, order[0]]
    for rq in order[1:]:
        col = step(col) + lat[:, rq]
    lvl_col = int(math.log2(n_rq))
    h_col, _ = _s5_segment_scan(col, h_ctx, decay, lvl_col, scr, reverse)
    starts = {order[0]: h_col}
    for prev, rq in zip(order[:-1], order[1:]):
        starts[rq] = step(starts[prev]) + lat[:, prev]
    lat_start = jnp.stack([starts[rq][:nb] for rq in range(n_rq)], axis=1)
    return jnp.concatenate([lat_start.reshape(n_lat_rows, S5_SW), h_ctx_rows[:nb].reshape(nb * n_cc, S5_SW)], axis=0)


def _s5_kernel(u_ref, arf_ref, arb_ref, acf_ref, acb_ref, ldf_ref, ldb_ref, bt_ref, ct_ref,
               y_ref, xrot_ref, zacc_ref, scr, *, nb, n_lat, n_ctx):
    j = pl.program_id(0) % S5_TILE_GROUPS
    n_rq = n_lat // (GRID_W * S5_CHUNK)
    n_cc = n_ctx // S5_CHUNK
    lat_rows = n_rq * GRID_W
    n_rows = nb * (lat_rows + n_cc)
    lanes = S5_TILE_GROUPS * S5_GROUP
    slot = lax.broadcasted_iota(jnp.int32, (1, lanes), 1) >> S5_GROUP_SHIFT
    wrap = S5_TILE_GROUPS - 1

    runs = []
    for b in range(nb):
        for rq in range(n_rq):
            runs.append((lambda t, b=b, rq=rq: pl.ds(b * n_lat + (rq * S5_CHUNK + t) * GRID_W, GRID_W),
                         b * lat_rows + rq * GRID_W, GRID_W))
    for b in range(nb):
        runs.append((lambda t, b=b: pl.ds(nb * n_lat + b * n_ctx + t, n_cc, stride=S5_CHUNK),
                     nb * lat_rows + b * n_cc, n_cc))

    @pl.when(j == 0)
    def _():
        for tok_rows, row0, n in runs:
            for t in range(S5_CHUNK):
                x = pltpu.roll(u_ref[tok_rows(t), :], (t & wrap) * S5_GROUP, 1)
                xrot_ref[t, pl.ds(row0, n), :] = x.astype(BF16)

    halves = []
    for th in range(2):
        acc = xrot_ref[th * S5_TILE_GROUPS]
        for tp in range(1, S5_TILE_GROUPS):
            acc = jnp.where(slot == ((tp + j) & wrap), xrot_ref[th * S5_TILE_GROUPS + tp], acc)
        halves.append(pltpu.roll(acc.astype(F32), ((-j) & wrap) * S5_GROUP, 1))
    u = jnp.concatenate(halves, axis=1).astype(BF16)

    tabs = [_s5_group_tables(ar_ref[0], ac_ref[0], ld_ref[0], bt_ref[0], ct_ref[0], reverse)
            for reverse, ar_ref, ac_ref, ld_ref in ((False, arf_ref, acf_ref, ldf_ref), (True, arb_ref, acb_ref, ldb_ref))]
    hlocs = [_dot(u, tab[1]) for tab in tabs]
    y = _dot(u, tabs[0][0]) + _dot(u, tabs[1][0])
    hstarts = [_s5_direction(hlocs[d][:, :S5_SW], hlocs[d][:, S5_SW:], tabs[d][3], scr, d == 1, nb, n_rq, n_cc
                             ).astype(BF16) for d in range(2)]
    y = y + _dot(hstarts[0], tabs[0][2]) + _dot(hstarts[1], tabs[1][2])

    for th in range(2):
        rolled = pltpu.roll(y[:, th * lanes:(th + 1) * lanes], j * S5_GROUP, 1)
        for tp in range(S5_TILE_GROUPS):
            base = (th * S5_TILE_GROUPS + tp) * n_rows
            for r0 in range(0, n_rows, lanes):
                nr = min(lanes, n_rows - r0)
                own = jnp.broadcast_to(slot == ((tp + j) & wrap), (nr, lanes))
                pltpu.store(zacc_ref.at[pl.ds(base + r0, nr), :], rolled[r0:r0 + nr], mask=own)

    @pl.when(j == S5_TILE_GROUPS - 1)
    def _():
        for tok_rows, row0, n in runs:
            for t in range(S5_CHUNK):
                z = zacc_ref[pl.ds(t * n_rows + row0, n), :]
                y_ref[tok_rows(t), :] = pltpu.roll(z, ((-t) & wrap) * S5_GROUP, 1)


def _s5(u, params_f, params_b, bt, ct, nb, n_lat, n_ctx):
    t, width = u.shape
    lanes = S5_TILE_GROUPS * S5_GROUP
    n_rows = nb * (n_lat + n_ctx) // S5_CHUNK
    tile = pl.BlockSpec((t, lanes), lambda g: (0, g // S5_TILE_GROUPS))
    grp = lambda a: pl.BlockSpec((1,) + a.shape[1:], lambda g: (g, 0, 0))
    arf, acf, ldf = params_f
    arb, acb, ldb = params_b
    ops = (arf, arb, acf, acb, ldf, ldb, bt, ct)
    return pl.pallas_call(
        functools.partial(_s5_kernel, nb=nb, n_lat=n_lat, n_ctx=n_ctx),
        grid=(S5_GROUPS,),
        in_specs=[tile] + [grp(a) for a in ops],
        out_specs=tile,
        out_shape=jax.ShapeDtypeStruct((t, width), F32),
        scratch_shapes=[pltpu.VMEM((S5_CHUNK, n_rows, lanes), BF16),
                        pltpu.VMEM((S5_CHUNK * n_rows, lanes), F32),
                        pltpu.VMEM((2 * nb, 2 * S5_PAD + GRID_W, S5_SW), F32)],
        compiler_params=_params(1),
        name="s5_bidir",
    )(u, *ops)


def _s5_param_layout(a_re, a_im, log_dt):
    arow = jnp.stack([jnp.concatenate([a_re, a_re], axis=-1), jnp.concatenate([a_im, a_im], axis=-1)], axis=1)
    acol = jnp.stack([a_re, a_im], axis=-1)
    return arow, acol, log_dt[:, None, None]


def _s5_bc_layout(b_re, b_im, c_re, c_im):
    bt = jnp.concatenate([b_re.transpose(0, 2, 1), b_im.transpose(0, 2, 1)], axis=-1)
    ct = jnp.concatenate([c_re.transpose(0, 2, 1), c_im.transpose(0, 2, 1)], axis=1)
    return bt, jnp.tile(ct, (1, 1, S5_CHUNK))


def _gelu_tanh(x):
    return 0.5 * x * (1.0 + jnp.tanh(math.sqrt(2.0 / math.pi) * (x + 0.044715 * (x * x * x))))


def _merge_kernel(of_ref, ob_ref, y_ref, u_ref, r_ref, ga_ref, gm_ref, xl_ref, xc_ref, g1_ref,
                  gnw_ref, d_ref, wgp_ref, wglu_ref, bglu_ref, wsp_ref, wout_ref, o_ref, *, n_lat_tiles):
    o = of_ref[...].astype(F32) + ob_ref[...].astype(F32)
    heads = []
    for h in range(GLA_HEADS):
        oh = o[:, h * GLA_DV:(h + 1) * GLA_DV]
        heads.append(oh * lax.rsqrt(jnp.mean(oh * oh, axis=-1, keepdims=True) + EPS) * gnw_ref[...])
    og = jnp.concatenate(heads, axis=1) * r_ref[...].astype(F32)
    ya = _dot(og.astype(BF16), wgp_ref[...])
    s = _gelu_tanh(y_ref[...] + d_ref[...] * u_ref[...])
    s = s * jax.nn.sigmoid(_dot(s.astype(BF16), wglu_ref[...]) + bglu_ref[...])
    yb = _dot(s.astype(BF16), wsp_ref[...])
    m = ga_ref[...].astype(F32) * ya + gm_ref[...].astype(F32) * yb
    x = _stream_tile(xl_ref, xc_ref, n_lat_tiles)
    o_ref[...] = x + g1_ref[...] * _dot(m.astype(BF16), wout_ref[...])


def _merge(of, ob, y, u, r, ga, gm, stream, mod3, tile_row, tm, n_tiles, weights, side=None):
    x_lat, x_ctx, n_lat_tiles, ctx_off = stream
    d = D_MODEL
    tok = lambda w: pl.BlockSpec((tm, w), lambda i: (i, 0))
    full = lambda a: pl.BlockSpec(a.shape, lambda i: (0,) * a.ndim)
    (x_mid,), cast = _hosted_call(
        functools.partial(_merge_kernel, n_lat_tiles=n_lat_tiles), "merge", n_tiles,
        [tok(GLA_VAL), tok(GLA_VAL), tok(S5_WIDTH), tok(S5_WIDTH), tok(GLA_VAL), tok(d), tok(d)]
        + _stream_specs(tm, n_lat_tiles, ctx_off) + [_mod_spec(2, tile_row)] + [full(a) for a in weights],
        [tok(d)], [jax.ShapeDtypeStruct((n_tiles * tm, d), F32)],
        (of, ob, y, u, r, ga, gm, x_lat, x_ctx, mod3, *weights), side)
    return x_mid, cast


def _swiglu_hidden(h, wg, wu, width, scale=None):
    parts = []
    for c0 in range(0, width, MXU_DIM):
        cols = slice(c0, c0 + MXU_DIM)
        a = _silu(_dot(h, wg[:, cols])) * _dot(h, wu[:, cols])
        parts.append((a if scale is None else a * scale).astype(BF16))
    return jnp.concatenate(parts, axis=1)


def _ffn_kernel(x_ref, sh_ref, sc_ref, g2_ref, nw_ref, wg_ref, wu_ref, wd_ref, o_ref):
    x = x_ref[...]
    h = _norm_mod(x, nw_ref[...], sh_ref[...], sc_ref[...]).astype(BF16)
    a = _swiglu_hidden(h, wg_ref, wu_ref, wg_ref.shape[1])
    o_ref[...] = x + g2_ref[...] * _dot(a, wd_ref[...])


def _ffn(x_all, mod3, tile_row, tm, norm_w, wg, wu, wd, side=None):
    t, d = x_all.shape
    full = lambda a: pl.BlockSpec(a.shape, lambda i: (0,) * a.ndim)
    once = lambda a: pl.BlockSpec(a.shape, lambda i: (0,) * a.ndim, pipeline_mode=pl.Buffered(1))
    (out,), cast = _hosted_call(
        _ffn_kernel, "ffn_dense", t // tm,
        [pl.BlockSpec((tm, d), lambda i: (i, 0)),
         _mod_spec(3, tile_row), _mod_spec(4, tile_row), _mod_spec(5, tile_row),
         full(norm_w), once(wg), once(wu), once(wd)],
        [pl.BlockSpec((tm, d), lambda i: (i, 0))], [jax.ShapeDtypeStruct((t, d), F32)],
        (x_all, mod3, mod3, mod3, norm_w, wg, wu, wd), side)
    return out, cast


def _moe_kernel(x_ref, sh_ref, sc_ref, g2_ref, nw_ref, fw_ref, wr_ref, wg_ref, wu_ref, wd_ref,
                o_ref, h_ref, gate_ref, acc_ref):
    e = pl.program_id(1)
    f = pl.program_id(2)
    lane = lax.broadcasted_iota(jnp.int32, gate_ref.shape, 1).astype(F32)

    @pl.when((e == 0) & (f == 0))
    def _():
        hf = _norm_mod(x_ref[...], nw_ref[...], sh_ref[...], sc_ref[...])
        hb = hf.astype(BF16)
        h_ref[...] = hb
        acc_ref[...] = jnp.zeros_like(acc_ref)
        h_lo = (hf - hb.astype(F32)).astype(BF16)
        wr = wr_ref[...]
        w_hi = wr.astype(BF16)
        w_lo = (wr - w_hi.astype(F32)).astype(BF16)
        logits = _dot(hb, w_hi) + _dot(h_lo, w_hi) + _dot(hb, w_lo)
        logits = jnp.where(lane < N_EXPERTS, logits, -jnp.inf)
        m1 = jnp.max(logits, axis=-1, keepdims=True)
        i1 = jnp.min(jnp.where(logits == m1, lane, float(Z_PAD)), axis=-1, keepdims=True)
        rest = jnp.where(lane == i1, -jnp.inf, logits)
        m2 = jnp.max(rest, axis=-1, keepdims=True)
        i2 = jnp.min(jnp.where(rest == m2, lane, float(Z_PAD)), axis=-1, keepdims=True)
        e2 = jnp.exp(m2 - m1)
        w1 = 1.0 / (1.0 + e2)
        gate_ref[...] = jnp.where(lane == i1, w1, 0.0) + jnp.where(lane == i2, e2 * w1, 0.0)

    h = h_ref[...]
    ge = jnp.sum(jnp.where(lane == e.astype(F32), gate_ref[...], 0.0), axis=-1, keepdims=True)
    a = _swiglu_hidden(h, wg_ref.at[0], wu_ref.at[0], wg_ref.shape[2], ge)
    acc_ref[...] += _dot(a, wd_ref[0])

    @pl.when((e == pl.num_programs(1) - 1) & (f == pl.num_programs(2) - 1))
    def _():
        y = x_ref[...] + g2_ref[...] * acc_ref[...]
        o_ref[...] = y * lax.rsqrt(jnp.mean(y * y, axis=-1, keepdims=True) + EPS) * fw_ref[...]


def _moe(x_lat, mod3, tile_row, tm, norm_w, final_w, w_router, wg, wu, wd, tf):
    t, d = x_lat.shape
    n_e, _, f_dim = wg.shape
    mspec = lambda col: pl.BlockSpec((None, 1, D_MODEL), lambda i, e, j: (tile_row(i), 0, col))
    const = lambda a: pl.BlockSpec(a.shape, lambda i, e, j: (0,) * a.ndim)
    return pl.pallas_call(
        _moe_kernel,
        grid=(t // tm, n_e, f_dim // tf),
        in_specs=[pl.BlockSpec((tm, d), lambda i, e, j: (i, 0)),
                  mspec(3), mspec(4), mspec(5), const(norm_w), const(final_w), const(w_router),
                  pl.BlockSpec((1, d, tf), lambda i, e, j: (e, 0, j)),
                  pl.BlockSpec((1, d, tf), lambda i, e, j: (e, 0, j)),
                  pl.BlockSpec((1, tf, d), lambda i, e, j: (e, j, 0))],
        out_specs=pl.BlockSpec((tm, d), lambda i, e, j: (i, 0)),
        out_shape=jax.ShapeDtypeStruct((t, d), F32),
        scratch_shapes=[pltpu.VMEM((tm, d), BF16), pltpu.VMEM((tm, Z_PAD), F32), pltpu.VMEM((tm, d), F32)],
        compiler_params=_params(3),
        name="moe",
    )(x_lat, mod3, mod3, mod3, norm_w, final_w, w_router, wg, wu, wd)


def _w_in_kernel(w_ref, o_ref):
    w = w_ref[0]
    z0 = GLA_KEY + GLA_VAL
    z1 = z0 + 2 * GLA_RANK
    pad = jnp.zeros((w.shape[0], Z_PAD - 2 * GLA_RANK), w.dtype)
    o_ref[0] = jnp.concatenate([w[:, :z0], w[:, z1:], w[:, z0:z1], pad], axis=1).astype(BF16)


def _permute_w_in(w_in):
    depth, d, width = w_in.shape
    rows = 256
    return pl.pallas_call(
        _w_in_kernel,
        grid=(depth, d // rows),
        in_specs=[pl.BlockSpec((1, rows, width), lambda l, i: (l, i, 0))],
        out_specs=pl.BlockSpec((1, rows, IN_PERM_WIDTH), lambda l, i: (l, i, 0)),
        out_shape=jax.ShapeDtypeStruct((depth, d, IN_PERM_WIDTH), BF16),
        compiler_params=_params(2),
        name="w_in_layout",
    )(w_in)


def _low_rank_gate(lr_f, lr_b, bias_f, bias_b):
    lr = jnp.zeros((Z_PAD, 2 * GLA_KEY), F32)
    lr = lr.at[:GLA_RANK, :GLA_KEY].set(lr_f).at[GLA_RANK:2 * GLA_RANK, GLA_KEY:].set(lr_b)
    return lr.astype(BF16), jnp.concatenate([bias_f, bias_b])[None, :]


def kernel(x, c, ctx, c_ctx, w_mod, b_mod, norm1_w, norm2_w, final_norm_w, w_in, gla_lr_f, gla_lr_b, gla_bias_f, gla_bias_b, gla_norm_w, s5_a_re_f, s5_a_im_f, s5_log_dt_f, s5_a_re_b, s5_a_im_b, s5_log_dt_b, s5_b_re, s5_b_im, s5_c_re, s5_c_im, s5_d, s5_w_glu, s5_b_glu, w_gla_proj, w_s5_proj, w_out, ffn_w_gate, ffn_w_up, ffn_w_down, moe_router, moe_w_gate, moe_w_up, moe_w_down):
    batch, n_lat, d = x.shape
    n_ctx = ctx.shape[1]
    depth = w_mod.shape[0]
    assert d == D_MODEL and batch < 8 and depth == 2
    assert n_lat % n_ctx == 0 and n_ctx % GLA_CHUNK == 0 and (n_lat // GRID_W) % S5_CHUNK == 0
    t_lat, t_ctx = batch * n_lat, batch * n_ctx
    t_all = t_lat + t_ctx

    tm = math.gcd(512, t_ctx)
    assert n_lat % tm == 0
    lat_tiles_per_batch = n_lat // tm
    n_lat_tiles = t_lat // tm
    tile_row = lambda i: jnp.where(i < n_lat_tiles, i // lat_tiles_per_batch, batch)

    cond = jnp.zeros((8, d), F32).at[:batch].set(c).at[batch].set(c_ctx)
    mod = _modulation(cond, w_mod, b_mod)

    stream = (x.reshape(t_lat, d), ctx.reshape(t_ctx, d), n_lat_tiles, 0)
    out = None
    n_e, _, f_e = moe_w_gate.shape[1:]
    moe_f32 = {"gate": moe_w_gate[0].reshape(n_e * d, f_e), "up": moe_w_up[0].reshape(n_e * d, f_e),
               "down": moe_w_down[0].reshape(n_e * f_e, d)}
    moe_bf16 = {}
    w_in_perm = _permute_w_in(w_in)
    for l in range(depth):
        last = l == depth - 1
        mod3 = mod[l].reshape(8, 1, 6 * d)
        lr, lb = _low_rank_gate(gla_lr_f[l], gla_lr_b[l], gla_bias_f[l], gla_bias_b[l])
        hosted = ("gate", "up")[l]
        (k, v, u, q, r, ga, gm, gf, gb), moe_bf16[hosted] = _in_projection(
            stream, t_all, mod3, tile_row, tm, norm1_w[l][None, :], w_in_perm[l], lr, lb,
            moe_f32[hosted])

        of, ob = _gla(q, k, v, gf, gb, batch, n_lat // n_ctx, n_ctx)

        bt, ct = _s5_bc_layout(s5_b_re[l], s5_b_im[l], s5_c_re[l], s5_c_im[l])
        y = _s5(u, _s5_param_layout(s5_a_re_f[l], s5_a_im_f[l], s5_log_dt_f[l]),
                _s5_param_layout(s5_a_re_b[l], s5_a_im_b[l], s5_log_dt_b[l]), bt, ct, batch, n_lat, n_ctx)

        merge_w = (gla_norm_w[l][None, :], s5_d[l][None, :], w_gla_proj[l].astype(BF16),
                   s5_w_glu[l].astype(BF16), s5_b_glu[l][None, :], w_s5_proj[l].astype(BF16),
                   w_out[l].astype(BF16))
        n_tiles = n_lat_tiles if last else t_all // tm
        x_mid, _ = _merge(of, ob, y, u, r, ga, gm, stream, mod3, tile_row, tm, n_tiles, merge_w)

        if l % 2 == 0:
            i = l // 2
            x_all, moe_bf16["down"] = _ffn(
                x_mid, mod3, tile_row, tm, norm2_w[l][None, :], ffn_w_gate[i].astype(BF16),
                ffn_w_up[i].astype(BF16), ffn_w_down[i].astype(BF16), moe_f32["down"])
            stream = (x_all, x_all, n_lat_tiles, n_lat_tiles)
        else:
            i = l // 2
            tm_f = math.gcd(1024, n_lat)
            row_f = lambda j, tm_f=tm_f: j // (n_lat // tm_f)
            w_router = jnp.zeros((d, Z_PAD), F32).at[:, :N_EXPERTS].set(moe_router[i])
            res = _moe(x_mid, mod3, row_f, tm_f, norm2_w[l][None, :], final_norm_w[None, :], w_router,
                       moe_bf16["gate"].reshape(n_e, d, f_e), moe_bf16["up"].reshape(n_e, d, f_e),
                       moe_bf16["down"].reshape(n_e, f_e, d), MOE_F_TILE)
            out = res.reshape(batch, n_lat, d)
    return out
```

```python
import functools
import math

import jax
import jax.numpy as jnp
from jax import lax
from jax.experimental import pallas as pl
from jax.experimental.pallas import tpu as pltpu

F32 = jnp.float32
BF16 = jnp.bfloat16
HIGHEST = lax.Precision.HIGHEST

D_MODEL = 1024
GRID_W = 64
GLA_HEADS = 4
GLA_DK = 64
GLA_DV = 128
GLA_KEY = GLA_HEADS * GLA_DK
GLA_VAL = GLA_HEADS * GLA_DV
GLA_RANK = 16
GLA_GATE_NORM = 16.0
GLA_CHUNK = 64
GLA_BLOCK = 256
GLA_LATENT_BLOCK = 1024
S5_WIDTH = 512
S5_GROUP = 16
S5_GROUPS = S5_WIDTH // S5_GROUP
S5_STATE = 64
S5_CHUNK = 16
S5_CW = S5_CHUNK * S5_GROUP
S5_SW = 2 * S5_STATE
S5_PAD = GRID_W // 2
S5_TILE_GROUPS = 128 // S5_GROUP
S5_GROUP_SHIFT = 4
N_EXPERTS = 8
EPS = 1e-6
MXU_DIM = 256
MOE_F_TILE = 7 * MXU_DIM
SIDE_CAST_BLOCKS = 32
Z_PAD = 128

VMEM_LIMIT = 56 * 1024 * 1024

SEC_K = (0, GLA_KEY)
SEC_V = (SEC_K[1], SEC_K[1] + GLA_VAL)
SEC_U = (SEC_V[1], SEC_V[1] + S5_WIDTH)
SEC_Q = (SEC_U[1], SEC_U[1] + GLA_KEY)
SEC_R = (SEC_Q[1], SEC_Q[1] + GLA_VAL)
SEC_GA = (SEC_R[1], SEC_R[1] + D_MODEL)
SEC_GM = (SEC_GA[1], SEC_GA[1] + D_MODEL)
SEC_Z = (SEC_GM[1], SEC_GM[1] + Z_PAD)
IN_PERM_WIDTH = SEC_Z[1]


def _params(n_axes):
    return pltpu.CompilerParams(dimension_semantics=("arbitrary",) * n_axes,
                                vmem_limit_bytes=VMEM_LIMIT)


def _silu(x):
    return x * jax.nn.sigmoid(x)


def _norm_mod(x, nw, sh, sc):
    y = x * lax.rsqrt(jnp.mean(x * x, axis=-1, keepdims=True) + EPS) * nw
    return y * (1.0 + sc) + sh


def _dot(a, b):
    return jnp.dot(a, b, preferred_element_type=F32)


def _dot_nt(a, b):
    return lax.dot_general(a, b, (((1,), (1,)), ((), ())), preferred_element_type=F32)


def _dot_tn(a, b):
    return lax.dot_general(a, b, (((0,), (0,)), ((), ())), preferred_element_type=F32)


def _mod_kernel(c_ref, w_ref, b_ref, o_ref):
    s = _silu(c_ref[...]).astype(BF16)
    o_ref[0] = _dot(s, w_ref[0].astype(BF16)) + b_ref[0]


def _modulation(cond, w_mod, b_mod):
    depth, d, n = w_mod.shape
    tn = 1536
    return pl.pallas_call(
        _mod_kernel,
        grid=(depth, n // tn),
        in_specs=[pl.BlockSpec((8, d), lambda l, j: (0, 0)),
                  pl.BlockSpec((1, d, tn), lambda l, j: (l, 0, j)),
                  pl.BlockSpec((1, 1, tn), lambda l, j: (l, 0, j))],
        out_specs=pl.BlockSpec((1, 8, tn), lambda l, j: (l, 0, j)),
        out_shape=jax.ShapeDtypeStruct((depth, 8, n), F32),
        compiler_params=_params(2),
        name="modulation",
    )(cond, w_mod, b_mod.reshape(depth, 1, n))


def _hosted_call(body, name, n_steps, in_specs, out_specs, out_shape, operands, side=None):
    if side is None:
        outs = pl.pallas_call(body, grid=(n_steps,), in_specs=in_specs, out_specs=out_specs,
                              out_shape=out_shape, compiler_params=_params(1), name=name)(*operands)
        return outs, None
    rows, cols = side.shape
    n_blk = 1 << (min(n_steps, SIDE_CAST_BLOCKS).bit_length() - 1)
    assert rows % (16 * n_blk) == 0
    spec = pl.BlockSpec((rows // n_blk, cols), lambda i: (jnp.minimum(i, n_blk - 1), 0))
    n_in = len(in_specs)

    def kernel(*refs):
        body(*refs[:n_in], *refs[n_in + 1:-1])
        refs[-1][...] = refs[n_in][...].astype(BF16)

    *outs, cast = pl.pallas_call(
        kernel, grid=(n_steps,), in_specs=list(in_specs) + [spec], out_specs=list(out_specs) + [spec],
        out_shape=list(out_shape) + [jax.ShapeDtypeStruct((rows, cols), BF16)],
        compiler_params=_params(1), name=name)(*operands, side)
    return outs, cast


def _stream_specs(tm, n_lat_tiles, ctx_off):
    lat = pl.BlockSpec((tm, D_MODEL), lambda i: (jnp.minimum(i, n_lat_tiles - 1), 0))
    ctx = pl.BlockSpec((tm, D_MODEL), lambda i: (jnp.maximum(i - n_lat_tiles, 0) + ctx_off, 0))
    return [lat, ctx]


def _stream_tile(lat_ref, ctx_ref, n_lat_tiles):
    return jnp.where(pl.program_id(0) < n_lat_tiles, lat_ref[...], ctx_ref[...])


def _inproj_kernel(xl_ref, xc_ref, sh_ref, sc_ref, nw_ref, w_ref, lr_ref, lb_ref,
                   k_ref, v_ref, u_ref, q_ref, r_ref, ga_ref, gm_ref, gf_ref, gb_ref, *, n_lat_tiles):
    x = _stream_tile(xl_ref, xc_ref, n_lat_tiles)
    h = _norm_mod(x, nw_ref[...], sh_ref[...], sc_ref[...]).astype(BF16)

    def sec(s):
        return _dot(h, w_ref[:, s[0]:s[1]])

    z = sec(SEC_Z).astype(BF16)
    pre = _dot(z, lr_ref[...]) + lb_ref[...]
    k_ref[...] = sec(SEC_K).astype(BF16)
    v_ref[...] = sec(SEC_V).astype(BF16)
    u_ref[...] = sec(SEC_U)
    q_ref[...] = (sec(SEC_Q) * GLA_DK ** -0.5).astype(BF16)
    r_ref[...] = _silu(sec(SEC_R)).astype(BF16)
    ga_ref[...] = jax.nn.sigmoid(sec(SEC_GA)).astype(BF16)
    gm_ref[...] = jax.nn.sigmoid(sec(SEC_GM)).astype(BF16)
    g = (jnp.minimum(pre, 0.0) - jnp.log(1.0 + jnp.exp(-jnp.abs(pre)))) * (1.0 / GLA_GATE_NORM)
    gf_ref[...] = g[:, :GLA_KEY]
    gb_ref[...] = g[:, GLA_KEY:]


def _mod_spec(col, tile_row):
    return pl.BlockSpec((None, 1, D_MODEL), lambda i: (tile_row(i), 0, col))


def _in_projection(stream, t, mod3, tile_row, tm, norm_w, w_perm, layer, lr, lb, side=None):
    x_lat, x_ctx, n_lat_tiles, ctx_off = stream
    widths = (GLA_KEY, GLA_VAL, S5_WIDTH, GLA_KEY, GLA_VAL, D_MODEL, D_MODEL)
    tok = lambda w: pl.BlockSpec((tm, w), lambda i: (i, 0))
    full = lambda a: pl.BlockSpec(a.shape, lambda i: (0,) * a.ndim)
    return _hosted_call(
        functools.partial(_inproj_kernel, n_lat_tiles=n_lat_tiles), "in_projection", t // tm,
        _stream_specs(tm, n_lat_tiles, ctx_off) + [_mod_spec(0, tile_row), _mod_spec(1, tile_row),
                                                   full(norm_w),
                                                   pl.BlockSpec((None,) + w_perm.shape[1:], lambda i: (layer, 0, 0)),
                                                   full(lr), full(lb)],
        [tok(w) for w in widths] + [tok(GLA_KEY), tok(GLA_KEY)],
        [jax.ShapeDtypeStruct((t, w), F32 if i == 2 else BF16) for i, w in enumerate(widths)]
        + [jax.ShapeDtypeStruct((t, GLA_KEY), F32)] * 2,
        (x_lat, x_ctx, mod3, mod3, norm_w, w_perm, lr, lb), side)


def _gla_kernel(*refs, resume):
    s_ref = refs[-1]
    ins = (refs[0:4], refs[4:8])
    if resume:
        s0_ref, outs = refs[8], refs[11:13]
    else:
        outs, s_out_ref = refs[8:10], refs[10]

    @pl.when(pl.program_id(1) == 0)
    def _():
        s_ref[...] = s0_ref[0] if resume else jnp.zeros_like(s_ref)

    n = GLA_BLOCK
    n_sub = refs[0].shape[0] // n
    nc = n // GLA_CHUNK
    assert nc * GLA_DK == n
    shift = GLA_CHUNK.bit_length() - 1
    row = lax.broadcasted_iota(jnp.int32, (n, n), 0)
    col = lax.broadcasted_iota(jnp.int32, (n, n), 1)
    same = (row >> shift) == (col >> shift)
    dirs = []
    for j in range(n_sub):
        for d, reverse in enumerate((False, True)):
            rows = pl.ds((n_sub - 1 - j if reverse else j) * n, n)
            dirs.append(tuple(r.at[rows, :] for r in ins[d]) + (outs[d].at[rows, :], s_ref.at[d], reverse))
    nd = range(len(dirs))
    heads = range(GLA_HEADS)
    ks = [slice(h * GLA_DK, (h + 1) * GLA_DK) for h in heads]
    vs = [slice(h * GLA_DV, (h + 1) * GLA_DV) for h in heads]

    tris, bcs = [], []
    for _, _, _, g_ref, _, _, reverse in dirs:
        tri = same & ((col >= row) if reverse else (col <= row))
        g = g_ref[...]
        g_hi = g.astype(BF16)
        g_lo = (g - g_hi.astype(F32)).astype(BF16)
        tri_b = tri.astype(BF16)
        tris.append(tri)
        bcs.append(_dot(tri_b, g_hi) + _dot(tri_b, g_lo))

    qds, kds, kls, decs, vals = [], [], [], [], []
    for (q_ref, k_ref, v_ref, _, _, _, reverse), bc in zip(dirs, bcs):
        last = 0 if reverse else GLA_CHUNK - 1
        bl3 = bc.reshape(nc, GLA_CHUNK, GLA_KEY)[:, last:last + 1, :]
        bl = jnp.broadcast_to(bl3, (nc, GLA_CHUNK, GLA_KEY)).reshape(n, GLA_KEY)
        q = q_ref[...].astype(F32)
        k = k_ref[...].astype(F32)
        qds.append((q * jnp.exp(bc)).astype(BF16))
        kds.append((k * jnp.exp(-bc)).astype(BF16))
        kls.append((k * jnp.exp(bl - bc)).astype(BF16))
        decs.append(jnp.exp(bl3))
        vals.append(v_ref[...])

    zero = jnp.zeros((), BF16)
    scores = [[_dot_nt(qds[d][:, ks[h]], kds[d][:, ks[h]]) for h in heads] for d in nd]
    incs = [[_dot_tn(vals[d][:, vs[h]], jnp.where(same, jnp.concatenate([kls[d][:, ks[h]]] * nc, axis=1), zero))
             for h in heads] for d in nd]
    intra = [[_dot(jnp.where(tris[d], scores[d][h], 0.0).astype(BF16), vals[d][:, vs[h]]) for h in heads]
             for d in nd]

    st_alls = []
    for d, (_, _, _, _, _, st_ref, reverse) in enumerate(dirs):
        order = list(range(nc))[::-1] if reverse else list(range(nc))
        st_alls.append([])
        for h in heads:
            st = st_ref[h]
            entering = {}
            for c in order:
                entering[c] = st
                st = st * decs[d][c][:, ks[h]] + incs[d][h][:, c * GLA_DK:(c + 1) * GLA_DK]
            st_ref[h] = st
            st_alls[d].append(jnp.concatenate([entering[c] for c in range(nc)], axis=1).astype(BF16))

    inter = [[_dot_nt(jnp.where(same, jnp.concatenate([qds[d][:, ks[h]]] * nc, axis=1), zero), st_alls[d][h])
              for h in heads] for d in nd]
    for d, (_, _, _, _, o_ref, _, _) in enumerate(dirs):
        o_ref[...] = jnp.concatenate([intra[d][h] + inter[d][h] for h in heads], axis=1).astype(o_ref.dtype)
    if not resume:
        s_out_ref[0] = s_ref[...]


def _gla(q, k, v, gf, gb, batch, n_lat, n_ctx):
    t = q.shape[0]
    state = (2, GLA_HEADS, GLA_DV, GLA_DK)
    widths = (GLA_KEY, GLA_KEY, GLA_VAL, GLA_KEY)
    out_shape = [jax.ShapeDtypeStruct((t, GLA_VAL), BF16)] * 2
    scratch = [pltpu.VMEM(state, F32)]
    state_spec = pl.BlockSpec((1,) + state, lambda b, i: (b, 0, 0, 0, 0))

    ctx0 = batch * n_lat // n_ctx
    ctx_map = lambda b, i: (ctx0 + b, 0)
    of, ob, states = pl.pallas_call(
        functools.partial(_gla_kernel, resume=False),
        grid=(batch, 1),
        in_specs=[pl.BlockSpec((n_ctx, w), ctx_map) for w in widths] * 2,
        out_specs=[pl.BlockSpec((n_ctx, GLA_VAL), ctx_map)] * 2 + [state_spec],
        out_shape=out_shape + [jax.ShapeDtypeStruct((batch,) + state, F32)],
        scratch_shapes=scratch,
        compiler_params=_params(2),
        name="gla_context",
    )(q, k, v, gf, q, k, v, gb)

    blk = math.gcd(GLA_LATENT_BLOCK, n_lat)
    nlb = n_lat // blk
    fwd = lambda b, i: (b * nlb + i, 0)
    bwd = lambda b, i: (b * nlb + nlb - 1 - i, 0)
    keep = pl.BlockSpec(memory_space=pl.ANY)
    return pl.pallas_call(
        functools.partial(_gla_kernel, resume=True),
        grid=(batch, nlb),
        in_specs=[pl.BlockSpec((blk, w), fwd) for w in widths] + [pl.BlockSpec((blk, w), bwd) for w in widths]
        + [state_spec, keep, keep],
        out_specs=[pl.BlockSpec((blk, GLA_VAL), fwd), pl.BlockSpec((blk, GLA_VAL), bwd)],
        out_shape=out_shape,
        input_output_aliases={9: 0, 10: 1},
        scratch_shapes=scratch,
        compiler_params=_params(2),
        name="gla_latent",
    )(q, k, v, gf, q, k, v, gb, states, of, ob)


def _s5_group_tables(arow, acol, log_dt, bt, ct, reverse):
    delta = jnp.exp(log_dt)
    lane_s = lax.broadcasted_iota(jnp.int32, (1, S5_SW), 1)
    sign = jnp.where(lane_s < S5_STATE, -1.0, 1.0)
    ar, ai = arow[0:1, :], arow[1:2, :]
    zr, zi = ar * delta, ai * delta

    def row_power(n):
        mag = jnp.exp(zr * n)
        return mag * jnp.cos(zi * n), mag * jnp.sin(zi * n)

    one = jnp.ones((1, 1), F32)
    l_r, l_i = row_power(one)
    den = ar * ar + ai * ai
    xr, xi = l_r - 1.0, l_i
    cr = (xr * ar + xi * ai) / den
    ci = (xi * ar - xr * ai) / den
    bdup = cr * bt + ci * (pltpu.roll(bt, S5_STATE, 1) * sign)
    bswp = pltpu.roll(bdup, S5_STATE, 1) * sign

    step = lax.broadcasted_iota(jnp.int32, (S5_CHUNK, 1), 0).astype(F32)
    er, ei = row_power(step if reverse else (S5_CHUNK - 1.0 - step))
    to_state = jnp.concatenate(
        [er[s:s + 1, :] * bdup + ei[s:s + 1, :] * bswp for s in range(S5_CHUNK)], axis=0)
    bdup_sw, bswp_sw = pltpu.roll(bdup, S5_STATE, 1), pltpu.roll(bswp, S5_STATE, 1)
    to_state_sw = jnp.concatenate(
        [er[s:s + 1, :] * bdup_sw + ei[s:s + 1, :] * bswp_sw for s in range(S5_CHUNK)], axis=0)

    zrc, zic = acol[:, 0:1] * delta, acol[:, 1:2] * delta
    lane_c = lax.broadcasted_iota(jnp.int32, (1, S5_CW), 1)
    src_row = S5_CHUNK - 1 - lax.broadcasted_iota(jnp.int32, (S5_CHUNK, 1), 0)
    spread = ((lane_c >> S5_GROUP_SHIFT) == src_row).astype(BF16)

    def to_columns(a):
        hi = a.astype(BF16)
        lo = (a - hi.astype(F32)).astype(BF16)
        return (_dot_tn(hi, spread) + _dot_tn(lo, spread))[:S5_STATE]

    pr, pi = to_columns(er), to_columns(ei)
    ctr, cti = ct[:S5_STATE, :], ct[S5_STATE:, :]
    kr, ki = ctr * pr - cti * pi, ctr * pi + cti * pr
    lam_mag = jnp.exp(zrc)
    lcr, lci = lam_mag * jnp.cos(zic), lam_mag * jnp.sin(zic)
    from_state = jnp.concatenate([kr * lcr - ki * lci, -(kr * lci + ki * lcr)], axis=0)
    krow = jnp.dot(bdup, jnp.concatenate([kr, -ki], axis=0), precision=HIGHEST,
                   preferred_element_type=F32)
    blocks = []
    for s in range(S5_CHUNK):
        if reverse:
            sh = S5_GROUP * (S5_CHUNK - 1 - s)
            blk = pltpu.roll(krow, (S5_CW - sh) % S5_CW, 1) if sh else krow
            blocks.append(jnp.where(lane_c < S5_GROUP * (s + 1), blk, 0.0))
        else:
            sh = S5_GROUP * s
            blk = pltpu.roll(krow, sh, 1) if sh else krow
            blocks.append(jnp.where(lane_c >= sh, blk, 0.0))
    toeplitz = jnp.concatenate(blocks, axis=0)

    wr, wi = row_power(one * float(S5_CHUNK))
    decay = []
    for _ in range(8):
        decay.append((wr, wi * sign))
        wr, wi = wr * wr - wi * wi, 2.0 * wr * wi
    to_state = jnp.concatenate([to_state, to_state_sw], axis=1)
    return toeplitz.astype(BF16), to_state.astype(BF16), from_state.astype(BF16), decay


def _cmul_pair(wa, wb, z):
    nb = z.shape[0] // 2
    partner = jnp.concatenate([z[nb:], z[:nb]], axis=0)
    wbs = jnp.concatenate([jnp.broadcast_to(wb, (nb, 1, S5_SW)), jnp.broadcast_to(-wb, (nb, 1, S5_SW))], axis=0)
    return wa * z + wbs * partner


def _s5_segment_scan(x, h0, decay, lvl0, scr, reverse):
    n = x.shape[1]
    lo = S5_PAD
    first, last = (n - 1, 0) if reverse else (0, n - 1)
    scr[...] = jnp.zeros_like(scr)
    scr[:, lo:lo + n, :] = x
    if h0 is not None:
        scr[:, lo + first:lo + first + 1, :] = x[:, first:first + 1, :] + _cmul_pair(*decay[lvl0], h0)
    off, lvl = 1, lvl0
    while off < n:
        src = lo + off if reverse else lo - off
        scr[:, lo:lo + n, :] = scr[:, lo:lo + n, :] + _cmul_pair(*decay[lvl], scr[:, src:src + n, :])
        off, lvl = off * 2, lvl + 1
    h_out = scr[:, lo + last:lo + last + 1, :]
    edge = lo + n if reverse else lo - 1
    if h0 is not None:
        scr[:, edge:edge + 1, :] = h0
    start = lo + 1 if reverse else lo - 1
    return scr[:, start:start + n, :], h_out


def _s5_direction(hloc, hloc_sw, decay, scr, reverse, nb, n_rq, n_cc):
    n_lat_rows = nb * n_rq * GRID_W
    pair = lambda f: jnp.concatenate([f(hloc), f(hloc_sw)], axis=0)
    lat = pair(lambda a: a[:n_lat_rows].reshape(nb, n_rq, GRID_W, S5_SW))
    ctx = pair(lambda a: a[n_lat_rows:].reshape(nb, n_cc, S5_SW))
    h_ctx_rows, h_ctx = _s5_segment_scan(ctx, None, decay, 0, scr, reverse)
    step = lambda h: _cmul_pair(*decay[0], h)
    order = list(range(n_rq))[::-1] if reverse else list(range(n_rq))
    col = lat[:, order[0]]
    for rq in order[1:]:
        col = step(col) + lat[:, rq]
    lvl_col = int(math.log2(n_rq))
    h_col, _ = _s5_segment_scan(col, h_ctx, decay, lvl_col, scr, reverse)
    starts = {order[0]: h_col}
    for prev, rq in zip(order[:-1], order[1:]):
        starts[rq] = step(starts[prev]) + lat[:, prev]
    lat_start = jnp.stack([starts[rq][:nb] for rq in range(n_rq)], axis=1)
    return jnp.concatenate([lat_start.reshape(n_lat_rows, S5_SW), h_ctx_rows[:nb].reshape(nb * n_cc, S5_SW)], axis=0)


def _s5_kernel(u_ref, d_ref, arf_ref, arb_ref, acf_ref, acb_ref, ldf_ref, ldb_ref, bt_ref, ct_ref,
               y_ref, xrot_ref, zacc_ref, scr, *, nb, n_lat, n_ctx):
    j = pl.program_id(0) % S5_TILE_GROUPS
    n_rq = n_lat // (GRID_W * S5_CHUNK)
    n_cc = n_ctx // S5_CHUNK
    lat_rows = n_rq * GRID_W
    n_rows = nb * (lat_rows + n_cc)
    lanes = S5_TILE_GROUPS * S5_GROUP
    slot = lax.broadcasted_iota(jnp.int32, (1, lanes), 1) >> S5_GROUP_SHIFT
    wrap = S5_TILE_GROUPS - 1

    runs = []
    for b in range(nb):
        for rq in range(n_rq):
            runs.append((lambda t, b=b, rq=rq: pl.ds(b * n_lat + (rq * S5_CHUNK + t) * GRID_W, GRID_W),
                         b * lat_rows + rq * GRID_W, GRID_W))
    for b in range(nb):
        runs.append((lambda t, b=b: pl.ds(nb * n_lat + b * n_ctx + t, n_cc, stride=S5_CHUNK),
                     nb * lat_rows + b * n_cc, n_cc))

    @pl.when(j == 0)
    def _():
        for tok_rows, row0, n in runs:
            for t in range(S5_CHUNK):
                x = pltpu.roll(u_ref[tok_rows(t), :], (t & wrap) * S5_GROUP, 1)
                xrot_ref[t, pl.ds(row0, n), :] = x.astype(BF16)

    halves = []
    for th in range(2):
        acc = xrot_ref[th * S5_TILE_GROUPS]
        for tp in range(1, S5_TILE_GROUPS):
            acc = jnp.where(slot == ((tp + j) & wrap), xrot_ref[th * S5_TILE_GROUPS + tp], acc)
        halves.append(pltpu.roll(acc.astype(F32), ((-j) & wrap) * S5_GROUP, 1))
    u = jnp.concatenate(halves, axis=1).astype(BF16)

    tabs = [_s5_group_tables(ar_ref[0], ac_ref[0], ld_ref[0], bt_ref[0], ct_ref[0], reverse)
            for reverse, ar_ref, ac_ref, ld_ref in ((False, arf_ref, acf_ref, ldf_ref), (True, arb_ref, acb_ref, ldb_ref))]
    hlocs = [_dot(u, tab[1]) for tab in tabs]
    y = _dot(u, tabs[0][0]) + _dot(u, tabs[1][0])
    hstarts = [_s5_direction(hlocs[d][:, :S5_SW], hlocs[d][:, S5_SW:], tabs[d][3], scr, d == 1, nb, n_rq, n_cc
                             ).astype(BF16) for d in range(2)]
    y = y + _dot(hstarts[0], tabs[0][2]) + _dot(hstarts[1], tabs[1][2])

    for th in range(2):
        rolled = pltpu.roll(y[:, th * lanes:(th + 1) * lanes], j * S5_GROUP, 1)
        for tp in range(S5_TILE_GROUPS):
            base = (th * S5_TILE_GROUPS + tp) * n_rows
            for r0 in range(0, n_rows, lanes):
                nr = min(lanes, n_rows - r0)
                own = jnp.broadcast_to(slot == ((tp + j) & wrap), (nr, lanes))
                pltpu.store(zacc_ref.at[pl.ds(base + r0, nr), :], rolled[r0:r0 + nr], mask=own)

    @pl.when(j == S5_TILE_GROUPS - 1)
    def _():
        for tok_rows, row0, n in runs:
            for t in range(S5_CHUNK):
                z = zacc_ref[pl.ds(t * n_rows + row0, n), :]
                rows = tok_rows(t)
                y_ref[rows, :] = pltpu.roll(z, ((-t) & wrap) * S5_GROUP, 1) + d_ref[...] * u_ref[rows, :]


def _s5(u, d_skip, params_f, params_b, bt, ct, nb, n_lat, n_ctx):
    t, width = u.shape
    lanes = S5_TILE_GROUPS * S5_GROUP
    n_rows = nb * (n_lat + n_ctx) // S5_CHUNK
    tile = pl.BlockSpec((t, lanes), lambda g: (0, g // S5_TILE_GROUPS))
    skip = pl.BlockSpec((1, lanes), lambda g: (0, g // S5_TILE_GROUPS))
    grp = lambda a: pl.BlockSpec((1,) + a.shape[1:], lambda g: (g, 0, 0))
    arf, acf, ldf = params_f
    arb, acb, ldb = params_b
    ops = (arf, arb, acf, acb, ldf, ldb, bt, ct)
    return pl.pallas_call(
        functools.partial(_s5_kernel, nb=nb, n_lat=n_lat, n_ctx=n_ctx),
        grid=(S5_GROUPS,),
        in_specs=[tile, skip] + [grp(a) for a in ops],
        out_specs=tile,
        out_shape=jax.ShapeDtypeStruct((t, width), F32),
        scratch_shapes=[pltpu.VMEM((S5_CHUNK, n_rows, lanes), BF16),
                        pltpu.VMEM((S5_CHUNK * n_rows, lanes), F32),
                        pltpu.VMEM((2 * nb, 2 * S5_PAD + GRID_W, S5_SW), F32)],
        compiler_params=_params(1),
        name="s5_bidir",
    )(u, d_skip, *ops)


def _s5_param_layout(a_re, a_im, log_dt):
    arow = jnp.stack([jnp.concatenate([a_re, a_re], axis=-1), jnp.concatenate([a_im, a_im], axis=-1)], axis=1)
    acol = jnp.stack([a_re, a_im], axis=-1)
    return arow, acol, log_dt[:, None, None]


def _s5_bc_layout(b_re, b_im, c_re, c_im):
    bt = jnp.concatenate([b_re.transpose(0, 2, 1), b_im.transpose(0, 2, 1)], axis=-1)
    ct = jnp.concatenate([c_re.transpose(0, 2, 1), c_im.transpose(0, 2, 1)], axis=1)
    return bt, jnp.tile(ct, (1, 1, S5_CHUNK))


def _gelu_tanh(x):
    return 0.5 * x * (1.0 + jnp.tanh(math.sqrt(2.0 / math.pi) * (x + 0.044715 * (x * x * x))))


def _merge_kernel(of_ref, ob_ref, y_ref, r_ref, ga_ref, gm_ref, xl_ref, xc_ref, g1_ref,
                  gnw_ref, wgp_ref, wglu_ref, bglu_ref, wsp_ref, wout_ref, o_ref, *, n_lat_tiles):
    o = of_ref[...].astype(F32) + ob_ref[...].astype(F32)
    heads = []
    for h in range(GLA_HEADS):
        oh = o[:, h * GLA_DV:(h + 1) * GLA_DV]
        heads.append(oh * lax.rsqrt(jnp.mean(oh * oh, axis=-1, keepdims=True) + EPS) * gnw_ref[...])
    og = jnp.concatenate(heads, axis=1) * r_ref[...].astype(F32)
    ya = _dot(og.astype(BF16), wgp_ref[...])
    s = _gelu_tanh(y_ref[...])
    s = s * jax.nn.sigmoid(_dot(s.astype(BF16), wglu_ref[...]) + bglu_ref[...])
    yb = _dot(s.astype(BF16), wsp_ref[...])
    m = ga_ref[...].astype(F32) * ya + gm_ref[...].astype(F32) * yb
    x = _stream_tile(xl_ref, xc_ref, n_lat_tiles)
    o_ref[...] = x + g1_ref[...] * _dot(m.astype(BF16), wout_ref[...])


def _merge(of, ob, y, r, ga, gm, stream, mod3, tile_row, tm, n_tiles, weights, side=None):
    x_lat, x_ctx, n_lat_tiles, ctx_off = stream
    d = D_MODEL
    tok = lambda w: pl.BlockSpec((tm, w), lambda i: (i, 0))
    full = lambda a: pl.BlockSpec(a.shape, lambda i: (0,) * a.ndim)
    (x_mid,), cast = _hosted_call(
        functools.partial(_merge_kernel, n_lat_tiles=n_lat_tiles), "merge", n_tiles,
        [tok(GLA_VAL), tok(GLA_VAL), tok(S5_WIDTH), tok(GLA_VAL), tok(d), tok(d)]
        + _stream_specs(tm, n_lat_tiles, ctx_off) + [_mod_spec(2, tile_row)] + [full(a) for a in weights],
        [tok(d)], [jax.ShapeDtypeStruct((n_tiles * tm, d), F32)],
        (of, ob, y, r, ga, gm, x_lat, x_ctx, mod3, *weights), side)
    return x_mid, cast


def _swiglu_hidden(h, wg, wu, width, scale=None):
    parts = []
    for c0 in range(0, width, MXU_DIM):
        cols = slice(c0, c0 + MXU_DIM)
        a = _silu(_dot(h, wg[:, cols])) * _dot(h, wu[:, cols])
        parts.append((a if scale is None else a * scale).astype(BF16))
    return jnp.concatenate(parts, axis=1)


def _ffn_kernel(x_ref, sh_ref, sc_ref, g2_ref, nw_ref, wg_ref, wu_ref, wd_ref, o_ref):
    x = x_ref[...]
    h = _norm_mod(x, nw_ref[...], sh_ref[...], sc_ref[...]).astype(BF16)
    a = _swiglu_hidden(h, wg_ref, wu_ref, wg_ref.shape[1])
    o_ref[...] = x + g2_ref[...] * _dot(a, wd_ref[...])


def _ffn(x_all, mod3, tile_row, tm, norm_w, wg, wu, wd, side=None):
    t, d = x_all.shape
    full = lambda a: pl.BlockSpec(a.shape, lambda i: (0,) * a.ndim)
    once = lambda a: pl.BlockSpec(a.shape, lambda i: (0,) * a.ndim, pipeline_mode=pl.Buffered(1))
    (out,), cast = _hosted_call(
        _ffn_kernel, "ffn_dense", t // tm,
        [pl.BlockSpec((tm, d), lambda i: (i, 0)),
         _mod_spec(3, tile_row), _mod_spec(4, tile_row), _mod_spec(5, tile_row),
         full(norm_w), once(wg), once(wu), once(wd)],
        [pl.BlockSpec((tm, d), lambda i: (i, 0))], [jax.ShapeDtypeStruct((t, d), F32)],
        (x_all, mod3, mod3, mod3, norm_w, wg, wu, wd), side)
    return out, cast


def _moe_kernel(x_ref, sh_ref, sc_ref, g2_ref, nw_ref, fw_ref, wr_ref, wg_ref, wu_ref, wd_ref,
                o_ref, h_ref, gate_ref, acc_ref):
    e = pl.program_id(1)
    f = pl.program_id(2)
    lane = lax.broadcasted_iota(jnp.int32, gate_ref.shape, 1).astype(F32)

    @pl.when((e == 0) & (f == 0))
    def _():
        hf = _norm_mod(x_ref[...], nw_ref[...], sh_ref[...], sc_ref[...])
        hb = hf.astype(BF16)
        h_ref[...] = hb
        acc_ref[...] = jnp.zeros_like(acc_ref)
        h_lo = (hf - hb.astype(F32)).astype(BF16)
        wr = wr_ref[...]
        w_hi = wr.astype(BF16)
        w_lo = (wr - w_hi.astype(F32)).astype(BF16)
        logits = _dot(hb, w_hi) + _dot(h_lo, w_hi) + _dot(hb, w_lo)
        logits = jnp.where(lane < N_EXPERTS, logits, -jnp.inf)
        m1 = jnp.max(logits, axis=-1, keepdims=True)
        i1 = jnp.min(jnp.where(logits == m1, lane, float(Z_PAD)), axis=-1, keepdims=True)
        rest = jnp.where(lane == i1, -jnp.inf, logits)
        m2 = jnp.max(rest, axis=-1, keepdims=True)
        i2 = jnp.min(jnp.where(rest == m2, lane, float(Z_PAD)), axis=-1, keepdims=True)
        e2 = jnp.exp(m2 - m1)
        w1 = 1.0 / (1.0 + e2)
        gate_ref[...] = jnp.where(lane == i1, w1, 0.0) + jnp.where(lane == i2, e2 * w1, 0.0)

    h = h_ref[...]
    ge = jnp.sum(jnp.where(lane == e.astype(F32), gate_ref[...], 0.0), axis=-1, keepdims=True)
    a = _swiglu_hidden(h, wg_ref.at[0], wu_ref.at[0], wg_ref.shape[2], ge)
    acc_ref[...] += _dot(a, wd_ref[0])

    @pl.when((e == pl.num_programs(1) - 1) & (f == pl.num_programs(2) - 1))
    def _():
        y = x_ref[...] + g2_ref[...] * acc_ref[...]
        o_ref[...] = y * lax.rsqrt(jnp.mean(y * y, axis=-1, keepdims=True) + EPS) * fw_ref[...]


def _moe(x_lat, mod3, tile_row, tm, norm_w, final_w, w_router, wg, wu, wd, tf):
    t, d = x_lat.shape
    n_e, _, f_dim = wg.shape
    mspec = lambda col: pl.BlockSpec((None, 1, D_MODEL), lambda i, e, j: (tile_row(i), 0, col))
    const = lambda a: pl.BlockSpec(a.shape, lambda i, e, j: (0,) * a.ndim)
    return pl.pallas_call(
        _moe_kernel,
        grid=(t // tm, n_e, f_dim // tf),
        in_specs=[pl.BlockSpec((tm, d), lambda i, e, j: (i, 0)),
                  mspec(3), mspec(4), mspec(5), const(norm_w), const(final_w), const(w_router),
                  pl.BlockSpec((1, d, tf), lambda i, e, j: (e, 0, j)),
                  pl.BlockSpec((1, d, tf), lambda i, e, j: (e, 0, j)),
                  pl.BlockSpec((1, tf, d), lambda i, e, j: (e, j, 0))],
        out_specs=pl.BlockSpec((tm, d), lambda i, e, j: (i, 0)),
        out_shape=jax.ShapeDtypeStruct((t, d), F32),
        scratch_shapes=[pltpu.VMEM((tm, d), BF16), pltpu.VMEM((tm, Z_PAD), F32), pltpu.VMEM((tm, d), F32)],
        compiler_params=_params(3),
        name="moe",
    )(x_lat, mod3, mod3, mod3, norm_w, final_w, w_router, wg, wu, wd)


def _w_in_kernel(w_ref, o_ref):
    w = w_ref[0]
    z0 = GLA_KEY + GLA_VAL
    z1 = z0 + 2 * GLA_RANK
    pad = jnp.zeros((w.shape[0], Z_PAD - 2 * GLA_RANK), w.dtype)
    o_ref[0] = jnp.concatenate([w[:, :z0], w[:, z1:], w[:, z0:z1], pad], axis=1).astype(BF16)


def _permute_w_in(w_in):
    depth, d, width = w_in.shape
    rows = 256
    return pl.pallas_call(
        _w_in_kernel,
        grid=(depth, d // rows),
        in_specs=[pl.BlockSpec((1, rows, width), lambda l, i: (l, i, 0))],
        out_specs=pl.BlockSpec((1, rows, IN_PERM_WIDTH), lambda l, i: (l, i, 0)),
        out_shape=jax.ShapeDtypeStruct((depth, d, IN_PERM_WIDTH), BF16),
        compiler_params=_params(2),
        name="w_in_layout",
    )(w_in)


def _low_rank_gate(lr_f, lr_b, bias_f, bias_b):
    lr = jnp.zeros((Z_PAD, 2 * GLA_KEY), F32)
    lr = lr.at[:GLA_RANK, :GLA_KEY].set(lr_f).at[GLA_RANK:2 * GLA_RANK, GLA_KEY:].set(lr_b)
    return lr.astype(BF16), jnp.concatenate([bias_f, bias_b])[None, :]


def kernel(x, c, ctx, c_ctx, w_mod, b_mod, norm1_w, norm2_w, final_norm_w, w_in, gla_lr_f, gla_lr_b, gla_bias_f, gla_bias_b, gla_norm_w, s5_a_re_f, s5_a_im_f, s5_log_dt_f, s5_a_re_b, s5_a_im_b, s5_log_dt_b, s5_b_re, s5_b_im, s5_c_re, s5_c_im, s5_d, s5_w_glu, s5_b_glu, w_gla_proj, w_s5_proj, w_out, ffn_w_gate, ffn_w_up, ffn_w_down, moe_router, moe_w_gate, moe_w_up, moe_w_down):
    batch, n_lat, d = x.shape
    n_ctx = ctx.shape[1]
    depth = w_mod.shape[0]
    assert d == D_MODEL and batch < 8 and depth == 2
    assert n_lat % n_ctx == 0 and n_ctx % GLA_CHUNK == 0 and (n_lat // GRID_W) % S5_CHUNK == 0
    t_lat, t_ctx = batch * n_lat, batch * n_ctx
    t_all = t_lat + t_ctx

    tm = math.gcd(512, t_ctx)
    assert n_lat % tm == 0
    lat_tiles_per_batch = n_lat // tm
    n_lat_tiles = t_lat // tm
    tile_row = lambda i: jnp.where(i < n_lat_tiles, i // lat_tiles_per_batch, batch)

    cond = jnp.zeros((8, d), F32).at[:batch].set(c).at[batch].set(c_ctx)
    mod = _modulation(cond, w_mod, b_mod)

    stream = (x.reshape(t_lat, d), ctx.reshape(t_ctx, d), n_lat_tiles, 0)
    out = None
    n_e, _, f_e = moe_w_gate.shape[1:]
    moe_f32 = {"gate": moe_w_gate[0].reshape(n_e * d, f_e), "up": moe_w_up[0].reshape(n_e * d, f_e),
               "down": moe_w_down[0].reshape(n_e * f_e, d)}
    moe_bf16 = {}
    w_in_perm = _permute_w_in(w_in)
    for l in range(depth):
        last = l == depth - 1
        mod3 = mod[l].reshape(8, 1, 6 * d)
        lr, lb = _low_rank_gate(gla_lr_f[l], gla_lr_b[l], gla_bias_f[l], gla_bias_b[l])
        hosted = ("gate", "up")[l]
        (k, v, u, q, r, ga, gm, gf, gb), moe_bf16[hosted] = _in_projection(
            stream, t_all, mod3, tile_row, tm, norm1_w[l][None, :], w_in_perm, l, lr, lb,
            moe_f32[hosted])

        of, ob = _gla(q, k, v, gf, gb, batch, n_lat, n_ctx)

        bt, ct = _s5_bc_layout(s5_b_re[l], s5_b_im[l], s5_c_re[l], s5_c_im[l])
        y = _s5(u, s5_d[l][None, :], _s5_param_layout(s5_a_re_f[l], s5_a_im_f[l], s5_log_dt_f[l]),
                _s5_param_layout(s5_a_re_b[l], s5_a_im_b[l], s5_log_dt_b[l]), bt, ct, batch, n_lat, n_ctx)

        merge_w = (gla_norm_w[l][None, :], w_gla_proj[l].astype(BF16),
                   s5_w_glu[l].astype(BF16), s5_b_glu[l][None, :], w_s5_proj[l].astype(BF16),
                   w_out[l].astype(BF16))
        n_tiles = n_lat_tiles if last else t_all // tm
        x_mid, _ = _merge(of, ob, y, r, ga, gm, stream, mod3, tile_row, tm, n_tiles, merge_w)

        if l % 2 == 0:
            i = l // 2
            x_all, moe_bf16["down"] = _ffn(
                x_mid, mod3, tile_row, tm, norm2_w[l][None, :], ffn_w_gate[i].astype(BF16),
                ffn_w_up[i].astype(BF16), ffn_w_down[i].astype(BF16), moe_f32["down"])
            stream = (x_all, x_all, n_lat_tiles, n_lat_tiles)
        else:
            i = l // 2
            tm_f = math.gcd(1024, n_lat)
            row_f = lambda j, tm_f=tm_f: j // (n_lat // tm_f)
            w_router = jnp.zeros((d, Z_PAD), F32).at[:, :N_EXPERTS].set(moe_router[i])
            res = _moe(x_mid, mod3, row_f, tm_f, norm2_w[l][None, :], final_norm_w[None, :], w_router,
                       moe_bf16["gate"].reshape(n_e, d, f_e), moe_bf16["up"].reshape(n_e, d, f_e),
                       moe_bf16["down"].reshape(n_e, f_e, d), MOE_F_TILE)
            out = res.reshape(batch, n_lat, d)
    return out
```

```python
import functools
import math

import jax
import jax.numpy as jnp
from jax import lax
from jax.experimental import pallas as pl
from jax.experimental.pallas import tpu as pltpu

F32 = jnp.float32
BF16 = jnp.bfloat16
HIGHEST = lax.Precision.HIGHEST

D_MODEL = 1024
GRID_W = 64
GLA_HEADS = 4
GLA_DK = 64
GLA_DV = 128
GLA_KEY = GLA_HEADS * GLA_DK
GLA_VAL = GLA_HEADS * GLA_DV
GLA_RANK = 16
GLA_GATE_NORM = 16.0
GLA_CHUNK = 64
GLA_BLOCK = 256
GLA_LATENT_BLOCK = 1024
S5_WIDTH = 512
S5_GROUP = 16
S5_GROUPS = S5_WIDTH // S5_GROUP
S5_STATE = 64
S5_CHUNK = 16
S5_CW = S5_CHUNK * S5_GROUP
S5_SW = 2 * S5_STATE
S5_PAD = GRID_W // 2
S5_TILE_GROUPS = 128 // S5_GROUP
S5_GROUP_SHIFT = 4
N_EXPERTS = 8
EPS = 1e-6
MXU_DIM = 256
MOE_F_TILE = 7 * MXU_DIM
SIDE_CAST_BLOCKS = 32
Z_PAD = 128

VMEM_LIMIT = 56 * 1024 * 1024

SEC_K = (0, GLA_KEY)
SEC_V = (SEC_K[1], SEC_K[1] + GLA_VAL)
SEC_U = (SEC_V[1], SEC_V[1] + S5_WIDTH)
SEC_Q = (SEC_U[1], SEC_U[1] + GLA_KEY)
SEC_R = (SEC_Q[1], SEC_Q[1] + GLA_VAL)
SEC_GA = (SEC_R[1], SEC_R[1] + D_MODEL)
SEC_GM = (SEC_GA[1], SEC_GA[1] + D_MODEL)
SEC_Z = (SEC_GM[1], SEC_GM[1] + Z_PAD)
IN_PERM_WIDTH = SEC_Z[1]


def _params(n_axes):
    return pltpu.CompilerParams(dimension_semantics=("arbitrary",) * n_axes,
                                vmem_limit_bytes=VMEM_LIMIT)


def _silu(x):
    return x * jax.nn.sigmoid(x)


def _norm_mod(x, nw, sh, sc):
    y = x * lax.rsqrt(jnp.mean(x * x, axis=-1, keepdims=True) + EPS) * nw
    return y * (1.0 + sc) + sh


def _dot(a, b):
    return jnp.dot(a, b, preferred_element_type=F32)


def _dot_nt(a, b):
    return lax.dot_general(a, b, (((1,), (1,)), ((), ())), preferred_element_type=F32)


def _dot_tn(a, b):
    return lax.dot_general(a, b, (((0,), (0,)), ((), ())), preferred_element_type=F32)


def _mod_kernel(c_ref, w_ref, b_ref, o_ref):
    s = _silu(c_ref[...]).astype(BF16)
    o_ref[0] = _dot(s, w_ref[0].astype(BF16)) + b_ref[0]


def _modulation(cond, w_mod, b_mod):
    depth, d, n = w_mod.shape
    tn = 1536
    return pl.pallas_call(
        _mod_kernel,
        grid=(depth, n // tn),
        in_specs=[pl.BlockSpec((8, d), lambda l, j: (0, 0)),
                  pl.BlockSpec((1, d, tn), lambda l, j: (l, 0, j)),
                  pl.BlockSpec((1, 1, tn), lambda l, j: (l, 0, j))],
        out_specs=pl.BlockSpec((1, 8, tn), lambda l, j: (l, 0, j)),
        out_shape=jax.ShapeDtypeStruct((depth, 8, n), F32),
        compiler_params=_params(2),
        name="modulation",
    )(cond, w_mod, b_mod.reshape(depth, 1, n))


def _hosted_call(body, name, n_steps, in_specs, out_specs, out_shape, operands, sides=()):
    specs = []
    for side in sides:
        rows, cols = side.shape
        n_blk = 1 << (min(n_steps, SIDE_CAST_BLOCKS).bit_length() - 1)
        while rows % (16 * n_blk):
            n_blk //= 2
        specs.append(pl.BlockSpec((rows // n_blk, cols), lambda i, n_blk=n_blk: (jnp.minimum(i, n_blk - 1), 0)))
    n_in, n_side = len(in_specs), len(sides)

    def kernel(*refs):
        body(*refs[:n_in], *refs[n_in + n_side:len(refs) - n_side])
        for src, dst in zip(refs[n_in:n_in + n_side], refs[len(refs) - n_side:]):
            dst[...] = src[...].astype(BF16)

    res = pl.pallas_call(
        kernel, grid=(n_steps,), in_specs=list(in_specs) + specs, out_specs=list(out_specs) + specs,
        out_shape=list(out_shape) + [jax.ShapeDtypeStruct(s.shape, BF16) for s in sides],
        compiler_params=_params(1), name=name)(*operands, *sides)
    return res[:len(out_specs)], res[len(out_specs):]


def _stream_specs(tm, n_lat_tiles, ctx_off):
    lat = pl.BlockSpec((tm, D_MODEL), lambda i: (jnp.minimum(i, n_lat_tiles - 1), 0))
    ctx = pl.BlockSpec((tm, D_MODEL), lambda i: (jnp.maximum(i - n_lat_tiles, 0) + ctx_off, 0))
    return [lat, ctx]


def _stream_tile(lat_ref, ctx_ref, n_lat_tiles):
    return jnp.where(pl.program_id(0) < n_lat_tiles, lat_ref[...], ctx_ref[...])


def _inproj_kernel(xl_ref, xc_ref, sh_ref, sc_ref, nw_ref, w_ref, lr_ref, lb_ref,
                   k_ref, v_ref, u_ref, q_ref, r_ref, ga_ref, gm_ref, gf_ref, gb_ref, *, n_lat_tiles):
    x = _stream_tile(xl_ref, xc_ref, n_lat_tiles)
    h = _norm_mod(x, nw_ref[...], sh_ref[...], sc_ref[...]).astype(BF16)

    def sec(s):
        return _dot(h, w_ref[:, s[0]:s[1]])

    z = sec(SEC_Z).astype(BF16)
    pre = _dot(z, lr_ref[...]) + lb_ref[...]
    k_ref[...] = sec(SEC_K).astype(BF16)
    v_ref[...] = sec(SEC_V).astype(BF16)
    u_ref[...] = sec(SEC_U)
    q_ref[...] = (sec(SEC_Q) * GLA_DK ** -0.5).astype(BF16)
    r_ref[...] = _silu(sec(SEC_R)).astype(BF16)
    ga_ref[...] = jax.nn.sigmoid(sec(SEC_GA)).astype(BF16)
    gm_ref[...] = jax.nn.sigmoid(sec(SEC_GM)).astype(BF16)
    g = (jnp.minimum(pre, 0.0) - jnp.log(1.0 + jnp.exp(-jnp.abs(pre)))) * (1.0 / GLA_GATE_NORM)
    gf_ref[...] = g[:, :GLA_KEY]
    gb_ref[...] = g[:, GLA_KEY:]


def _mod_spec(col, tile_row):
    return pl.BlockSpec((None, 1, D_MODEL), lambda i: (tile_row(i), 0, col))


def _in_projection(stream, t, mod3, tile_row, tm, norm_w, w_perm, layer, lr, lb, sides=()):
    x_lat, x_ctx, n_lat_tiles, ctx_off = stream
    widths = (GLA_KEY, GLA_VAL, S5_WIDTH, GLA_KEY, GLA_VAL, D_MODEL, D_MODEL)
    tok = lambda w: pl.BlockSpec((tm, w), lambda i: (i, 0))
    full = lambda a: pl.BlockSpec(a.shape, lambda i: (0,) * a.ndim)
    return _hosted_call(
        functools.partial(_inproj_kernel, n_lat_tiles=n_lat_tiles), "in_projection", t // tm,
        _stream_specs(tm, n_lat_tiles, ctx_off) + [_mod_spec(0, tile_row), _mod_spec(1, tile_row),
                                                   full(norm_w),
                                                   pl.BlockSpec((None,) + w_perm.shape[1:], lambda i: (layer, 0, 0)),
                                                   full(lr), full(lb)],
        [tok(w) for w in widths] + [tok(GLA_KEY), tok(GLA_KEY)],
        [jax.ShapeDtypeStruct((t, w), F32 if i == 2 else BF16) for i, w in enumerate(widths)]
        + [jax.ShapeDtypeStruct((t, GLA_KEY), F32)] * 2,
        (x_lat, x_ctx, mod3, mod3, norm_w, w_perm, lr, lb), sides)


def _gla_kernel(*refs, resume):
    s_ref = refs[-1]
    ins = (refs[0:4], refs[4:8])
    if resume:
        s0_ref, outs = refs[8], refs[11:13]
    else:
        outs, s_out_ref = refs[8:10], refs[10]

    @pl.when(pl.program_id(1) == 0)
    def _():
        s_ref[...] = s0_ref[0] if resume else jnp.zeros_like(s_ref)

    n = GLA_BLOCK
    n_sub = refs[0].shape[0] // n
    nc = n // GLA_CHUNK
    assert nc * GLA_DK == n
    shift = GLA_CHUNK.bit_length() - 1
    row = lax.broadcasted_iota(jnp.int32, (n, n), 0)
    col = lax.broadcasted_iota(jnp.int32, (n, n), 1)
    same = (row >> shift) == (col >> shift)
    dirs = []
    for j in range(n_sub):
        for d, reverse in enumerate((False, True)):
            rows = pl.ds((n_sub - 1 - j if reverse else j) * n, n)
            dirs.append(tuple(r.at[rows, :] for r in ins[d]) + (outs[d].at[rows, :], s_ref.at[d], reverse))
    nd = range(len(dirs))
    heads = range(GLA_HEADS)
    ks = [slice(h * GLA_DK, (h + 1) * GLA_DK) for h in heads]
    vs = [slice(h * GLA_DV, (h + 1) * GLA_DV) for h in heads]

    tris, bcs = [], []
    for _, _, _, g_ref, _, _, reverse in dirs:
        tri = same & ((col >= row) if reverse else (col <= row))
        g = g_ref[...]
        g_hi = g.astype(BF16)
        g_lo = (g - g_hi.astype(F32)).astype(BF16)
        tri_b = tri.astype(BF16)
        tris.append(tri)
        bcs.append(_dot(tri_b, g_hi) + _dot(tri_b, g_lo))

    qds, kds, kls, decs, vals = [], [], [], [], []
    for (q_ref, k_ref, v_ref, _, _, _, reverse), bc in zip(dirs, bcs):
        last = 0 if reverse else GLA_CHUNK - 1
        bl3 = bc.reshape(nc, GLA_CHUNK, GLA_KEY)[:, last:last + 1, :]
        bl = jnp.broadcast_to(bl3, (nc, GLA_CHUNK, GLA_KEY)).reshape(n, GLA_KEY)
        q = q_ref[...].astype(F32)
        k = k_ref[...].astype(F32)
        qds.append((q * jnp.exp(bc)).astype(BF16))
        kds.append((k * jnp.exp(-bc)).astype(BF16))
        kls.append((k * jnp.exp(bl - bc)).astype(BF16))
        decs.append(jnp.exp(bl3))
        vals.append(v_ref[...])

    zero = jnp.zeros((), BF16)
    scores = [[_dot_nt(qds[d][:, ks[h]], kds[d][:, ks[h]]) for h in heads] for d in nd]
    incs = [[_dot_tn(vals[d][:, vs[h]], jnp.where(same, jnp.concatenate([kls[d][:, ks[h]]] * nc, axis=1), zero))
             for h in heads] for d in nd]
    intra = [[_dot(jnp.where(tris[d], scores[d][h], 0.0).astype(BF16), vals[d][:, vs[h]]) for h in heads]
             for d in nd]

    st_alls = []
    for d, (_, _, _, _, _, st_ref, reverse) in enumerate(dirs):
        order = list(range(nc))[::-1] if reverse else list(range(nc))
        st_alls.append([])
        for h in heads:
            st = st_ref[h]
            entering = {}
            for c in order:
                entering[c] = st
                st = st * decs[d][c][:, ks[h]] + incs[d][h][:, c * GLA_DK:(c + 1) * GLA_DK]
            st_ref[h] = st
            st_alls[d].append(jnp.concatenate([entering[c] for c in range(nc)], axis=1).astype(BF16))

    inter = [[_dot_nt(jnp.where(same, jnp.concatenate([qds[d][:, ks[h]]] * nc, axis=1), zero), st_alls[d][h])
              for h in heads] for d in nd]
    for d, (_, _, _, _, o_ref, _, _) in enumerate(dirs):
        o_ref[...] = jnp.concatenate([intra[d][h] + inter[d][h] for h in heads], axis=1).astype(o_ref.dtype)
    if not resume:
        s_out_ref[0] = s_ref[...]


def _gla(q, k, v, gf, gb, batch, n_lat, n_ctx):
    t = q.shape[0]
    state = (2, GLA_HEADS, GLA_DV, GLA_DK)
    widths = (GLA_KEY, GLA_KEY, GLA_VAL, GLA_KEY)
    out_shape = [jax.ShapeDtypeStruct((t, GLA_VAL), BF16)] * 2
    scratch = [pltpu.VMEM(state, F32)]
    state_spec = pl.BlockSpec((1,) + state, lambda b, i: (b, 0, 0, 0, 0))

    ctx0 = batch * n_lat // n_ctx
    ctx_map = lambda b, i: (ctx0 + b, 0)
    of, ob, states = pl.pallas_call(
        functools.partial(_gla_kernel, resume=False),
        grid=(batch, 1),
        in_specs=[pl.BlockSpec((n_ctx, w), ctx_map) for w in widths] * 2,
        out_specs=[pl.BlockSpec((n_ctx, GLA_VAL), ctx_map)] * 2 + [state_spec],
        out_shape=out_shape + [jax.ShapeDtypeStruct((batch,) + state, F32)],
        scratch_shapes=scratch,
        compiler_params=_params(2),
        name="gla_context",
    )(q, k, v, gf, q, k, v, gb)

    blk = math.gcd(GLA_LATENT_BLOCK, n_lat)
    nlb = n_lat // blk
    fwd = lambda b, i: (b * nlb + i, 0)
    bwd = lambda b, i: (b * nlb + nlb - 1 - i, 0)
    keep = pl.BlockSpec(memory_space=pl.ANY)
    return pl.pallas_call(
        functools.partial(_gla_kernel, resume=True),
        grid=(batch, nlb),
        in_specs=[pl.BlockSpec((blk, w), fwd) for w in widths] + [pl.BlockSpec((blk, w), bwd) for w in widths]
        + [state_spec, keep, keep],
        out_specs=[pl.BlockSpec((blk, GLA_VAL), fwd), pl.BlockSpec((blk, GLA_VAL), bwd)],
        out_shape=out_shape,
        input_output_aliases={9: 0, 10: 1},
        scratch_shapes=scratch,
        compiler_params=_params(2),
        name="gla_latent",
    )(q, k, v, gf, q, k, v, gb, states, of, ob)


def _s5_group_tables(arow, acol, log_dt, bt, ct, reverse):
    delta = jnp.exp(log_dt)
    lane_s = lax.broadcasted_iota(jnp.int32, (1, S5_SW), 1)
    sign = jnp.where(lane_s < S5_STATE, -1.0, 1.0)
    ar, ai = arow[0:1, :], arow[1:2, :]
    zr, zi = ar * delta, ai * delta

    def row_power(n):
        mag = jnp.exp(zr * n)
        return mag * jnp.cos(zi * n), mag * jnp.sin(zi * n)

    one = jnp.ones((1, 1), F32)
    l_r, l_i = row_power(one)
    den = ar * ar + ai * ai
    xr, xi = l_r - 1.0, l_i
    cr = (xr * ar + xi * ai) / den
    ci = (xi * ar - xr * ai) / den
    bdup = cr * bt + ci * (pltpu.roll(bt, S5_STATE, 1) * sign)
    bswp = pltpu.roll(bdup, S5_STATE, 1) * sign

    step = lax.broadcasted_iota(jnp.int32, (S5_CHUNK, 1), 0).astype(F32)
    er, ei = row_power(step if reverse else (S5_CHUNK - 1.0 - step))
    to_state = jnp.concatenate(
        [er[s:s + 1, :] * bdup + ei[s:s + 1, :] * bswp for s in range(S5_CHUNK)], axis=0)
    bdup_sw, bswp_sw = pltpu.roll(bdup, S5_STATE, 1), pltpu.roll(bswp, S5_STATE, 1)
    to_state_sw = jnp.concatenate(
        [er[s:s + 1, :] * bdup_sw + ei[s:s + 1, :] * bswp_sw for s in range(S5_CHUNK)], axis=0)

    zrc, zic = acol[:, 0:1] * delta, acol[:, 1:2] * delta
    lane_c = lax.broadcasted_iota(jnp.int32, (1, S5_CW), 1)
    src_row = S5_CHUNK - 1 - lax.broadcasted_iota(jnp.int32, (S5_CHUNK, 1), 0)
    spread = ((lane_c >> S5_GROUP_SHIFT) == src_row).astype(BF16)

    def to_columns(a):
        hi = a.astype(BF16)
        lo = (a - hi.astype(F32)).astype(BF16)
        return (_dot_tn(hi, spread) + _dot_tn(lo, spread))[:S5_STATE]

    pr, pi = to_columns(er), to_columns(ei)
    ctr, cti = ct[:S5_STATE, :], ct[S5_STATE:, :]
    kr, ki = ctr * pr - cti * pi, ctr * pi + cti * pr
    lam_mag = jnp.exp(zrc)
    lcr, lci = lam_mag * jnp.cos(zic), lam_mag * jnp.sin(zic)
    from_state = jnp.concatenate([kr * lcr - ki * lci, -(kr * lci + ki * lcr)], axis=0)
    krow = jnp.dot(bdup, jnp.concatenate([kr, -ki], axis=0), precision=HIGHEST,
                   preferred_element_type=F32)
    blocks = []
    for s in range(S5_CHUNK):
        if reverse:
            sh = S5_GROUP * (S5_CHUNK - 1 - s)
            blk = pltpu.roll(krow, (S5_CW - sh) % S5_CW, 1) if sh else krow
            blocks.append(jnp.where(lane_c < S5_GROUP * (s + 1), blk, 0.0))
        else:
            sh = S5_GROUP * s
            blk = pltpu.roll(krow, sh, 1) if sh else krow
            blocks.append(jnp.where(lane_c >= sh, blk, 0.0))
    toeplitz = jnp.concatenate(blocks, axis=0)

    wr, wi = row_power(one * float(S5_CHUNK))
    decay = []
    for _ in range(8):
        decay.append((wr, wi * sign))
        wr, wi = wr * wr - wi * wi, 2.0 * wr * wi
    to_state = jnp.concatenate([to_state, to_state_sw], axis=1)
    return toeplitz.astype(BF16), to_state.astype(BF16), from_state.astype(BF16), decay


def _cmul_pair(wa, wb, z):
    nb = z.shape[0] // 2
    partner = jnp.concatenate([z[nb:], z[:nb]], axis=0)
    wbs = jnp.concatenate([jnp.broadcast_to(wb, (nb, 1, S5_SW)), jnp.broadcast_to(-wb, (nb, 1, S5_SW))], axis=0)
    return wa * z + wbs * partner


def _s5_segment_scan(x, h0, decay, lvl0, scr, reverse):
    n = x.shape[1]
    lo = S5_PAD
    first, last = (n - 1, 0) if reverse else (0, n - 1)
    scr[...] = jnp.zeros_like(scr)
    scr[:, lo:lo + n, :] = x
    if h0 is not None:
        scr[:, lo + first:lo + first + 1, :] = x[:, first:first + 1, :] + _cmul_pair(*decay[lvl0], h0)
    off, lvl = 1, lvl0
    while off < n:
        src = lo + off if reverse else lo - off
        scr[:, lo:lo + n, :] = scr[:, lo:lo + n, :] + _cmul_pair(*decay[lvl], scr[:, src:src + n, :])
        off, lvl = off * 2, lvl + 1
    h_out = scr[:, lo + last:lo + last + 1, :]
    edge = lo + n if reverse else lo - 1
    if h0 is not None:
        scr[:, edge:edge + 1, :] = h0
    start = lo + 1 if reverse else lo - 1
    return scr[:, start:start + n, :], h_out


def _s5_direction(hloc, hloc_sw, decay, scr, reverse, nb, n_rq, n_cc):
    n_lat_rows = nb * n_rq * GRID_W
    pair = lambda f: jnp.concatenate([f(hloc), f(hloc_sw)], axis=0)
    lat = pair(lambda a: a[:n_lat_rows].reshape(nb, n_rq, GRID_W, S5_SW))
    ctx = pair(lambda a: a[n_lat_rows:].reshape(nb, n_cc, S5_SW))
    h_ctx_rows, h_ctx = _s5_segment_scan(ctx, None, decay, 0, scr, reverse)
    step = lambda h: _cmul_pair(*decay[0], h)
    order = list(range(n_rq))[::-1] if reverse else list(range(n_rq))
    col = lat[:, order[0]]
    for rq in order[1:]:
        col = step(col) + lat[:, rq]
    lvl_col = int(math.log2(n_rq))
    h_col, _ = _s5_segment_scan(col, h_ctx, decay, lvl_col, scr, reverse)
    starts = {order[0]: h_col}
    for prev, rq in zip(order[:-1], order[1:]):
        starts[rq] = step(starts[prev]) + lat[:, prev]
    lat_start = jnp.stack([starts[rq][:nb] for rq in range(n_rq)], axis=1)
    return jnp.concatenate([lat_start.reshape(n_lat_rows, S5_SW), h_ctx_rows[:nb].reshape(nb * n_cc, S5_SW)], axis=0)


def _s5_kernel(u_ref, d_ref, arf_ref, arb_ref, acf_ref, acb_ref, ldf_ref, ldb_ref, bt_ref, ct_ref,
               y_ref, xrot_ref, zacc_ref, scr, *, nb, n_lat, n_ctx):
    j = pl.program_id(0) % S5_TILE_GROUPS
    n_rq = n_lat // (GRID_W * S5_CHUNK)
    n_cc = n_ctx // S5_CHUNK
    lat_rows = n_rq * GRID_W
    n_rows = nb * (lat_rows + n_cc)
    lanes = S5_TILE_GROUPS * S5_GROUP
    slot = lax.broadcasted_iota(jnp.int32, (1, lanes), 1) >> S5_GROUP_SHIFT
    wrap = S5_TILE_GROUPS - 1

    runs = []
    for b in range(nb):
        for rq in range(n_rq):
            runs.append((lambda t, b=b, rq=rq: pl.ds(b * n_lat + (rq * S5_CHUNK + t) * GRID_W, GRID_W),
                         b * lat_rows + rq * GRID_W, GRID_W))
    for b in range(nb):
        runs.append((lambda t, b=b: pl.ds(nb * n_lat + b * n_ctx + t, n_cc, stride=S5_CHUNK),
                     nb * lat_rows + b * n_cc, n_cc))

    @pl.when(j == 0)
    def _():
        for tok_rows, row0, n in runs:
            for t in range(S5_CHUNK):
                x = pltpu.roll(u_ref[tok_rows(t), :], (t & wrap) * S5_GROUP, 1)
                xrot_ref[t, pl.ds(row0, n), :] = x.astype(BF16)

    halves = []
    for th in range(2):
        acc = xrot_ref[th * S5_TILE_GROUPS]
        for tp in range(1, S5_TILE_GROUPS):
            acc = jnp.where(slot == ((tp + j) & wrap), xrot_ref[th * S5_TILE_GROUPS + tp], acc)
        halves.append(pltpu.roll(acc.astype(F32), ((-j) & wrap) * S5_GROUP, 1))
    u = jnp.concatenate(halves, axis=1).astype(BF16)

    tabs = [_s5_group_tables(ar_ref[0], ac_ref[0], ld_ref[0], bt_ref[0], ct_ref[0], reverse)
            for reverse, ar_ref, ac_ref, ld_ref in ((False, arf_ref, acf_ref, ldf_ref), (True, arb_ref, acb_ref, ldb_ref))]
    hlocs = [_dot(u, tab[1]) for tab in tabs]
    y = _dot(u, tabs[0][0]) + _dot(u, tabs[1][0])
    hstarts = [_s5_direction(hlocs[d][:, :S5_SW], hlocs[d][:, S5_SW:], tabs[d][3], scr, d == 1, nb, n_rq, n_cc
                             ).astype(BF16) for d in range(2)]
    y = y + _dot(hstarts[0], tabs[0][2]) + _dot(hstarts[1], tabs[1][2])

    for th in range(2):
        rolled = pltpu.roll(y[:, th * lanes:(th + 1) * lanes], j * S5_GROUP, 1)
        for tp in range(S5_TILE_GROUPS):
            base = (th * S5_TILE_GROUPS + tp) * n_rows
            for r0 in range(0, n_rows, lanes):
                nr = min(lanes, n_rows - r0)
                own = jnp.broadcast_to(slot == ((tp + j) & wrap), (nr, lanes))
                pltpu.store(zacc_ref.at[pl.ds(base + r0, nr), :], rolled[r0:r0 + nr], mask=own)

    @pl.when(j == S5_TILE_GROUPS - 1)
    def _():
        for tok_rows, row0, n in runs:
            for t in range(S5_CHUNK):
                z = zacc_ref[pl.ds(t * n_rows + row0, n), :]
                rows = tok_rows(t)
                y_ref[rows, :] = pltpu.roll(z, ((-t) & wrap) * S5_GROUP, 1) + d_ref[...] * u_ref[rows, :]


def _s5(u, d_skip, params_f, params_b, bt, ct, nb, n_lat, n_ctx):
    t, width = u.shape
    lanes = S5_TILE_GROUPS * S5_GROUP
    n_rows = nb * (n_lat + n_ctx) // S5_CHUNK
    tile = pl.BlockSpec((t, lanes), lambda g: (0, g // S5_TILE_GROUPS))
    skip = pl.BlockSpec((1, lanes), lambda g: (0, g // S5_TILE_GROUPS))
    grp = lambda a: pl.BlockSpec((1,) + a.shape[1:], lambda g: (g, 0, 0))
    arf, acf, ldf = params_f
    arb, acb, ldb = params_b
    ops = (arf, arb, acf, acb, ldf, ldb, bt, ct)
    return pl.pallas_call(
        functools.partial(_s5_kernel, nb=nb, n_lat=n_lat, n_ctx=n_ctx),
        grid=(S5_GROUPS,),
        in_specs=[tile, skip] + [grp(a) for a in ops],
        out_specs=tile,
        out_shape=jax.ShapeDtypeStruct((t, width), F32),
        scratch_shapes=[pltpu.VMEM((S5_CHUNK, n_rows, lanes), BF16),
                        pltpu.VMEM((S5_CHUNK * n_rows, lanes), F32),
                        pltpu.VMEM((2 * nb, 2 * S5_PAD + GRID_W, S5_SW), F32)],
        compiler_params=_params(1),
        name="s5_bidir",
    )(u, d_skip, *ops)


def _s5_param_layout(a_re, a_im, log_dt):
    arow = jnp.stack([jnp.concatenate([a_re, a_re], axis=-1), jnp.concatenate([a_im, a_im], axis=-1)], axis=1)
    acol = jnp.stack([a_re, a_im], axis=-1)
    return arow, acol, log_dt[:, None, None]


def _s5_bc_layout(b_re, b_im, c_re, c_im):
    bt = jnp.concatenate([b_re.transpose(0, 2, 1), b_im.transpose(0, 2, 1)], axis=-1)
    ct = jnp.concatenate([c_re.transpose(0, 2, 1), c_im.transpose(0, 2, 1)], axis=1)
    return bt, jnp.tile(ct, (1, 1, S5_CHUNK))


def _gelu_tanh(x):
    return 0.5 * x * (1.0 + jnp.tanh(math.sqrt(2.0 / math.pi) * (x + 0.044715 * (x * x * x))))


def _merge_kernel(of_ref, ob_ref, y_ref, r_ref, ga_ref, gm_ref, xl_ref, xc_ref, g1_ref,
                  gnw_ref, wgp_ref, wglu_ref, bglu_ref, wsp_ref, wout_ref, o_ref, *, n_lat_tiles):
    o = of_ref[...].astype(F32) + ob_ref[...].astype(F32)
    heads = []
    for h in range(GLA_HEADS):
        oh = o[:, h * GLA_DV:(h + 1) * GLA_DV]
        heads.append(oh * lax.rsqrt(jnp.mean(oh * oh, axis=-1, keepdims=True) + EPS) * gnw_ref[...])
    og = jnp.concatenate(heads, axis=1) * r_ref[...].astype(F32)
    ya = _dot(og.astype(BF16), wgp_ref[...])
    s = _gelu_tanh(y_ref[...])
    s = s * jax.nn.sigmoid(_dot(s.astype(BF16), wglu_ref[...]) + bglu_ref[...])
    yb = _dot(s.astype(BF16), wsp_ref[...])
    m = ga_ref[...].astype(F32) * ya + gm_ref[...].astype(F32) * yb
    x = _stream_tile(xl_ref, xc_ref, n_lat_tiles)
    o_ref[...] = x + g1_ref[...] * _dot(m.astype(BF16), wout_ref[...])


def _merge(of, ob, y, r, ga, gm, stream, mod3, tile_row, tm, n_tiles, weights):
    x_lat, x_ctx, n_lat_tiles, ctx_off = stream
    d = D_MODEL
    tok = lambda w: pl.BlockSpec((tm, w), lambda i: (i, 0))
    full = lambda a: pl.BlockSpec(a.shape, lambda i: (0,) * a.ndim)
    (x_mid,), _ = _hosted_call(
        functools.partial(_merge_kernel, n_lat_tiles=n_lat_tiles), "merge", n_tiles,
        [tok(GLA_VAL), tok(GLA_VAL), tok(S5_WIDTH), tok(GLA_VAL), tok(d), tok(d)]
        + _stream_specs(tm, n_lat_tiles, ctx_off) + [_mod_spec(2, tile_row)] + [full(a) for a in weights],
        [tok(d)], [jax.ShapeDtypeStruct((n_tiles * tm, d), F32)],
        (of, ob, y, r, ga, gm, x_lat, x_ctx, mod3, *weights))
    return x_mid


def _swiglu(h, wg, wu, wd, scale=None):
    parts = []
    for c0 in range(0, wg.shape[1], MXU_DIM):
        cols = slice(c0, c0 + MXU_DIM)
        a = _silu(_dot(h, wg[:, cols])) * _dot(h, wu[:, cols])
        parts.append((a if scale is None else a * scale).astype(BF16))
    return _dot(jnp.concatenate(parts, axis=1), wd[...])


def _ffn_kernel(x_ref, sh_ref, sc_ref, g2_ref, nw_ref, wg_ref, wu_ref, wd_ref, o_ref):
    x = x_ref[...]
    h = _norm_mod(x, nw_ref[...], sh_ref[...], sc_ref[...]).astype(BF16)
    o_ref[...] = x + g2_ref[...] * _swiglu(h, wg_ref, wu_ref, wd_ref)


def _ffn(x_all, mod3, tile_row, tm, norm_w, wg, wu, wd, sides=()):
    t, d = x_all.shape
    full = lambda a: pl.BlockSpec(a.shape, lambda i: (0,) * a.ndim)
    once = lambda a: pl.BlockSpec(a.shape, lambda i: (0,) * a.ndim, pipeline_mode=pl.Buffered(1))
    (out,), casts = _hosted_call(
        _ffn_kernel, "ffn_dense", t // tm,
        [pl.BlockSpec((tm, d), lambda i: (i, 0)),
         _mod_spec(3, tile_row), _mod_spec(4, tile_row), _mod_spec(5, tile_row),
         full(norm_w), once(wg), once(wu), once(wd)],
        [pl.BlockSpec((tm, d), lambda i: (i, 0))], [jax.ShapeDtypeStruct((t, d), F32)],
        (x_all, mod3, mod3, mod3, norm_w, wg, wu, wd), sides)
    return out, casts


def _moe_kernel(x_ref, sh_ref, sc_ref, g2_ref, nw_ref, fw_ref, wr_ref, wg_ref, wu_ref, wd_ref,
                o_ref, h_ref, gate_ref, acc_ref):
    e = pl.program_id(1)
    f = pl.program_id(2)
    lane = lax.broadcasted_iota(jnp.int32, gate_ref.shape, 1).astype(F32)

    @pl.when((e == 0) & (f == 0))
    def _():
        hf = _norm_mod(x_ref[...], nw_ref[...], sh_ref[...], sc_ref[...])
        hb = hf.astype(BF16)
        h_ref[...] = hb
        acc_ref[...] = jnp.zeros_like(acc_ref)
        h_lo = (hf - hb.astype(F32)).astype(BF16)
        wr = wr_ref[...]
        w_hi = wr.astype(BF16)
        w_lo = (wr - w_hi.astype(F32)).astype(BF16)
        logits = _dot(hb, w_hi) + _dot(h_lo, w_hi) + _dot(hb, w_lo)
        logits = jnp.where(lane < N_EXPERTS, logits, -jnp.inf)
        m1 = jnp.max(logits, axis=-1, keepdims=True)
        i1 = jnp.min(jnp.where(logits == m1, lane, float(Z_PAD)), axis=-1, keepdims=True)
        rest = jnp.where(lane == i1, -jnp.inf, logits)
        m2 = jnp.max(rest, axis=-1, keepdims=True)
        i2 = jnp.min(jnp.where(rest == m2, lane, float(Z_PAD)), axis=-1, keepdims=True)
        e2 = jnp.exp(m2 - m1)
        w1 = 1.0 / (1.0 + e2)
        gate_ref[...] = jnp.where(lane == i1, w1, 0.0) + jnp.where(lane == i2, e2 * w1, 0.0)

    h = h_ref[...]
    ge = jnp.sum(jnp.where(lane == e.astype(F32), gate_ref[...], 0.0), axis=-1, keepdims=True)
    acc_ref[...] += _swiglu(h, wg_ref.at[0], wu_ref.at[0], wd_ref.at[0], ge)

    @pl.when((e == pl.num_programs(1) - 1) & (f == pl.num_programs(2) - 1))
    def _():
        y = x_ref[...] + g2_ref[...] * acc_ref[...]
        o_ref[...] = y * lax.rsqrt(jnp.mean(y * y, axis=-1, keepdims=True) + EPS) * fw_ref[...]


def _moe(x_lat, mod3, tile_row, tm, norm_w, final_w, w_router, wg, wu, wd, tf):
    t, d = x_lat.shape
    n_e, _, f_dim = wg.shape
    mspec = lambda col: pl.BlockSpec((None, 1, D_MODEL), lambda i, e, j: (tile_row(i), 0, col))
    const = lambda a: pl.BlockSpec(a.shape, lambda i, e, j: (0,) * a.ndim)
    return pl.pallas_call(
        _moe_kernel,
        grid=(t // tm, n_e, f_dim // tf),
        in_specs=[pl.BlockSpec((tm, d), lambda i, e, j: (i, 0)),
                  mspec(3), mspec(4), mspec(5), const(norm_w), const(final_w), const(w_router),
                  pl.BlockSpec((1, d, tf), lambda i, e, j: (e, 0, j)),
                  pl.BlockSpec((1, d, tf), lambda i, e, j: (e, 0, j)),
                  pl.BlockSpec((1, tf, d), lambda i, e, j: (e, j, 0))],
        out_specs=pl.BlockSpec((tm, d), lambda i, e, j: (i, 0)),
        out_shape=jax.ShapeDtypeStruct((t, d), F32),
        scratch_shapes=[pltpu.VMEM((tm, d), BF16), pltpu.VMEM((tm, Z_PAD), F32), pltpu.VMEM((tm, d), F32)],
        compiler_params=_params(3),
        name="moe",
    )(x_lat, mod3, mod3, mod3, norm_w, final_w, w_router, wg, wu, wd)


def _w_in_kernel(w_ref, o_ref):
    w = w_ref[0]
    z0 = GLA_KEY + GLA_VAL
    z1 = z0 + 2 * GLA_RANK
    pad = jnp.zeros((w.shape[0], Z_PAD - 2 * GLA_RANK), w.dtype)
    o_ref[0] = jnp.concatenate([w[:, :z0], w[:, z1:], w[:, z0:z1], pad], axis=1).astype(BF16)


def _permute_w_in(w_in):
    depth, d, width = w_in.shape
    rows = 256
    return pl.pallas_call(
        _w_in_kernel,
        grid=(depth, d // rows),
        in_specs=[pl.BlockSpec((1, rows, width), lambda l, i: (l, i, 0))],
        out_specs=pl.BlockSpec((1, rows, IN_PERM_WIDTH), lambda l, i: (l, i, 0)),
        out_shape=jax.ShapeDtypeStruct((depth, d, IN_PERM_WIDTH), BF16),
        compiler_params=_params(2),
        name="w_in_layout",
    )(w_in)


def _low_rank_gate(lr_f, lr_b, bias_f, bias_b):
    lr = jnp.zeros((Z_PAD, 2 * GLA_KEY), F32)
    lr = lr.at[:GLA_RANK, :GLA_KEY].set(lr_f).at[GLA_RANK:2 * GLA_RANK, GLA_KEY:].set(lr_b)
    return lr.astype(BF16), jnp.concatenate([bias_f, bias_b])[None, :]


def kernel(x, c, ctx, c_ctx, w_mod, b_mod, norm1_w, norm2_w, final_norm_w, w_in, gla_lr_f, gla_lr_b, gla_bias_f, gla_bias_b, gla_norm_w, s5_a_re_f, s5_a_im_f, s5_log_dt_f, s5_a_re_b, s5_a_im_b, s5_log_dt_b, s5_b_re, s5_b_im, s5_c_re, s5_c_im, s5_d, s5_w_glu, s5_b_glu, w_gla_proj, w_s5_proj, w_out, ffn_w_gate, ffn_w_up, ffn_w_down, moe_router, moe_w_gate, moe_w_up, moe_w_down):
    batch, n_lat, d = x.shape
    n_ctx = ctx.shape[1]
    depth = w_mod.shape[0]
    assert d == D_MODEL and batch < 8 and depth == 2
    assert n_lat % n_ctx == 0 and n_ctx % GLA_CHUNK == 0 and (n_lat // GRID_W) % S5_CHUNK == 0
    t_lat, t_ctx = batch * n_lat, batch * n_ctx
    t_all = t_lat + t_ctx

    tm = math.gcd(512, t_ctx)
    assert n_lat % tm == 0
    lat_tiles_per_batch = n_lat // tm
    n_lat_tiles = t_lat // tm
    tile_row = lambda i: jnp.where(i < n_lat_tiles, i // lat_tiles_per_batch, batch)

    cond = jnp.zeros((8, d), F32).at[:batch].set(c).at[batch].set(c_ctx)
    mod = _modulation(cond, w_mod, b_mod)

    stream = (x.reshape(t_lat, d), ctx.reshape(t_ctx, d), n_lat_tiles, 0)
    out = None
    n_e, _, f_e = moe_w_gate.shape[1:]
    moe_f32 = (moe_w_gate[0].reshape(n_e * d, f_e), moe_w_up[0].reshape(n_e * d, f_e),
               moe_w_down[0].reshape(n_e * f_e, d))
    moe_bf16 = [None] * 3
    w_in_perm = _permute_w_in(w_in)
    for l in range(depth):
        last = l == depth - 1
        mod3 = mod[l].reshape(8, 1, 6 * d)
        lr, lb = _low_rank_gate(gla_lr_f[l], gla_lr_b[l], gla_bias_f[l], gla_bias_b[l])
        sides = [moe_f32[l], w_gla_proj[l], s5_w_glu[l], w_s5_proj[l], w_out[l]]
        if l == 0:
            sides += [ffn_w_gate[0], ffn_w_up[0], ffn_w_down[0]]
        (k, v, u, q, r, ga, gm, gf, gb), casts = _in_projection(
            stream, t_all, mod3, tile_row, tm, norm1_w[l][None, :], w_in_perm, l, lr, lb, sides)
        moe_bf16[l], wgp, wglu, wsp, wout = casts[:5]

        of, ob = _gla(q, k, v, gf, gb, batch, n_lat, n_ctx)

        bt, ct = _s5_bc_layout(s5_b_re[l], s5_b_im[l], s5_c_re[l], s5_c_im[l])
        y = _s5(u, s5_d[l][None, :], _s5_param_layout(s5_a_re_f[l], s5_a_im_f[l], s5_log_dt_f[l]),
                _s5_param_layout(s5_a_re_b[l], s5_a_im_b[l], s5_log_dt_b[l]), bt, ct, batch, n_lat, n_ctx)

        merge_w = (gla_norm_w[l][None, :], wgp, wglu, s5_b_glu[l][None, :], wsp, wout)
        n_tiles = n_lat_tiles if last else t_all // tm
        x_mid = _merge(of, ob, y, r, ga, gm, stream, mod3, tile_row, tm, n_tiles, merge_w)

        if l % 2 == 0:
            x_all, (moe_bf16[2],) = _ffn(x_mid, mod3, tile_row, tm, norm2_w[l][None, :], *casts[5:], [moe_f32[2]])
            stream = (x_all, x_all, n_lat_tiles, n_lat_tiles)
        else:
            i = l // 2
            tm_f = math.gcd(1024, n_lat)
            row_f = lambda j, tm_f=tm_f: j // (n_lat // tm_f)
            w_router = jnp.zeros((d, Z_PAD), F32).at[:, :N_EXPERTS].set(moe_router[i])
            res = _moe(x_mid, mod3, row_f, tm_f, norm2_w[l][None, :], final_norm_w[None, :], w_router,
                       moe_bf16[0].reshape(n_e, d, f_e), moe_bf16[1].reshape(n_e, d, f_e),
                       moe_bf16[2].reshape(n_e, f_e, d), MOE_F_TILE)
            out = res.reshape(batch, n_lat, d)
    return out
```

```python
import functools
import math

import jax
import jax.numpy as jnp
from jax import lax
from jax.experimental import pallas as pl
from jax.experimental.pallas import tpu as pltpu

F32 = jnp.float32
BF16 = jnp.bfloat16
HIGHEST = lax.Precision.HIGHEST

D_MODEL = 1024
GRID_W = 64
GLA_HEADS = 4
GLA_DK = 64
GLA_DV = 128
GLA_KEY = GLA_HEADS * GLA_DK
GLA_VAL = GLA_HEADS * GLA_DV
GLA_RANK = 16
GLA_GATE_NORM = 16.0
GLA_CHUNK = 64
GLA_BLOCK = 256
GLA_LATENT_BLOCK = 1024
S5_WIDTH = 512
S5_GROUP = 16
S5_GROUPS = S5_WIDTH // S5_GROUP
S5_STATE = 64
S5_CHUNK = 16
S5_CW = S5_CHUNK * S5_GROUP
S5_SW = 2 * S5_STATE
S5_PAD = GRID_W // 2
S5_TILE_GROUPS = 128 // S5_GROUP
S5_GROUP_SHIFT = 4
N_EXPERTS = 8
EPS = 1e-6
MXU_DIM = 256
MOE_F_TILE = 7 * MXU_DIM
SIDE_CAST_BLOCKS = 32
Z_PAD = 128

VMEM_LIMIT = 56 * 1024 * 1024

SEC_K = (0, GLA_KEY)
SEC_V = (SEC_K[1], SEC_K[1] + GLA_VAL)
SEC_U = (SEC_V[1], SEC_V[1] + S5_WIDTH)
SEC_Q = (SEC_U[1], SEC_U[1] + GLA_KEY)
SEC_R = (SEC_Q[1], SEC_Q[1] + GLA_VAL)
SEC_GA = (SEC_R[1], SEC_R[1] + D_MODEL)
SEC_GM = (SEC_GA[1], SEC_GA[1] + D_MODEL)
SEC_Z = (SEC_GM[1], SEC_GM[1] + Z_PAD)
IN_PERM_WIDTH = SEC_Z[1]


def _params(n_axes):
    return pltpu.CompilerParams(dimension_semantics=("arbitrary",) * n_axes,
                                vmem_limit_bytes=VMEM_LIMIT)


def _silu(x):
    return x * jax.nn.sigmoid(x)


def _norm_mod(x, nw, sh, sc):
    y = x * lax.rsqrt(jnp.mean(x * x, axis=-1, keepdims=True) + EPS) * nw
    return y * (1.0 + sc) + sh


def _dot(a, b):
    return jnp.dot(a, b, preferred_element_type=F32)


def _dot_nt(a, b):
    return lax.dot_general(a, b, (((1,), (1,)), ((), ())), preferred_element_type=F32)


def _dot_tn(a, b):
    return lax.dot_general(a, b, (((0,), (0,)), ((), ())), preferred_element_type=F32)


def _mod_kernel(c_ref, w_ref, b_ref, o_ref):
    s = _silu(c_ref[...]).astype(BF16)
    o_ref[0] = _dot(s, w_ref[0].astype(BF16)) + b_ref[0]


def _modulation(cond, w_mod, b_mod):
    depth, d, n = w_mod.shape
    tn = 1536
    return pl.pallas_call(
        _mod_kernel,
        grid=(depth, n // tn),
        in_specs=[pl.BlockSpec((8, d), lambda l, j: (0, 0)),
                  pl.BlockSpec((1, d, tn), lambda l, j: (l, 0, j)),
                  pl.BlockSpec((1, 1, tn), lambda l, j: (l, 0, j))],
        out_specs=pl.BlockSpec((1, 8, tn), lambda l, j: (l, 0, j)),
        out_shape=jax.ShapeDtypeStruct((depth, 8, n), F32),
        compiler_params=_params(2),
        name="modulation",
    )(cond, w_mod, b_mod.reshape(depth, 1, n))


def _hosted_call(body, name, n_steps, in_specs, out_specs, out_shape, operands, sides=()):
    specs = []
    for side in sides:
        rows, cols = side.shape
        n_blk = 1 << (min(n_steps, SIDE_CAST_BLOCKS).bit_length() - 1)
        while rows % (16 * n_blk):
            n_blk //= 2
        specs.append(pl.BlockSpec((rows // n_blk, cols), lambda i, n_blk=n_blk: (jnp.minimum(i, n_blk - 1), 0)))
    n_in, n_side = len(in_specs), len(sides)

    def kernel(*refs):
        body(*refs[:n_in], *refs[n_in + n_side:len(refs) - n_side])
        for src, dst in zip(refs[n_in:n_in + n_side], refs[len(refs) - n_side:]):
            dst[...] = src[...].astype(BF16)

    res = pl.pallas_call(
        kernel, grid=(n_steps,), in_specs=list(in_specs) + specs, out_specs=list(out_specs) + specs,
        out_shape=list(out_shape) + [jax.ShapeDtypeStruct(s.shape, BF16) for s in sides],
        compiler_params=_params(1), name=name)(*operands, *sides)
    return res[:len(out_specs)], res[len(out_specs):]


def _stream_specs(tm, n_lat_tiles, ctx_off, width=D_MODEL):
    lat = pl.BlockSpec((tm, width), lambda i: (jnp.minimum(i, n_lat_tiles - 1), 0))
    ctx = pl.BlockSpec((tm, width), lambda i: (jnp.maximum(i - n_lat_tiles, 0) + ctx_off, 0))
    return [lat, ctx]


def _stream_tile(lat_ref, ctx_ref, n_lat_tiles):
    return jnp.where(pl.program_id(0) < n_lat_tiles, lat_ref[...], ctx_ref[...])


def _inproj_kernel(xl_ref, xc_ref, sh_ref, sc_ref, nw_ref, w_ref, lr_ref, lb_ref,
                   k_ref, v_ref, u_ref, q_ref, r_ref, ga_ref, gm_ref, gf_ref, gb_ref, *, n_lat_tiles):
    x = _stream_tile(xl_ref, xc_ref, n_lat_tiles)
    h = _norm_mod(x, nw_ref[...], sh_ref[...], sc_ref[...]).astype(BF16)

    def sec(s):
        return _dot(h, w_ref[:, s[0]:s[1]])

    z = sec(SEC_Z).astype(BF16)
    pre = _dot(z, lr_ref[...]) + lb_ref[...]
    k_ref[...] = sec(SEC_K).astype(BF16)
    v_ref[...] = sec(SEC_V).astype(BF16)
    u_ref[...] = sec(SEC_U)
    q_ref[...] = (sec(SEC_Q) * GLA_DK ** -0.5).astype(BF16)
    r_ref[...] = _silu(sec(SEC_R)).astype(BF16)
    ga_ref[...] = jax.nn.sigmoid(sec(SEC_GA)).astype(BF16)
    gm_ref[...] = jax.nn.sigmoid(sec(SEC_GM)).astype(BF16)
    g = (jnp.minimum(pre, 0.0) - jnp.log(1.0 + jnp.exp(-jnp.abs(pre)))) * (1.0 / GLA_GATE_NORM)
    gf_ref[...] = g[:, :GLA_KEY]
    gb_ref[...] = g[:, GLA_KEY:]


def _mod_spec(col, tile_row):
    return pl.BlockSpec((None, 1, D_MODEL), lambda i: (tile_row(i), 0, col))


def _in_projection(stream, t, mod3, tile_row, tm, norm_w, w_perm, layer, lr, lb, sides=()):
    x_lat, x_ctx, n_lat_tiles, ctx_off = stream
    widths = (GLA_KEY, GLA_VAL, S5_WIDTH, GLA_KEY, GLA_VAL, D_MODEL, D_MODEL)
    tok = lambda w: pl.BlockSpec((tm, w), lambda i: (i, 0))
    full = lambda a: pl.BlockSpec(a.shape, lambda i: (0,) * a.ndim)
    return _hosted_call(
        functools.partial(_inproj_kernel, n_lat_tiles=n_lat_tiles), "in_projection", t // tm,
        _stream_specs(tm, n_lat_tiles, ctx_off) + [_mod_spec(0, tile_row), _mod_spec(1, tile_row),
                                                   full(norm_w),
                                                   pl.BlockSpec((None,) + w_perm.shape[1:], lambda i: (layer, 0, 0)),
                                                   full(lr), full(lb)],
        [tok(w) for w in widths] + [tok(GLA_KEY), tok(GLA_KEY)],
        [jax.ShapeDtypeStruct((t, w), F32 if i == 2 else BF16) for i, w in enumerate(widths)]
        + [jax.ShapeDtypeStruct((t, GLA_KEY), F32)] * 2,
        (x_lat, x_ctx, mod3, mod3, norm_w, w_perm, lr, lb), sides)


def _gla_kernel(*refs, resume):
    s_ref = refs[-1]
    ins = (refs[0:4], refs[4:8])
    if resume:
        s0_ref, outs = refs[8], refs[9:11]
    else:
        outs, s_out_ref = refs[8:10], refs[10]

    @pl.when(pl.program_id(1) == 0)
    def _():
        s_ref[...] = s0_ref[0] if resume else jnp.zeros_like(s_ref)

    n = GLA_BLOCK
    n_sub = refs[0].shape[0] // n
    nc = n // GLA_CHUNK
    assert nc * GLA_DK == n
    shift = GLA_CHUNK.bit_length() - 1
    row = lax.broadcasted_iota(jnp.int32, (n, n), 0)
    col = lax.broadcasted_iota(jnp.int32, (n, n), 1)
    same = (row >> shift) == (col >> shift)
    dirs = []
    for j in range(n_sub):
        for d, reverse in enumerate((False, True)):
            rows = pl.ds((n_sub - 1 - j if reverse else j) * n, n)
            dirs.append(tuple(r.at[rows, :] for r in ins[d]) + (outs[d].at[rows, :], s_ref.at[d], reverse))
    nd = range(len(dirs))
    heads = range(GLA_HEADS)
    ks = [slice(h * GLA_DK, (h + 1) * GLA_DK) for h in heads]
    vs = [slice(h * GLA_DV, (h + 1) * GLA_DV) for h in heads]

    tris, bcs = [], []
    for _, _, _, g_ref, _, _, reverse in dirs:
        tri = same & ((col >= row) if reverse else (col <= row))
        g = g_ref[...]
        g_hi = g.astype(BF16)
        g_lo = (g - g_hi.astype(F32)).astype(BF16)
        tri_b = tri.astype(BF16)
        tris.append(tri)
        bcs.append(_dot(tri_b, g_hi) + _dot(tri_b, g_lo))

    qds, kds, kls, decs, vals = [], [], [], [], []
    for (q_ref, k_ref, v_ref, _, _, _, reverse), bc in zip(dirs, bcs):
        last = 0 if reverse else GLA_CHUNK - 1
        bl3 = bc.reshape(nc, GLA_CHUNK, GLA_KEY)[:, last:last + 1, :]
        bl = jnp.broadcast_to(bl3, (nc, GLA_CHUNK, GLA_KEY)).reshape(n, GLA_KEY)
        q = q_ref[...].astype(F32)
        k = k_ref[...].astype(F32)
        qds.append((q * jnp.exp(bc)).astype(BF16))
        kds.append((k * jnp.exp(-bc)).astype(BF16))
        kls.append((k * jnp.exp(bl - bc)).astype(BF16))
        decs.append(jnp.exp(bl3))
        vals.append(v_ref[...])

    zero = jnp.zeros((), BF16)
    scores = [[_dot_nt(qds[d][:, ks[h]], kds[d][:, ks[h]]) for h in heads] for d in nd]
    incs = [[_dot_tn(vals[d][:, vs[h]], jnp.where(same, jnp.concatenate([kls[d][:, ks[h]]] * nc, axis=1), zero))
             for h in heads] for d in nd]
    intra = [[_dot(jnp.where(tris[d], scores[d][h], 0.0).astype(BF16), vals[d][:, vs[h]]) for h in heads]
             for d in nd]

    st_alls = []
    for d, (_, _, _, _, _, st_ref, reverse) in enumerate(dirs):
        order = list(range(nc))[::-1] if reverse else list(range(nc))
        st_alls.append([])
        for h in heads:
            st = st_ref[h]
            entering = {}
            for c in order:
                entering[c] = st
                st = st * decs[d][c][:, ks[h]] + incs[d][h][:, c * GLA_DK:(c + 1) * GLA_DK]
            st_ref[h] = st
            st_alls[d].append(jnp.concatenate([entering[c] for c in range(nc)], axis=1).astype(BF16))

    inter = [[_dot_nt(jnp.where(same, jnp.concatenate([qds[d][:, ks[h]]] * nc, axis=1), zero), st_alls[d][h])
              for h in heads] for d in nd]
    for d, (_, _, _, _, o_ref, _, _) in enumerate(dirs):
        o_ref[...] = jnp.concatenate([intra[d][h] + inter[d][h] for h in heads], axis=1).astype(o_ref.dtype)
    if not resume:
        s_out_ref[0] = s_ref[...]


def _gla(q, k, v, gf, gb, batch, n_lat, n_ctx):
    state = (2, GLA_HEADS, GLA_DV, GLA_DK)
    widths = (GLA_KEY, GLA_KEY, GLA_VAL, GLA_KEY)
    scratch = [pltpu.VMEM(state, F32)]
    state_spec = pl.BlockSpec((1,) + state, lambda b, i: (b, 0, 0, 0, 0))

    ctx0 = batch * n_lat // n_ctx
    of_ctx, ob_ctx, states = pl.pallas_call(
        functools.partial(_gla_kernel, resume=False),
        grid=(batch, 1),
        in_specs=[pl.BlockSpec((n_ctx, w), lambda b, i: (ctx0 + b, 0)) for w in widths] * 2,
        out_specs=[pl.BlockSpec((n_ctx, GLA_VAL), lambda b, i: (b, 0))] * 2 + [state_spec],
        out_shape=[jax.ShapeDtypeStruct((batch * n_ctx, GLA_VAL), BF16)] * 2
        + [jax.ShapeDtypeStruct((batch,) + state, F32)],
        scratch_shapes=scratch,
        compiler_params=_params(2),
        name="gla_context",
    )(q, k, v, gf, q, k, v, gb)

    blk = math.gcd(GLA_LATENT_BLOCK, n_lat)
    nlb = n_lat // blk
    fwd = lambda b, i: (b * nlb + i, 0)
    bwd = lambda b, i: (b * nlb + nlb - 1 - i, 0)
    of_lat, ob_lat = pl.pallas_call(
        functools.partial(_gla_kernel, resume=True),
        grid=(batch, nlb),
        in_specs=[pl.BlockSpec((blk, w), fwd) for w in widths] + [pl.BlockSpec((blk, w), bwd) for w in widths]
        + [state_spec],
        out_specs=[pl.BlockSpec((blk, GLA_VAL), fwd), pl.BlockSpec((blk, GLA_VAL), bwd)],
        out_shape=[jax.ShapeDtypeStruct((batch * n_lat, GLA_VAL), BF16)] * 2,
        scratch_shapes=scratch,
        compiler_params=_params(2),
        name="gla_latent",
    )(q, k, v, gf, q, k, v, gb, states)
    return of_lat, ob_lat, of_ctx, ob_ctx


def _s5_group_tables(arow, acol, log_dt, bt, ct, reverse):
    delta = jnp.exp(log_dt)
    lane_s = lax.broadcasted_iota(jnp.int32, (1, S5_SW), 1)
    sign = jnp.where(lane_s < S5_STATE, -1.0, 1.0)
    ar, ai = arow[0:1, :], arow[1:2, :]
    zr, zi = ar * delta, ai * delta

    def row_power(n):
        mag = jnp.exp(zr * n)
        return mag * jnp.cos(zi * n), mag * jnp.sin(zi * n)

    one = jnp.ones((1, 1), F32)
    l_r, l_i = row_power(one)
    den = ar * ar + ai * ai
    xr, xi = l_r - 1.0, l_i
    cr = (xr * ar + xi * ai) / den
    ci = (xi * ar - xr * ai) / den
    bdup = cr * bt + ci * (pltpu.roll(bt, S5_STATE, 1) * sign)
    bswp = pltpu.roll(bdup, S5_STATE, 1) * sign

    step = lax.broadcasted_iota(jnp.int32, (S5_CHUNK, 1), 0).astype(F32)
    er, ei = row_power(step if reverse else (S5_CHUNK - 1.0 - step))
    to_state = jnp.concatenate(
        [er[s:s + 1, :] * bdup + ei[s:s + 1, :] * bswp for s in range(S5_CHUNK)], axis=0)
    bdup_sw, bswp_sw = pltpu.roll(bdup, S5_STATE, 1), pltpu.roll(bswp, S5_STATE, 1)
    to_state_sw = jnp.concatenate(
        [er[s:s + 1, :] * bdup_sw + ei[s:s + 1, :] * bswp_sw for s in range(S5_CHUNK)], axis=0)

    zrc, zic = acol[:, 0:1] * delta, acol[:, 1:2] * delta
    lane_c = lax.broadcasted_iota(jnp.int32, (1, S5_CW), 1)
    src_row = S5_CHUNK - 1 - lax.broadcasted_iota(jnp.int32, (S5_CHUNK, 1), 0)
    spread = ((lane_c >> S5_GROUP_SHIFT) == src_row).astype(BF16)

    def to_columns(a):
        hi = a.astype(BF16)
        lo = (a - hi.astype(F32)).astype(BF16)
        return (_dot_tn(hi, spread) + _dot_tn(lo, spread))[:S5_STATE]

    pr, pi = to_columns(er), to_columns(ei)
    ctr, cti = ct[:S5_STATE, :], ct[S5_STATE:, :]
    kr, ki = ctr * pr - cti * pi, ctr * pi + cti * pr
    lam_mag = jnp.exp(zrc)
    lcr, lci = lam_mag * jnp.cos(zic), lam_mag * jnp.sin(zic)
    from_state = jnp.concatenate([kr * lcr - ki * lci, -(kr * lci + ki * lcr)], axis=0)
    krow = jnp.dot(bdup, jnp.concatenate([kr, -ki], axis=0), precision=HIGHEST,
                   preferred_element_type=F32)
    blocks = []
    for s in range(S5_CHUNK):
        if reverse:
            sh = S5_GROUP * (S5_CHUNK - 1 - s)
            blk = pltpu.roll(krow, (S5_CW - sh) % S5_CW, 1) if sh else krow
            blocks.append(jnp.where(lane_c < S5_GROUP * (s + 1), blk, 0.0))
        else:
            sh = S5_GROUP * s
            blk = pltpu.roll(krow, sh, 1) if sh else krow
            blocks.append(jnp.where(lane_c >= sh, blk, 0.0))
    toeplitz = jnp.concatenate(blocks, axis=0)

    wr, wi = row_power(one * float(S5_CHUNK))
    decay = []
    for _ in range(8):
        decay.append((wr, wi * sign))
        wr, wi = wr * wr - wi * wi, 2.0 * wr * wi
    to_state = jnp.concatenate([to_state, to_state_sw], axis=1)
    return toeplitz.astype(BF16), to_state.astype(BF16), from_state.astype(BF16), decay


def _cmul_pair(wa, wb, z):
    nb = z.shape[0] // 2
    partner = jnp.concatenate([z[nb:], z[:nb]], axis=0)
    wbs = jnp.concatenate([jnp.broadcast_to(wb, (nb, 1, S5_SW)), jnp.broadcast_to(-wb, (nb, 1, S5_SW))], axis=0)
    return wa * z + wbs * partner


def _s5_segment_scan(x, h0, decay, lvl0, scr, reverse):
    n = x.shape[1]
    lo = S5_PAD
    first, last = (n - 1, 0) if reverse else (0, n - 1)
    scr[...] = jnp.zeros_like(scr)
    scr[:, lo:lo + n, :] = x
    if h0 is not None:
        scr[:, lo + first:lo + first + 1, :] = x[:, first:first + 1, :] + _cmul_pair(*decay[lvl0], h0)
    off, lvl = 1, lvl0
    while off < n:
        src = lo + off if reverse else lo - off
        scr[:, lo:lo + n, :] = scr[:, lo:lo + n, :] + _cmul_pair(*decay[lvl], scr[:, src:src + n, :])
        off, lvl = off * 2, lvl + 1
    h_out = scr[:, lo + last:lo + last + 1, :]
    edge = lo + n if reverse else lo - 1
    if h0 is not None:
        scr[:, edge:edge + 1, :] = h0
    start = lo + 1 if reverse else lo - 1
    return scr[:, start:start + n, :], h_out


def _s5_direction(hloc, hloc_sw, decay, scr, reverse, nb, n_rq, n_cc):
    n_lat_rows = nb * n_rq * GRID_W
    pair = lambda f: jnp.concatenate([f(hloc), f(hloc_sw)], axis=0)
    lat = pair(lambda a: a[:n_lat_rows].reshape(nb, n_rq, GRID_W, S5_SW))
    ctx = pair(lambda a: a[n_lat_rows:].reshape(nb, n_cc, S5_SW))
    h_ctx_rows, h_ctx = _s5_segment_scan(ctx, None, decay, 0, scr, reverse)
    step = lambda h: _cmul_pair(*decay[0], h)
    order = list(range(n_rq))[::-1] if reverse else list(range(n_rq))
    col = lat[:, order[0]]
    for rq in order[1:]:
        col = step(col) + lat[:, rq]
    lvl_col = int(math.log2(n_rq))
    h_col, _ = _s5_segment_scan(col, h_ctx, decay, lvl_col, scr, reverse)
    starts = {order[0]: h_col}
    for prev, rq in zip(order[:-1], order[1:]):
        starts[rq] = step(starts[prev]) + lat[:, prev]
    lat_start = jnp.stack([starts[rq][:nb] for rq in range(n_rq)], axis=1)
    return jnp.concatenate([lat_start.reshape(n_lat_rows, S5_SW), h_ctx_rows[:nb].reshape(nb * n_cc, S5_SW)], axis=0)


def _s5_kernel(u_ref, d_ref, arf_ref, arb_ref, acf_ref, acb_ref, ldf_ref, ldb_ref, bt_ref, ct_ref,
               y_ref, xrot_ref, zacc_ref, scr, *, nb, n_lat, n_ctx):
    j = pl.program_id(0) % S5_TILE_GROUPS
    n_rq = n_lat // (GRID_W * S5_CHUNK)
    n_cc = n_ctx // S5_CHUNK
    lat_rows = n_rq * GRID_W
    n_rows = nb * (lat_rows + n_cc)
    lanes = S5_TILE_GROUPS * S5_GROUP
    slot = lax.broadcasted_iota(jnp.int32, (1, lanes), 1) >> S5_GROUP_SHIFT
    wrap = S5_TILE_GROUPS - 1

    runs = []
    for b in range(nb):
        for rq in range(n_rq):
            runs.append((lambda t, b=b, rq=rq: pl.ds(b * n_lat + (rq * S5_CHUNK + t) * GRID_W, GRID_W),
                         b * lat_rows + rq * GRID_W, GRID_W))
    for b in range(nb):
        runs.append((lambda t, b=b: pl.ds(nb * n_lat + b * n_ctx + t, n_cc, stride=S5_CHUNK),
                     nb * lat_rows + b * n_cc, n_cc))

    @pl.when(j == 0)
    def _():
        for tok_rows, row0, n in runs:
            for t in range(S5_CHUNK):
                x = pltpu.roll(u_ref[tok_rows(t), :], (t & wrap) * S5_GROUP, 1)
                xrot_ref[t, pl.ds(row0, n), :] = x.astype(BF16)

    halves = []
    for th in range(2):
        acc = xrot_ref[th * S5_TILE_GROUPS]
        for tp in range(1, S5_TILE_GROUPS):
            acc = jnp.where(slot == ((tp + j) & wrap), xrot_ref[th * S5_TILE_GROUPS + tp], acc)
        halves.append(pltpu.roll(acc.astype(F32), ((-j) & wrap) * S5_GROUP, 1))
    u = jnp.concatenate(halves, axis=1).astype(BF16)

    tabs = [_s5_group_tables(ar_ref[0], ac_ref[0], ld_ref[0], bt_ref[0], ct_ref[0], reverse)
            for reverse, ar_ref, ac_ref, ld_ref in ((False, arf_ref, acf_ref, ldf_ref), (True, arb_ref, acb_ref, ldb_ref))]
    hlocs = [_dot(u, tab[1]) for tab in tabs]
    y = _dot(u, tabs[0][0]) + _dot(u, tabs[1][0])
    hstarts = [_s5_direction(hlocs[d][:, :S5_SW], hlocs[d][:, S5_SW:], tabs[d][3], scr, d == 1, nb, n_rq, n_cc
                             ).astype(BF16) for d in range(2)]
    y = y + _dot(hstarts[0], tabs[0][2]) + _dot(hstarts[1], tabs[1][2])

    for th in range(2):
        rolled = pltpu.roll(y[:, th * lanes:(th + 1) * lanes], j * S5_GROUP, 1)
        for tp in range(S5_TILE_GROUPS):
            base = (th * S5_TILE_GROUPS + tp) * n_rows
            for r0 in range(0, n_rows, lanes):
                nr = min(lanes, n_rows - r0)
                own = jnp.broadcast_to(slot == ((tp + j) & wrap), (nr, lanes))
                pltpu.store(zacc_ref.at[pl.ds(base + r0, nr), :], rolled[r0:r0 + nr], mask=own)

    @pl.when(j == S5_TILE_GROUPS - 1)
    def _():
        for tok_rows, row0, n in runs:
            for t in range(S5_CHUNK):
                z = zacc_ref[pl.ds(t * n_rows + row0, n), :]
                rows = tok_rows(t)
                y_ref[rows, :] = pltpu.roll(z, ((-t) & wrap) * S5_GROUP, 1) + d_ref[...] * u_ref[rows, :]


def _s5(u, d_skip, params_f, params_b, bt, ct, nb, n_lat, n_ctx):
    t, width = u.shape
    lanes = S5_TILE_GROUPS * S5_GROUP
    n_rows = nb * (n_lat + n_ctx) // S5_CHUNK
    tile = pl.BlockSpec((t, lanes), lambda g: (0, g // S5_TILE_GROUPS))
    skip = pl.BlockSpec((1, lanes), lambda g: (0, g // S5_TILE_GROUPS))
    grp = lambda a: pl.BlockSpec((1,) + a.shape[1:], lambda g: (g, 0, 0))
    arf, acf, ldf = params_f
    arb, acb, ldb = params_b
    ops = (arf, arb, acf, acb, ldf, ldb, bt, ct)
    return pl.pallas_call(
        functools.partial(_s5_kernel, nb=nb, n_lat=n_lat, n_ctx=n_ctx),
        grid=(S5_GROUPS,),
        in_specs=[tile, skip] + [grp(a) for a in ops],
        out_specs=tile,
        out_shape=jax.ShapeDtypeStruct((t, width), F32),
        scratch_shapes=[pltpu.VMEM((S5_CHUNK, n_rows, lanes), BF16),
                        pltpu.VMEM((S5_CHUNK * n_rows, lanes), F32),
                        pltpu.VMEM((2 * nb, 2 * S5_PAD + GRID_W, S5_SW), F32)],
        compiler_params=_params(1),
        name="s5_bidir",
    )(u, d_skip, *ops)


def _s5_param_layout(a_re, a_im, log_dt):
    arow = jnp.stack([jnp.concatenate([a_re, a_re], axis=-1), jnp.concatenate([a_im, a_im], axis=-1)], axis=1)
    acol = jnp.stack([a_re, a_im], axis=-1)
    return arow, acol, log_dt[:, None, None]


def _s5_bc_layout(b_re, b_im, c_re, c_im):
    bt = jnp.concatenate([b_re.transpose(0, 2, 1), b_im.transpose(0, 2, 1)], axis=-1)
    ct = jnp.concatenate([c_re.transpose(0, 2, 1), c_im.transpose(0, 2, 1)], axis=1)
    return bt, jnp.tile(ct, (1, 1, S5_CHUNK))


def _gelu_tanh(x):
    return 0.5 * x * (1.0 + jnp.tanh(math.sqrt(2.0 / math.pi) * (x + 0.044715 * (x * x * x))))


def _merge_kernel(ofl_ref, ofc_ref, obl_ref, obc_ref, y_ref, r_ref, ga_ref, gm_ref, xl_ref, xc_ref, g1_ref,
                  gnw_ref, wgp_ref, wglu_ref, bglu_ref, wsp_ref, wout_ref, o_ref, *, n_lat_tiles):
    o = (_stream_tile(ofl_ref, ofc_ref, n_lat_tiles).astype(F32)
         + _stream_tile(obl_ref, obc_ref, n_lat_tiles).astype(F32))
    heads = []
    for h in range(GLA_HEADS):
        oh = o[:, h * GLA_DV:(h + 1) * GLA_DV]
        heads.append(oh * lax.rsqrt(jnp.mean(oh * oh, axis=-1, keepdims=True) + EPS) * gnw_ref[...])
    og = jnp.concatenate(heads, axis=1) * r_ref[...].astype(F32)
    ya = _dot(og.astype(BF16), wgp_ref[...])
    s = _gelu_tanh(y_ref[...])
    s = s * jax.nn.sigmoid(_dot(s.astype(BF16), wglu_ref[...]) + bglu_ref[...])
    yb = _dot(s.astype(BF16), wsp_ref[...])
    m = ga_ref[...].astype(F32) * ya + gm_ref[...].astype(F32) * yb
    x = _stream_tile(xl_ref, xc_ref, n_lat_tiles)
    o_ref[...] = x + g1_ref[...] * _dot(m.astype(BF16), wout_ref[...])


def _merge(gla_out, y, r, ga, gm, stream, mod3, tile_row, tm, n_tiles, weights):
    x_lat, x_ctx, n_lat_tiles, ctx_off = stream
    of_lat, ob_lat, of_ctx, ob_ctx = gla_out
    d = D_MODEL
    tok = lambda w: pl.BlockSpec((tm, w), lambda i: (i, 0))
    full = lambda a: pl.BlockSpec(a.shape, lambda i: (0,) * a.ndim)
    gla_specs = _stream_specs(tm, n_lat_tiles, 0, GLA_VAL)
    (x_mid,), _ = _hosted_call(
        functools.partial(_merge_kernel, n_lat_tiles=n_lat_tiles), "merge", n_tiles,
        gla_specs + gla_specs + [tok(S5_WIDTH), tok(GLA_VAL), tok(d), tok(d)]
        + _stream_specs(tm, n_lat_tiles, ctx_off) + [_mod_spec(2, tile_row)] + [full(a) for a in weights],
        [tok(d)], [jax.ShapeDtypeStruct((n_tiles * tm, d), F32)],
        (of_lat, of_ctx, ob_lat, ob_ctx, y, r, ga, gm, x_lat, x_ctx, mod3, *weights))
    return x_mid


def _swiglu(h, wg, wu, wd, scale=None):
    parts = []
    for c0 in range(0, wg.shape[1], MXU_DIM):
        cols = slice(c0, c0 + MXU_DIM)
        a = _silu(_dot(h, wg[:, cols])) * _dot(h, wu[:, cols])
        parts.append((a if scale is None else a * scale).astype(BF16))
    return _dot(jnp.concatenate(parts, axis=1), wd[...])


def _ffn_kernel(x_ref, sh_ref, sc_ref, g2_ref, nw_ref, wg_ref, wu_ref, wd_ref, o_ref):
    x = x_ref[...]
    h = _norm_mod(x, nw_ref[...], sh_ref[...], sc_ref[...]).astype(BF16)
    o_ref[...] = x + g2_ref[...] * _swiglu(h, wg_ref, wu_ref, wd_ref)


def _ffn(x_all, mod3, tile_row, tm, norm_w, wg, wu, wd, sides=()):
    t, d = x_all.shape
    full = lambda a: pl.BlockSpec(a.shape, lambda i: (0,) * a.ndim)
    once = lambda a: pl.BlockSpec(a.shape, lambda i: (0,) * a.ndim, pipeline_mode=pl.Buffered(1))
    (out,), casts = _hosted_call(
        _ffn_kernel, "ffn_dense", t // tm,
        [pl.BlockSpec((tm, d), lambda i: (i, 0)),
         _mod_spec(3, tile_row), _mod_spec(4, tile_row), _mod_spec(5, tile_row),
         full(norm_w), once(wg), once(wu), once(wd)],
        [pl.BlockSpec((tm, d), lambda i: (i, 0))], [jax.ShapeDtypeStruct((t, d), F32)],
        (x_all, mod3, mod3, mod3, norm_w, wg, wu, wd), sides)
    return out, casts


def _moe_kernel(x_ref, sh_ref, sc_ref, g2_ref, nw_ref, fw_ref, wr_ref, wg_ref, wu_ref, wd_ref,
                o_ref, h_ref, gate_ref, acc_ref):
    e = pl.program_id(1)
    f = pl.program_id(2)
    lane = lax.broadcasted_iota(jnp.int32, gate_ref.shape, 1).astype(F32)

    @pl.when((e == 0) & (f == 0))
    def _():
        hf = _norm_mod(x_ref[...], nw_ref[...], sh_ref[...], sc_ref[...])
        hb = hf.astype(BF16)
        h_ref[...] = hb
        acc_ref[...] = jnp.zeros_like(acc_ref)
        h_lo = (hf - hb.astype(F32)).astype(BF16)
        wr = wr_ref[...]
        w_hi = wr.astype(BF16)
        w_lo = (wr - w_hi.astype(F32)).astype(BF16)
        logits = _dot(hb, w_hi) + _dot(h_lo, w_hi) + _dot(hb, w_lo)
        logits = jnp.where(lane < N_EXPERTS, logits, -jnp.inf)
        m1 = jnp.max(logits, axis=-1, keepdims=True)
        i1 = jnp.min(jnp.where(logits == m1, lane, float(Z_PAD)), axis=-1, keepdims=True)
        rest = jnp.where(lane == i1, -jnp.inf, logits)
        m2 = jnp.max(rest, axis=-1, keepdims=True)
        i2 = jnp.min(jnp.where(rest == m2, lane, float(Z_PAD)), axis=-1, keepdims=True)
        e2 = jnp.exp(m2 - m1)
        w1 = 1.0 / (1.0 + e2)
        gate_ref[...] = jnp.where(lane == i1, w1, 0.0) + jnp.where(lane == i2, e2 * w1, 0.0)

    h = h_ref[...]
    ge = jnp.sum(jnp.where(lane == e.astype(F32), gate_ref[...], 0.0), axis=-1, keepdims=True)
    acc_ref[...] += _swiglu(h, wg_ref.at[0], wu_ref.at[0], wd_ref.at[0], ge)

    @pl.when((e == pl.num_programs(1) - 1) & (f == pl.num_programs(2) - 1))
    def _():
        y = x_ref[...] + g2_ref[...] * acc_ref[...]
        o_ref[...] = y * lax.rsqrt(jnp.mean(y * y, axis=-1, keepdims=True) + EPS) * fw_ref[...]


def _moe(x_lat, mod3, tile_row, tm, norm_w, final_w, w_router, wg, wu, wd, tf):
    t, d = x_lat.shape
    n_e, _, f_dim = wg.shape
    mspec = lambda col: pl.BlockSpec((None, 1, D_MODEL), lambda i, e, j: (tile_row(i), 0, col))
    const = lambda a: pl.BlockSpec(a.shape, lambda i, e, j: (0,) * a.ndim)
    return pl.pallas_call(
        _moe_kernel,
        grid=(t // tm, n_e, f_dim // tf),
        in_specs=[pl.BlockSpec((tm, d), lambda i, e, j: (i, 0)),
                  mspec(3), mspec(4), mspec(5), const(norm_w), const(final_w), const(w_router),
                  pl.BlockSpec((1, d, tf), lambda i, e, j: (e, 0, j)),
                  pl.BlockSpec((1, d, tf), lambda i, e, j: (e, 0, j)),
                  pl.BlockSpec((1, tf, d), lambda i, e, j: (e, j, 0))],
        out_specs=pl.BlockSpec((tm, d), lambda i, e, j: (i, 0)),
        out_shape=jax.ShapeDtypeStruct((t, d), F32),
        scratch_shapes=[pltpu.VMEM((tm, d), BF16), pltpu.VMEM((tm, Z_PAD), F32), pltpu.VMEM((tm, d), F32)],
        compiler_params=_params(3),
        name="moe",
    )(x_lat, mod3, mod3, mod3, norm_w, final_w, w_router, wg, wu, wd)


def _w_in_kernel(w_ref, o_ref):
    w = w_ref[0]
    z0 = GLA_KEY + GLA_VAL
    z1 = z0 + 2 * GLA_RANK
    pad = jnp.zeros((w.shape[0], Z_PAD - 2 * GLA_RANK), w.dtype)
    o_ref[0] = jnp.concatenate([w[:, :z0], w[:, z1:], w[:, z0:z1], pad], axis=1).astype(BF16)


def _permute_w_in(w_in):
    depth, d, width = w_in.shape
    rows = 256
    return pl.pallas_call(
        _w_in_kernel,
        grid=(depth, d // rows),
        in_specs=[pl.BlockSpec((1, rows, width), lambda l, i: (l, i, 0))],
        out_specs=pl.BlockSpec((1, rows, IN_PERM_WIDTH), lambda l, i: (l, i, 0)),
        out_shape=jax.ShapeDtypeStruct((depth, d, IN_PERM_WIDTH), BF16),
        compiler_params=_params(2),
        name="w_in_layout",
    )(w_in)


def _low_rank_gate(lr_f, lr_b, bias_f, bias_b):
    lr = jnp.zeros((Z_PAD, 2 * GLA_KEY), F32)
    lr = lr.at[:GLA_RANK, :GLA_KEY].set(lr_f).at[GLA_RANK:2 * GLA_RANK, GLA_KEY:].set(lr_b)
    return lr.astype(BF16), jnp.concatenate([bias_f, bias_b])[None, :]


def kernel(x, c, ctx, c_ctx, w_mod, b_mod, norm1_w, norm2_w, final_norm_w, w_in, gla_lr_f, gla_lr_b, gla_bias_f, gla_bias_b, gla_norm_w, s5_a_re_f, s5_a_im_f, s5_log_dt_f, s5_a_re_b, s5_a_im_b, s5_log_dt_b, s5_b_re, s5_b_im, s5_c_re, s5_c_im, s5_d, s5_w_glu, s5_b_glu, w_gla_proj, w_s5_proj, w_out, ffn_w_gate, ffn_w_up, ffn_w_down, moe_router, moe_w_gate, moe_w_up, moe_w_down):
    batch, n_lat, d = x.shape
    n_ctx = ctx.shape[1]
    depth = w_mod.shape[0]
    assert d == D_MODEL and batch < 8 and depth == 2
    assert n_lat % n_ctx == 0 and n_ctx % GLA_CHUNK == 0 and (n_lat // GRID_W) % S5_CHUNK == 0
    t_lat, t_ctx = batch * n_lat, batch * n_ctx
    t_all = t_lat + t_ctx

    tm = math.gcd(512, t_ctx)
    assert n_lat % tm == 0
    lat_tiles_per_batch = n_lat // tm
    n_lat_tiles = t_lat // tm
    tile_row = lambda i: jnp.where(i < n_lat_tiles, i // lat_tiles_per_batch, batch)

    cond = jnp.zeros((8, d), F32).at[:batch].set(c).at[batch].set(c_ctx)
    mod = _modulation(cond, w_mod, b_mod)

    stream = (x.reshape(t_lat, d), ctx.reshape(t_ctx, d), n_lat_tiles, 0)
    out = None
    n_e, _, f_e = moe_w_gate.shape[1:]
    moe_f32 = (moe_w_gate[0].reshape(n_e * d, f_e), moe_w_up[0].reshape(n_e * d, f_e),
               moe_w_down[0].reshape(n_e * f_e, d))
    moe_bf16 = [None] * 3
    w_in_perm = _permute_w_in(w_in)
    for l in range(depth):
        last = l == depth - 1
        mod3 = mod[l].reshape(8, 1, 6 * d)
        lr, lb = _low_rank_gate(gla_lr_f[l], gla_lr_b[l], gla_bias_f[l], gla_bias_b[l])
        sides = [moe_f32[l], w_gla_proj[l], s5_w_glu[l], w_s5_proj[l], w_out[l]]
        if l == 0:
            sides += [ffn_w_gate[0], ffn_w_up[0], ffn_w_down[0]]
        (k, v, u, q, r, ga, gm, gf, gb), casts = _in_projection(
            stream, t_all, mod3, tile_row, tm, norm1_w[l][None, :], w_in_perm, l, lr, lb, sides)
        moe_bf16[l], wgp, wglu, wsp, wout = casts[:5]

        gla_out = _gla(q, k, v, gf, gb, batch, n_lat, n_ctx)

        bt, ct = _s5_bc_layout(s5_b_re[l], s5_b_im[l], s5_c_re[l], s5_c_im[l])
        y = _s5(u, s5_d[l][None, :], _s5_param_layout(s5_a_re_f[l], s5_a_im_f[l], s5_log_dt_f[l]),
                _s5_param_layout(s5_a_re_b[l], s5_a_im_b[l], s5_log_dt_b[l]), bt, ct, batch, n_lat, n_ctx)

        merge_w = (gla_norm_w[l][None, :], wgp, wglu, s5_b_glu[l][None, :], wsp, wout)
        n_tiles = n_lat_tiles if last else t_all // tm
        x_mid = _merge(gla_out, y, r, ga, gm, stream, mod3, tile_row, tm, n_tiles, merge_w)

        if l % 2 == 0:
            x_all, (moe_bf16[2],) = _ffn(x_mid, mod3, tile_row, tm, norm2_w[l][None, :], *casts[5:], [moe_f32[2]])
            stream = (x_all, x_all, n_lat_tiles, n_lat_tiles)
        else:
            i = l // 2
            tm_f = math.gcd(1024, n_lat)
            row_f = lambda j, tm_f=tm_f: j // (n_lat // tm_f)
            w_router = jnp.zeros((d, Z_PAD), F32).at[:, :N_EXPERTS].set(moe_router[i])
            res = _moe(x_mid, mod3, row_f, tm_f, norm2_w[l][None, :], final_norm_w[None, :], w_router,
                       moe_bf16[0].reshape(n_e, d, f_e), moe_bf16[1].reshape(n_e, d, f_e),
                       moe_bf16[2].reshape(n_e, f_e, d), MOE_F_TILE)
            out = res.reshape(batch, n_lat, d)
    return out
```

```python
import functools
import math

import jax
import jax.numpy as jnp
from jax import lax
from jax.experimental import pallas as pl
from jax.experimental.pallas import tpu as pltpu

F32 = jnp.float32
BF16 = jnp.bfloat16
HIGHEST = lax.Precision.HIGHEST

D_MODEL = 1024
GRID_W = 64
GLA_HEADS = 4
GLA_DK = 64
GLA_DV = 128
GLA_KEY = GLA_HEADS * GLA_DK
GLA_VAL = GLA_HEADS * GLA_DV
GLA_RANK = 16
GLA_GATE_NORM = 16.0
GLA_CHUNK = 64
GLA_BLOCK = 128
GLA_LATENT_BLOCK = 1024
S5_WIDTH = 512
S5_GROUP = 16
S5_GROUPS = S5_WIDTH // S5_GROUP
S5_STATE = 64
S5_CHUNK = 16
S5_CW = S5_CHUNK * S5_GROUP
S5_SW = 2 * S5_STATE
S5_PAD = GRID_W // 2
S5_TILE_GROUPS = 128 // S5_GROUP
S5_GROUP_SHIFT = 4
N_EXPERTS = 8
EPS = 1e-6
MXU_DIM = 256
MOE_F_TILE = 7 * MXU_DIM
MOE_TOKEN_TILE = 1024
TOKEN_TILE = 512
MOD_COLS_TILE = 1536
SIDE_CAST_BLOCKS = 32
Z_PAD = 128

VMEM_LIMIT = 56 * 1024 * 1024

SEC_K = (0, GLA_KEY)
SEC_V = (SEC_K[1], SEC_K[1] + GLA_VAL)
SEC_U = (SEC_V[1], SEC_V[1] + S5_WIDTH)
SEC_Q = (SEC_U[1], SEC_U[1] + GLA_KEY)
SEC_R = (SEC_Q[1], SEC_Q[1] + GLA_VAL)
SEC_GA = (SEC_R[1], SEC_R[1] + D_MODEL)
SEC_GM = (SEC_GA[1], SEC_GA[1] + D_MODEL)
SEC_Z = (SEC_GM[1], SEC_GM[1] + Z_PAD)
IN_PERM_WIDTH = SEC_Z[1]


def _params(n_axes):
    return pltpu.CompilerParams(dimension_semantics=("arbitrary",) * n_axes,
                                vmem_limit_bytes=VMEM_LIMIT)


def _silu(x):
    return x * jax.nn.sigmoid(x)


def _norm_mod(x, nw, sh, sc):
    y = x * lax.rsqrt(jnp.mean(x * x, axis=-1, keepdims=True) + EPS) * nw
    return y * (1.0 + sc) + sh


def _dot(a, b):
    return jnp.dot(a, b, preferred_element_type=F32)


def _dot_nt(a, b):
    return lax.dot_general(a, b, (((1,), (1,)), ((), ())), preferred_element_type=F32)


def _dot_tn(a, b):
    return lax.dot_general(a, b, (((0,), (0,)), ((), ())), preferred_element_type=F32)


def _mod_kernel(c_ref, w_ref, b_ref, o_ref):
    s = _silu(c_ref[...]).astype(BF16)
    o_ref[0] = _dot(s, w_ref[0].astype(BF16)) + b_ref[0]


def _modulation(cond, w_mod, b_mod):
    depth, d, n = w_mod.shape
    tn = math.gcd(MOD_COLS_TILE, n)
    return pl.pallas_call(
        _mod_kernel,
        grid=(depth, n // tn),
        in_specs=[pl.BlockSpec((8, d), lambda l, j: (0, 0)),
                  pl.BlockSpec((1, d, tn), lambda l, j: (l, 0, j)),
                  pl.BlockSpec((1, 1, tn), lambda l, j: (l, 0, j))],
        out_specs=pl.BlockSpec((1, 8, tn), lambda l, j: (l, 0, j)),
        out_shape=jax.ShapeDtypeStruct((depth, 8, n), F32),
        compiler_params=_params(2),
        name="modulation",
    )(cond, w_mod, b_mod.reshape(depth, 1, n))


def _hosted_call(body, name, n_steps, in_specs, out_specs, out_shape, operands, sides=()):
    specs = []
    for side in sides:
        rows, cols = side.shape
        n_blk = 1 << (min(n_steps, SIDE_CAST_BLOCKS).bit_length() - 1)
        while rows % (16 * n_blk):
            n_blk //= 2
        specs.append(pl.BlockSpec((rows // n_blk, cols), lambda i, n_blk=n_blk: (jnp.minimum(i, n_blk - 1), 0)))
    n_in, n_side = len(in_specs), len(sides)

    def kernel(*refs):
        body(*refs[:n_in], *refs[n_in + n_side:len(refs) - n_side])
        for src, dst in zip(refs[n_in:n_in + n_side], refs[len(refs) - n_side:]):
            dst[...] = src[...].astype(BF16)

    res = pl.pallas_call(
        kernel, grid=(n_steps,), in_specs=list(in_specs) + specs, out_specs=list(out_specs) + specs,
        out_shape=list(out_shape) + [jax.ShapeDtypeStruct(s.shape, BF16) for s in sides],
        compiler_params=_params(1), name=name)(*operands, *sides)
    return res[:len(out_specs)], res[len(out_specs):]


def _stream_specs(tm, n_lat_tiles, ctx_off, width=D_MODEL):
    lat = pl.BlockSpec((tm, width), lambda i: (jnp.minimum(i, n_lat_tiles - 1), 0))
    ctx = pl.BlockSpec((tm, width), lambda i: (jnp.maximum(i - n_lat_tiles, 0) + ctx_off, 0))
    return [lat, ctx]


def _stream_tile(lat_ref, ctx_ref, n_lat_tiles):
    return jnp.where(pl.program_id(0) < n_lat_tiles, lat_ref[...], ctx_ref[...])


def _inproj_kernel(xl_ref, xc_ref, sh_ref, sc_ref, nw_ref, w_ref, lr_ref, lb_ref,
                   k_ref, v_ref, u_ref, q_ref, r_ref, ga_ref, gm_ref, gf_ref, gb_ref, *, n_lat_tiles):
    x = _stream_tile(xl_ref, xc_ref, n_lat_tiles)
    h = _norm_mod(x, nw_ref[...], sh_ref[...], sc_ref[...]).astype(BF16)

    def sec(s):
        return _dot(h, w_ref[:, s[0]:s[1]])

    z = sec(SEC_Z).astype(BF16)
    pre = _dot(z, lr_ref[...]) + lb_ref[...]
    k_ref[...] = sec(SEC_K).astype(BF16)
    v_ref[...] = sec(SEC_V).astype(BF16)
    u_ref[...] = sec(SEC_U)
    q_ref[...] = (sec(SEC_Q) * GLA_DK ** -0.5).astype(BF16)
    r_ref[...] = _silu(sec(SEC_R)).astype(BF16)
    ga_ref[...] = jax.nn.sigmoid(sec(SEC_GA)).astype(BF16)
    gm_ref[...] = jax.nn.sigmoid(sec(SEC_GM)).astype(BF16)
    g = (jnp.minimum(pre, 0.0) - jnp.log(1.0 + jnp.exp(-jnp.abs(pre)))) * (1.0 / GLA_GATE_NORM)
    gf_ref[...] = g[:, :GLA_KEY]
    gb_ref[...] = g[:, GLA_KEY:]


def _mod_spec(col, tile_row):
    return pl.BlockSpec((None, 1, D_MODEL), lambda i: (tile_row(i), 0, col))


def _in_projection(stream, t, mod3, tile_row, tm, norm_w, w_perm, layer, lr, lb, sides=()):
    x_lat, x_ctx, n_lat_tiles, ctx_off = stream
    widths = (GLA_KEY, GLA_VAL, S5_WIDTH, GLA_KEY, GLA_VAL, D_MODEL, D_MODEL)
    tok = lambda w: pl.BlockSpec((tm, w), lambda i: (i, 0))
    full = lambda a: pl.BlockSpec(a.shape, lambda i: (0,) * a.ndim)
    return _hosted_call(
        functools.partial(_inproj_kernel, n_lat_tiles=n_lat_tiles), "in_projection", t // tm,
        _stream_specs(tm, n_lat_tiles, ctx_off) + [_mod_spec(0, tile_row), _mod_spec(1, tile_row),
                                                   full(norm_w),
                                                   pl.BlockSpec((None,) + w_perm.shape[1:], lambda i: (layer, 0, 0)),
                                                   full(lr), full(lb)],
        [tok(w) for w in widths] + [tok(GLA_KEY), tok(GLA_KEY)],
        [jax.ShapeDtypeStruct((t, w), F32 if i == 2 else BF16) for i, w in enumerate(widths)]
        + [jax.ShapeDtypeStruct((t, GLA_KEY), F32)] * 2,
        (x_lat, x_ctx, mod3, mod3, norm_w, w_perm, lr, lb), sides)


def _gla_kernel(*refs, resume):
    s_ref = refs[-1]
    ins = (refs[0:4], refs[4:8])
    if resume:
        s0_ref, outs = refs[8], refs[9:11]
    else:
        outs, s_out_ref = refs[8:10], refs[10]

    @pl.when(pl.program_id(1) == 0)
    def _():
        s_ref[...] = s0_ref[0] if resume else jnp.zeros_like(s_ref)

    n = GLA_BLOCK
    n_sub = refs[0].shape[0] // n
    nc = n // GLA_CHUNK
    assert nc * GLA_DK == n
    shift = GLA_CHUNK.bit_length() - 1
    row = lax.broadcasted_iota(jnp.int32, (n, n), 0)
    col = lax.broadcasted_iota(jnp.int32, (n, n), 1)
    same = (row >> shift) == (col >> shift)
    dirs = []
    for j in range(n_sub):
        for d, reverse in enumerate((False, True)):
            rows = pl.ds((n_sub - 1 - j if reverse else j) * n, n)
            dirs.append(tuple(r.at[rows, :] for r in ins[d]) + (outs[d].at[rows, :], s_ref.at[d], reverse))
    nd = range(len(dirs))
    heads = range(GLA_HEADS)
    ks = [slice(h * GLA_DK, (h + 1) * GLA_DK) for h in heads]
    vs = [slice(h * GLA_DV, (h + 1) * GLA_DV) for h in heads]

    tris, bcs = [], []
    for _, _, _, g_ref, _, _, reverse in dirs:
        tri = same & ((col >= row) if reverse else (col <= row))
        g = g_ref[...]
        g_hi = g.astype(BF16)
        g_lo = (g - g_hi.astype(F32)).astype(BF16)
        tri_b = tri.astype(BF16)
        tris.append(tri)
        bcs.append(_dot(tri_b, g_hi) + _dot(tri_b, g_lo))

    qds, kds, kls, decs, vals = [], [], [], [], []
    for (q_ref, k_ref, v_ref, _, _, _, reverse), bc in zip(dirs, bcs):
        last = 0 if reverse else GLA_CHUNK - 1
        bl3 = bc.reshape(nc, GLA_CHUNK, GLA_KEY)[:, last:last + 1, :]
        bl = jnp.broadcast_to(bl3, (nc, GLA_CHUNK, GLA_KEY)).reshape(n, GLA_KEY)
        q = q_ref[...].astype(F32)
        k = k_ref[...].astype(F32)
        qds.append((q * jnp.exp(bc)).astype(BF16))
        kds.append((k * jnp.exp(-bc)).astype(BF16))
        kls.append((k * jnp.exp(bl - bc)).astype(BF16))
        decs.append(jnp.exp(bl3))
        vals.append(v_ref[...])

    zero = jnp.zeros((), BF16)
    scores = [[_dot_nt(qds[d][:, ks[h]], kds[d][:, ks[h]]) for h in heads] for d in nd]
    incs = [[_dot_tn(vals[d][:, vs[h]], jnp.where(same, jnp.concatenate([kls[d][:, ks[h]]] * nc, axis=1), zero))
             for h in heads] for d in nd]
    intra = [[_dot(jnp.where(tris[d], scores[d][h], 0.0).astype(BF16), vals[d][:, vs[h]]) for h in heads]
             for d in nd]

    st_alls = []
    for d, (_, _, _, _, _, st_ref, reverse) in enumerate(dirs):
        order = list(range(nc))[::-1] if reverse else list(range(nc))
        st_alls.append([])
        for h in heads:
            st = st_ref[h]
            entering = {}
            for c in order:
                entering[c] = st
                st = st * decs[d][c][:, ks[h]] + incs[d][h][:, c * GLA_DK:(c + 1) * GLA_DK]
            st_ref[h] = st
            st_alls[d].append(jnp.concatenate([entering[c] for c in range(nc)], axis=1).astype(BF16))

    inter = [[_dot_nt(jnp.where(same, jnp.concatenate([qds[d][:, ks[h]]] * nc, axis=1), zero), st_alls[d][h])
              for h in heads] for d in nd]
    for d, (_, _, _, _, o_ref, _, _) in enumerate(dirs):
        o_ref[...] = jnp.concatenate([intra[d][h] + inter[d][h] for h in heads], axis=1).astype(o_ref.dtype)
    if not resume:
        s_out_ref[0] = s_ref[...]


def _gla(q, k, v, gf, gb, batch, n_lat, n_ctx):
    state = (2, GLA_HEADS, GLA_DV, GLA_DK)
    widths = (GLA_KEY, GLA_KEY, GLA_VAL, GLA_KEY)
    scratch = [pltpu.VMEM(state, F32)]
    state_spec = pl.BlockSpec((1,) + state, lambda b, i: (b, 0, 0, 0, 0))

    ctx0 = batch * n_lat // n_ctx
    of_ctx, ob_ctx, states = pl.pallas_call(
        functools.partial(_gla_kernel, resume=False),
        grid=(batch, 1),
        in_specs=[pl.BlockSpec((n_ctx, w), lambda b, i: (ctx0 + b, 0)) for w in widths] * 2,
        out_specs=[pl.BlockSpec((n_ctx, GLA_VAL), lambda b, i: (b, 0))] * 2 + [state_spec],
        out_shape=[jax.ShapeDtypeStruct((batch * n_ctx, GLA_VAL), BF16)] * 2
        + [jax.ShapeDtypeStruct((batch,) + state, F32)],
        scratch_shapes=scratch,
        compiler_params=_params(2),
        name="gla_context",
    )(q, k, v, gf, q, k, v, gb)

    blk = math.gcd(GLA_LATENT_BLOCK, n_lat)
    nlb = n_lat // blk
    fwd = lambda b, i: (b * nlb + i, 0)
    bwd = lambda b, i: (b * nlb + nlb - 1 - i, 0)
    of_lat, ob_lat = pl.pallas_call(
        functools.partial(_gla_kernel, resume=True),
        grid=(batch, nlb),
        in_specs=[pl.BlockSpec((blk, w), fwd) for w in widths] + [pl.BlockSpec((blk, w), bwd) for w in widths]
        + [state_spec],
        out_specs=[pl.BlockSpec((blk, GLA_VAL), fwd), pl.BlockSpec((blk, GLA_VAL), bwd)],
        out_shape=[jax.ShapeDtypeStruct((batch * n_lat, GLA_VAL), BF16)] * 2,
        scratch_shapes=scratch,
        compiler_params=_params(2),
        name="gla_latent",
    )(q, k, v, gf, q, k, v, gb, states)
    return of_lat, ob_lat, of_ctx, ob_ctx


def _s5_group_tables(arow, acol, log_dt, bt, ct, reverse):
    delta = jnp.exp(log_dt)
    lane_s = lax.broadcasted_iota(jnp.int32, (1, S5_SW), 1)
    sign = jnp.where(lane_s < S5_STATE, -1.0, 1.0)
    ar, ai = arow[0:1, :], arow[1:2, :]
    zr, zi = ar * delta, ai * delta

    def row_power(n):
        mag = jnp.exp(zr * n)
        return mag * jnp.cos(zi * n), mag * jnp.sin(zi * n)

    one = jnp.ones((1, 1), F32)
    l_r, l_i = row_power(one)
    den = ar * ar + ai * ai
    xr, xi = l_r - 1.0, l_i
    cr = (xr * ar + xi * ai) / den
    ci = (xi * ar - xr * ai) / den
    bdup = cr * bt + ci * (pltpu.roll(bt, S5_STATE, 1) * sign)
    bswp = pltpu.roll(bdup, S5_STATE, 1) * sign

    step = lax.broadcasted_iota(jnp.int32, (S5_CHUNK, 1), 0).astype(F32)
    er, ei = row_power(step if reverse else (S5_CHUNK - 1.0 - step))
    to_state = jnp.concatenate(
        [er[s:s + 1, :] * bdup + ei[s:s + 1, :] * bswp for s in range(S5_CHUNK)], axis=0)
    bdup_sw, bswp_sw = pltpu.roll(bdup, S5_STATE, 1), pltpu.roll(bswp, S5_STATE, 1)
    to_state_sw = jnp.concatenate(
        [er[s:s + 1, :] * bdup_sw + ei[s:s + 1, :] * bswp_sw for s in range(S5_CHUNK)], axis=0)

    zrc, zic = acol[:, 0:1] * delta, acol[:, 1:2] * delta
    lane_c = lax.broadcasted_iota(jnp.int32, (1, S5_CW), 1)
    src_row = S5_CHUNK - 1 - lax.broadcasted_iota(jnp.int32, (S5_CHUNK, 1), 0)
    spread = ((lane_c >> S5_GROUP_SHIFT) == src_row).astype(BF16)

    def to_columns(a):
        hi = a.astype(BF16)
        lo = (a - hi.astype(F32)).astype(BF16)
        return (_dot_tn(hi, spread) + _dot_tn(lo, spread))[:S5_STATE]

    pr, pi = to_columns(er), to_columns(ei)
    ctr, cti = ct[:S5_STATE, :], ct[S5_STATE:, :]
    kr, ki = ctr * pr - cti * pi, ctr * pi + cti * pr
    lam_mag = jnp.exp(zrc)
    lcr, lci = lam_mag * jnp.cos(zic), lam_mag * jnp.sin(zic)
    from_state = jnp.concatenate([kr * lcr - ki * lci, -(kr * lci + ki * lcr)], axis=0)
    krow = jnp.dot(bdup, jnp.concatenate([kr, -ki], axis=0), precision=HIGHEST,
                   preferred_element_type=F32)
    blocks = []
    for s in range(S5_CHUNK):
        if reverse:
            sh = S5_GROUP * (S5_CHUNK - 1 - s)
            blk = pltpu.roll(krow, (S5_CW - sh) % S5_CW, 1) if sh else krow
            blocks.append(jnp.where(lane_c < S5_GROUP * (s + 1), blk, 0.0))
        else:
            sh = S5_GROUP * s
            blk = pltpu.roll(krow, sh, 1) if sh else krow
            blocks.append(jnp.where(lane_c >= sh, blk, 0.0))
    toeplitz = jnp.concatenate(blocks, axis=0)

    wr, wi = row_power(one * float(S5_CHUNK))
    decay = []
    for _ in range(8):
        decay.append((wr, wi * sign))
        wr, wi = wr * wr - wi * wi, 2.0 * wr * wi
    to_state = jnp.concatenate([to_state, to_state_sw], axis=1)
    return toeplitz.astype(BF16), to_state.astype(BF16), from_state.astype(BF16), decay


def _cmul_pair(wa, wb, z):
    nb = z.shape[0] // 2
    partner = jnp.concatenate([z[nb:], z[:nb]], axis=0)
    wbs = jnp.concatenate([jnp.broadcast_to(wb, (nb, 1, S5_SW)), jnp.broadcast_to(-wb, (nb, 1, S5_SW))], axis=0)
    return wa * z + wbs * partner


def _s5_segment_scan(x, h0, decay, lvl0, scr, reverse):
    n = x.shape[1]
    lo = S5_PAD
    first, last = (n - 1, 0) if reverse else (0, n - 1)
    scr[...] = jnp.zeros_like(scr)
    scr[:, lo:lo + n, :] = x
    if h0 is not None:
        scr[:, lo + first:lo + first + 1, :] = x[:, first:first + 1, :] + _cmul_pair(*decay[lvl0], h0)
    off, lvl = 1, lvl0
    while off < n:
        src = lo + off if reverse else lo - off
        scr[:, lo:lo + n, :] = scr[:, lo:lo + n, :] + _cmul_pair(*decay[lvl], scr[:, src:src + n, :])
        off, lvl = off * 2, lvl + 1
    h_out = scr[:, lo + last:lo + last + 1, :]
    edge = lo + n if reverse else lo - 1
    if h0 is not None:
        scr[:, edge:edge + 1, :] = h0
    start = lo + 1 if reverse else lo - 1
    return scr[:, start:start + n, :], h_out


def _s5_direction(hloc, hloc_sw, decay, scr, reverse, nb, n_rq, n_cc):
    n_lat_rows = nb * n_rq * GRID_W
    pair = lambda f: jnp.concatenate([f(hloc), f(hloc_sw)], axis=0)
    lat = pair(lambda a: a[:n_lat_rows].reshape(nb, n_rq, GRID_W, S5_SW))
    ctx = pair(lambda a: a[n_lat_rows:].reshape(nb, n_cc, S5_SW))
    h_ctx_rows, h_ctx = _s5_segment_scan(ctx, None, decay, 0, scr, reverse)
    step = lambda h: _cmul_pair(*decay[0], h)
    order = list(range(n_rq))[::-1] if reverse else list(range(n_rq))
    col = lat[:, order[0]]
    for rq in order[1:]:
        col = step(col) + lat[:, rq]
    lvl_col = int(math.log2(n_rq))
    h_col, _ = _s5_segment_scan(col, h_ctx, decay, lvl_col, scr, reverse)
    starts = {order[0]: h_col}
    for prev, rq in zip(order[:-1], order[1:]):
        starts[rq] = step(starts[prev]) + lat[:, prev]
    lat_start = jnp.stack([starts[rq][:nb] for rq in range(n_rq)], axis=1)
    return jnp.concatenate([lat_start.reshape(n_lat_rows, S5_SW), h_ctx_rows[:nb].reshape(nb * n_cc, S5_SW)], axis=0)


def _s5_kernel(u_ref, d_ref, arf_ref, arb_ref, acf_ref, acb_ref, ldf_ref, ldb_ref, bt_ref, ct_ref,
               y_ref, xrot_ref, zacc_ref, scr, *, nb, n_lat, n_ctx):
    j = pl.program_id(0) % S5_TILE_GROUPS
    n_rq = n_lat // (GRID_W * S5_CHUNK)
    n_cc = n_ctx // S5_CHUNK
    lat_rows = n_rq * GRID_W
    n_rows = nb * (lat_rows + n_cc)
    lanes = S5_TILE_GROUPS * S5_GROUP
    slot = lax.broadcasted_iota(jnp.int32, (1, lanes), 1) >> S5_GROUP_SHIFT
    wrap = S5_TILE_GROUPS - 1

    runs = []
    for b in range(nb):
        for rq in range(n_rq):
            runs.append((lambda t, b=b, rq=rq: pl.ds(b * n_lat + (rq * S5_CHUNK + t) * GRID_W, GRID_W),
                         b * lat_rows + rq * GRID_W, GRID_W))
    for b in range(nb):
        runs.append((lambda t, b=b: pl.ds(nb * n_lat + b * n_ctx + t, n_cc, stride=S5_CHUNK),
                     nb * lat_rows + b * n_cc, n_cc))

    @pl.when(j == 0)
    def _():
        for tok_rows, row0, n in runs:
            for t in range(S5_CHUNK):
                x = pltpu.roll(u_ref[tok_rows(t), :], (t & wrap) * S5_GROUP, 1)
                xrot_ref[t, pl.ds(row0, n), :] = x.astype(BF16)

    halves = []
    for th in range(2):
        acc = xrot_ref[th * S5_TILE_GROUPS]
        for tp in range(1, S5_TILE_GROUPS):
            acc = jnp.where(slot == ((tp + j) & wrap), xrot_ref[th * S5_TILE_GROUPS + tp], acc)
        halves.append(pltpu.roll(acc.astype(F32), ((-j) & wrap) * S5_GROUP, 1))
    u = jnp.concatenate(halves, axis=1).astype(BF16)

    tabs = [_s5_group_tables(ar_ref[0], ac_ref[0], ld_ref[0], bt_ref[0], ct_ref[0], reverse)
            for reverse, ar_ref, ac_ref, ld_ref in ((False, arf_ref, acf_ref, ldf_ref), (True, arb_ref, acb_ref, ldb_ref))]
    hlocs = [_dot(u, tab[1]) for tab in tabs]
    y = _dot(u, tabs[0][0]) + _dot(u, tabs[1][0])
    hstarts = [_s5_direction(hlocs[d][:, :S5_SW], hlocs[d][:, S5_SW:], tabs[d][3], scr, d == 1, nb, n_rq, n_cc
                             ).astype(BF16) for d in range(2)]
    y = y + _dot(hstarts[0], tabs[0][2]) + _dot(hstarts[1], tabs[1][2])

    for th in range(2):
        rolled = pltpu.roll(y[:, th * lanes:(th + 1) * lanes], j * S5_GROUP, 1)
        for tp in range(S5_TILE_GROUPS):
            base = (th * S5_TILE_GROUPS + tp) * n_rows
            for r0 in range(0, n_rows, lanes):
                nr = min(lanes, n_rows - r0)
                own = jnp.broadcast_to(slot == ((tp + j) & wrap), (nr, lanes))
                pltpu.store(zacc_ref.at[pl.ds(base + r0, nr), :], rolled[r0:r0 + nr], mask=own)

    @pl.when(j == S5_TILE_GROUPS - 1)
    def _():
        for tok_rows, row0, n in runs:
            for t in range(S5_CHUNK):
                z = zacc_ref[pl.ds(t * n_rows + row0, n), :]
                rows = tok_rows(t)
                y_ref[rows, :] = pltpu.roll(z, ((-t) & wrap) * S5_GROUP, 1) + d_ref[...] * u_ref[rows, :]


def _s5(u, d_skip, params_f, params_b, bt, ct, nb, n_lat, n_ctx):
    t, width = u.shape
    lanes = S5_TILE_GROUPS * S5_GROUP
    n_rows = nb * (n_lat + n_ctx) // S5_CHUNK
    tile = pl.BlockSpec((t, lanes), lambda g: (0, g // S5_TILE_GROUPS))
    skip = pl.BlockSpec((1, lanes), lambda g: (0, g // S5_TILE_GROUPS))
    grp = lambda a: pl.BlockSpec((1,) + a.shape[1:], lambda g: (g, 0, 0))
    arf, acf, ldf = params_f
    arb, acb, ldb = params_b
    ops = (arf, arb, acf, acb, ldf, ldb, bt, ct)
    return pl.pallas_call(
        functools.partial(_s5_kernel, nb=nb, n_lat=n_lat, n_ctx=n_ctx),
        grid=(S5_GROUPS,),
        in_specs=[tile, skip] + [grp(a) for a in ops],
        out_specs=tile,
        out_shape=jax.ShapeDtypeStruct((t, width), F32),
        scratch_shapes=[pltpu.VMEM((S5_CHUNK, n_rows, lanes), BF16),
                        pltpu.VMEM((S5_CHUNK * n_rows, lanes), F32),
                        pltpu.VMEM((2 * nb, 2 * S5_PAD + GRID_W, S5_SW), F32)],
        compiler_params=_params(1),
        name="s5_bidir",
    )(u, d_skip, *ops)


def _s5_param_layout(a_re, a_im, log_dt):
    arow = jnp.stack([jnp.concatenate([a_re, a_re], axis=-1), jnp.concatenate([a_im, a_im], axis=-1)], axis=1)
    acol = jnp.stack([a_re, a_im], axis=-1)
    return arow, acol, log_dt[:, None, None]


def _s5_bc_layout(b_re, b_im, c_re, c_im):
    bt = jnp.concatenate([b_re.transpose(0, 2, 1), b_im.transpose(0, 2, 1)], axis=-1)
    ct = jnp.concatenate([c_re.transpose(0, 2, 1), c_im.transpose(0, 2, 1)], axis=1)
    return bt, jnp.tile(ct, (1, 1, S5_CHUNK))


def _gelu_tanh(x):
    return 0.5 * x * (1.0 + jnp.tanh(math.sqrt(2.0 / math.pi) * (x + 0.044715 * (x * x * x))))


def _merge_kernel(ofl_ref, ofc_ref, obl_ref, obc_ref, y_ref, r_ref, ga_ref, gm_ref, xl_ref, xc_ref, g1_ref,
                  gnw_ref, wgp_ref, wglu_ref, bglu_ref, wsp_ref, wout_ref, o_ref, *, n_lat_tiles):
    o = (_stream_tile(ofl_ref, ofc_ref, n_lat_tiles).astype(F32)
         + _stream_tile(obl_ref, obc_ref, n_lat_tiles).astype(F32))
    heads = []
    for h in range(GLA_HEADS):
        oh = o[:, h * GLA_DV:(h + 1) * GLA_DV]
        heads.append(oh * lax.rsqrt(jnp.mean(oh * oh, axis=-1, keepdims=True) + EPS) * gnw_ref[...])
    og = jnp.concatenate(heads, axis=1) * r_ref[...].astype(F32)
    ya = _dot(og.astype(BF16), wgp_ref[...])
    s = _gelu_tanh(y_ref[...])
    s = s * jax.nn.sigmoid(_dot(s.astype(BF16), wglu_ref[...]) + bglu_ref[...])
    yb = _dot(s.astype(BF16), wsp_ref[...])
    m = ga_ref[...].astype(F32) * ya + gm_ref[...].astype(F32) * yb
    x = _stream_tile(xl_ref, xc_ref, n_lat_tiles)
    o_ref[...] = x + g1_ref[...] * _dot(m.astype(BF16), wout_ref[...])


def _merge(gla_out, y, r, ga, gm, stream, mod3, tile_row, tm, n_tiles, weights):
    x_lat, x_ctx, n_lat_tiles, ctx_off = stream
    of_lat, ob_lat, of_ctx, ob_ctx = gla_out
    d = D_MODEL
    tok = lambda w: pl.BlockSpec((tm, w), lambda i: (i, 0))
    full = lambda a: pl.BlockSpec(a.shape, lambda i: (0,) * a.ndim)
    gla_specs = _stream_specs(tm, n_lat_tiles, 0, GLA_VAL)
    (x_mid,), _ = _hosted_call(
        functools.partial(_merge_kernel, n_lat_tiles=n_lat_tiles), "merge", n_tiles,
        gla_specs + gla_specs + [tok(S5_WIDTH), tok(GLA_VAL), tok(d), tok(d)]
        + _stream_specs(tm, n_lat_tiles, ctx_off) + [_mod_spec(2, tile_row)] + [full(a) for a in weights],
        [tok(d)], [jax.ShapeDtypeStruct((n_tiles * tm, d), F32)],
        (of_lat, of_ctx, ob_lat, ob_ctx, y, r, ga, gm, x_lat, x_ctx, mod3, *weights))
    return x_mid


def _swiglu(h, wg, wu, wd, scale=None):
    parts = []
    for c0 in range(0, wg.shape[1], MXU_DIM):
        cols = slice(c0, c0 + MXU_DIM)
        a = _silu(_dot(h, wg[:, cols])) * _dot(h, wu[:, cols])
        parts.append((a if scale is None else a * scale).astype(BF16))
    return _dot(jnp.concatenate(parts, axis=1), wd[...])


def _ffn_kernel(x_ref, sh_ref, sc_ref, g2_ref, nw_ref, wg_ref, wu_ref, wd_ref, o_ref):
    x = x_ref[...]
    h = _norm_mod(x, nw_ref[...], sh_ref[...], sc_ref[...]).astype(BF16)
    o_ref[...] = x + g2_ref[...] * _swiglu(h, wg_ref, wu_ref, wd_ref)


def _ffn(x_all, mod3, tile_row, tm, norm_w, wg, wu, wd, sides=()):
    t, d = x_all.shape
    full = lambda a: pl.BlockSpec(a.shape, lambda i: (0,) * a.ndim)
    once = lambda a: pl.BlockSpec(a.shape, lambda i: (0,) * a.ndim, pipeline_mode=pl.Buffered(1))
    (out,), casts = _hosted_call(
        _ffn_kernel, "ffn_dense", t // tm,
        [pl.BlockSpec((tm, d), lambda i: (i, 0)),
         _mod_spec(3, tile_row), _mod_spec(4, tile_row), _mod_spec(5, tile_row),
         full(norm_w), once(wg), once(wu), once(wd)],
        [pl.BlockSpec((tm, d), lambda i: (i, 0))], [jax.ShapeDtypeStruct((t, d), F32)],
        (x_all, mod3, mod3, mod3, norm_w, wg, wu, wd), sides)
    return out, casts


def _moe_kernel(x_ref, sh_ref, sc_ref, g2_ref, nw_ref, fw_ref, wr_ref, wg_ref, wu_ref, wd_ref,
                o_ref, h_ref, gate_ref, acc_ref):
    e = pl.program_id(1)
    f = pl.program_id(2)
    lane = lax.broadcasted_iota(jnp.int32, gate_ref.shape, 1).astype(F32)

    @pl.when((e == 0) & (f == 0))
    def _():
        hf = _norm_mod(x_ref[...], nw_ref[...], sh_ref[...], sc_ref[...])
        hb = hf.astype(BF16)
        h_ref[...] = hb
        acc_ref[...] = jnp.zeros_like(acc_ref)
        h_lo = (hf - hb.astype(F32)).astype(BF16)
        wr = wr_ref[...]
        w_hi = wr.astype(BF16)
        w_lo = (wr - w_hi.astype(F32)).astype(BF16)
        logits = _dot(hb, w_hi) + _dot(h_lo, w_hi) + _dot(hb, w_lo)
        logits = jnp.where(lane < N_EXPERTS, logits, -jnp.inf)
        m1 = jnp.max(logits, axis=-1, keepdims=True)
        i1 = jnp.min(jnp.where(logits == m1, lane, float(Z_PAD)), axis=-1, keepdims=True)
        rest = jnp.where(lane == i1, -jnp.inf, logits)
        m2 = jnp.max(rest, axis=-1, keepdims=True)
        i2 = jnp.min(jnp.where(rest == m2, lane, float(Z_PAD)), axis=-1, keepdims=True)
        e2 = jnp.exp(m2 - m1)
        w1 = 1.0 / (1.0 + e2)
        gate_ref[...] = jnp.where(lane == i1, w1, 0.0) + jnp.where(lane == i2, e2 * w1, 0.0)

    h = h_ref[...]
    ge = jnp.sum(jnp.where(lane == e.astype(F32), gate_ref[...], 0.0), axis=-1, keepdims=True)
    acc_ref[...] += _swiglu(h, wg_ref.at[0], wu_ref.at[0], wd_ref.at[0], ge)

    @pl.when((e == pl.num_programs(1) - 1) & (f == pl.num_programs(2) - 1))
    def _():
        y = x_ref[...] + g2_ref[...] * acc_ref[...]
        o_ref[...] = y * lax.rsqrt(jnp.mean(y * y, axis=-1, keepdims=True) + EPS) * fw_ref[...]


def _moe(x_lat, mod3, tile_row, tm, norm_w, final_w, w_router, wg, wu, wd, tf):
    t, d = x_lat.shape
    n_e, _, f_dim = wg.shape
    mspec = lambda col: pl.BlockSpec((None, 1, D_MODEL), lambda i, e, j: (tile_row(i), 0, col))
    const = lambda a: pl.BlockSpec(a.shape, lambda i, e, j: (0,) * a.ndim)
    return pl.pallas_call(
        _moe_kernel,
        grid=(t // tm, n_e, f_dim // tf),
        in_specs=[pl.BlockSpec((tm, d), lambda i, e, j: (i, 0)),
                  mspec(3), mspec(4), mspec(5), const(norm_w), const(final_w), const(w_router),
                  pl.BlockSpec((1, d, tf), lambda i, e, j: (e, 0, j)),
                  pl.BlockSpec((1, d, tf), lambda i, e, j: (e, 0, j)),
                  pl.BlockSpec((1, tf, d), lambda i, e, j: (e, j, 0))],
        out_specs=pl.BlockSpec((tm, d), lambda i, e, j: (i, 0)),
        out_shape=jax.ShapeDtypeStruct((t, d), F32),
        scratch_shapes=[pltpu.VMEM((tm, d), BF16), pltpu.VMEM((tm, Z_PAD), F32), pltpu.VMEM((tm, d), F32)],
        compiler_params=_params(3),
        name="moe",
    )(x_lat, mod3, mod3, mod3, norm_w, final_w, w_router, wg, wu, wd)


def _w_in_kernel(w_ref, o_ref):
    w = w_ref[0]
    z0 = GLA_KEY + GLA_VAL
    z1 = z0 + 2 * GLA_RANK
    pad = jnp.zeros((w.shape[0], Z_PAD - 2 * GLA_RANK), w.dtype)
    o_ref[0] = jnp.concatenate([w[:, :z0], w[:, z1:], w[:, z0:z1], pad], axis=1).astype(BF16)


def _permute_w_in(w_in):
    depth, d, width = w_in.shape
    rows = math.gcd(MXU_DIM, d)
    return pl.pallas_call(
        _w_in_kernel,
        grid=(depth, d // rows),
        in_specs=[pl.BlockSpec((1, rows, width), lambda l, i: (l, i, 0))],
        out_specs=pl.BlockSpec((1, rows, IN_PERM_WIDTH), lambda l, i: (l, i, 0)),
        out_shape=jax.ShapeDtypeStruct((depth, d, IN_PERM_WIDTH), BF16),
        compiler_params=_params(2),
        name="w_in_layout",
    )(w_in)


def _low_rank_gate(lr_f, lr_b, bias_f, bias_b):
    lr = jnp.zeros((Z_PAD, 2 * GLA_KEY), F32)
    lr = lr.at[:GLA_RANK, :GLA_KEY].set(lr_f).at[GLA_RANK:2 * GLA_RANK, GLA_KEY:].set(lr_b)
    return lr.astype(BF16), jnp.concatenate([bias_f, bias_b])[None, :]


def kernel(x, c, ctx, c_ctx, w_mod, b_mod, norm1_w, norm2_w, final_norm_w, w_in, gla_lr_f, gla_lr_b, gla_bias_f, gla_bias_b, gla_norm_w, s5_a_re_f, s5_a_im_f, s5_log_dt_f, s5_a_re_b, s5_a_im_b, s5_log_dt_b, s5_b_re, s5_b_im, s5_c_re, s5_c_im, s5_d, s5_w_glu, s5_b_glu, w_gla_proj, w_s5_proj, w_out, ffn_w_gate, ffn_w_up, ffn_w_down, moe_router, moe_w_gate, moe_w_up, moe_w_down):
    batch, n_lat, d = x.shape
    n_ctx = ctx.shape[1]
    depth = w_mod.shape[0]
    assert d == D_MODEL and batch < 8 and depth == 2
    assert n_lat % n_ctx == 0 and n_ctx % GLA_CHUNK == 0 and (n_lat // GRID_W) % S5_CHUNK == 0
    t_lat, t_ctx = batch * n_lat, batch * n_ctx
    t_all = t_lat + t_ctx

    tm = math.gcd(TOKEN_TILE, t_ctx)
    assert n_lat % tm == 0
    lat_tiles_per_batch = n_lat // tm
    n_lat_tiles = t_lat // tm
    tile_row = lambda i: jnp.where(i < n_lat_tiles, i // lat_tiles_per_batch, batch)

    cond = jnp.zeros((8, d), F32).at[:batch].set(c).at[batch].set(c_ctx)
    mod = _modulation(cond, w_mod, b_mod)

    stream = (x.reshape(t_lat, d), ctx.reshape(t_ctx, d), n_lat_tiles, 0)
    out = None
    n_e, _, f_e = moe_w_gate.shape[1:]
    moe_f32 = (moe_w_gate[0].reshape(n_e * d, f_e), moe_w_up[0].reshape(n_e * d, f_e),
               moe_w_down[0].reshape(n_e * f_e, d))
    moe_bf16 = [None] * 3
    w_in_perm = _permute_w_in(w_in)
    for l in range(depth):
        last = l == depth - 1
        mod3 = mod[l].reshape(8, 1, 6 * d)
        lr, lb = _low_rank_gate(gla_lr_f[l], gla_lr_b[l], gla_bias_f[l], gla_bias_b[l])
        sides = [moe_f32[l], w_gla_proj[l], s5_w_glu[l], w_s5_proj[l], w_out[l]]
        if l == 0:
            sides += [ffn_w_gate[0], ffn_w_up[0], ffn_w_down[0]]
        (k, v, u, q, r, ga, gm, gf, gb), casts = _in_projection(
            stream, t_all, mod3, tile_row, tm, norm1_w[l][None, :], w_in_perm, l, lr, lb, sides)
        moe_bf16[l], wgp, wglu, wsp, wout = casts[:5]

        gla_out = _gla(q, k, v, gf, gb, batch, n_lat, n_ctx)

        bt, ct = _s5_bc_layout(s5_b_re[l], s5_b_im[l], s5_c_re[l], s5_c_im[l])
        y = _s5(u, s5_d[l][None, :], _s5_param_layout(s5_a_re_f[l], s5_a_im_f[l], s5_log_dt_f[l]),
                _s5_param_layout(s5_a_re_b[l], s5_a_im_b[l], s5_log_dt_b[l]), bt, ct, batch, n_lat, n_ctx)

        merge_w = (gla_norm_w[l][None, :], wgp, wglu, s5_b_glu[l][None, :], wsp, wout)
        n_tiles = n_lat_tiles if last else t_all // tm
        x_mid = _merge(gla_out, y, r, ga, gm, stream, mod3, tile_row, tm, n_tiles, merge_w)

        if l % 2 == 0:
            x_all, (moe_bf16[2],) = _ffn(x_mid, mod3, tile_row, tm, norm2_w[l][None, :], *casts[5:], [moe_f32[2]])
            stream = (x_all, x_all, n_lat_tiles, n_lat_tiles)
        else:
            i = l // 2
            tm_f = math.gcd(MOE_TOKEN_TILE, n_lat)
            row_f = lambda j, tm_f=tm_f: j // (n_lat // tm_f)
            w_router = jnp.zeros((d, Z_PAD), F32).at[:, :N_EXPERTS].set(moe_router[i])
            res = _moe(x_mid, mod3, row_f, tm_f, norm2_w[l][None, :], final_norm_w[None, :], w_router,
                       moe_bf16[0].reshape(n_e, d, f_e), moe_bf16[1].reshape(n_e, d, f_e),
                       moe_bf16[2].reshape(n_e, f_e, d), MOE_F_TILE)
            out = res.reshape(batch, n_lat, d)
    return out
```
